```python
import jax, jax.numpy as jnp
from jax import lax
import numpy as np

D_MODEL = 1024
BATCH = 8
SEQ = 4096
DEPTH = 4

PLE_DIM = 256
D_FF = 2816
POOL_WINDOWS = (2, 4, 8, 16)
POOL_GROUP = D_MODEL // 8
POOL_WIDTH = len(POOL_WINDOWS) * POOL_GROUP
SGU_HEADS = 4
SGU_HEAD_DIM = D_MODEL // 8
SGU_WIDTH = SGU_HEADS * SGU_HEAD_DIM
CHUNK = 128
CONV_WIDTH = D_MODEL // 2
CONV_KERNEL = 31
N_BRANCH = 3
OFF_POOL = 0
OFF_U = OFF_POOL + POOL_WIDTH
OFF_V = OFF_U + SGU_WIDTH
OFF_GLU_A = OFF_V + SGU_WIDTH
OFF_GLU_B = OFF_GLU_A + CONV_WIDTH
OFF_GATES = OFF_GLU_B + CONV_WIDTH
IN_COLS = OFF_GATES + N_BRANCH * D_MODEL
EPS = 1e-6

kernel_name = "hybrid_pool_sgu_conformer_gated_trunk"


def rms_norm(x, g):
    xf = x.astype(jnp.float32)
    y = xf * lax.rsqrt(jnp.mean(xf * xf, axis=-1, keepdims=True) + EPS)
    return (y * g.astype(jnp.float32)).astype(x.dtype)


def layer_norm(x, g, b):
    xf = x.astype(jnp.float32)
    mu = jnp.mean(xf, axis=-1, keepdims=True)
    xc = xf - mu
    y = xc * lax.rsqrt(jnp.mean(xc * xc, axis=-1, keepdims=True) + EPS)
    return (y * g.astype(jnp.float32) + b.astype(jnp.float32)).astype(x.dtype)


def swiglu(x, w_gate, w_up, w_down):
    return (jax.nn.silu(x @ w_gate) * (x @ w_up)) @ w_down


def pool_mixer(xa, w_grp, scale):
    b_, s_, _ = xa.shape
    xf = xa.astype(jnp.float32).reshape(b_, s_, len(POOL_WINDOWS), POOL_GROUP)
    csum = jnp.cumsum(xf, axis=1)
    pos = jnp.arange(s_)
    outs = []
    for gi, w in enumerate(POOL_WINDOWS):
        cg = csum[:, :, gi]
        shifted = jnp.pad(cg, ((0, 0), (w, 0), (0, 0)))[:, :s_]
        count = jnp.minimum(pos + 1, w).astype(jnp.float32)[None, :, None]
        outs.append((cg - shifted) / count - xf[:, :, gi])
    pooled = jnp.stack(outs, axis=2).astype(xa.dtype)
    y = jnp.einsum('bsgi,gio->bsgo', pooled, w_grp)
    return y.reshape(b_, s_, POOL_WIDTH) * scale


def sgu_mixer(u, v, ln_g, ln_b, w_s, b_s):
    b_, s_, _ = u.shape
    u = jax.nn.gelu(u, approximate=False)
    v = layer_norm(jax.nn.gelu(v, approximate=False), ln_g, ln_b)
    vc = v.reshape(b_, s_ // CHUNK, CHUNK, SGU_HEADS, SGU_HEAD_DIM)
    causal = jnp.tril(jnp.ones((CHUNK, CHUNK), dtype=bool))
    w = jnp.where(causal[None], w_s, jnp.zeros_like(w_s))
    s = jnp.einsum('hts,bcshd->bcthd', w, vc) + b_s.T[None, None, :, :, None]
    return u * s.reshape(b_, s_, SGU_WIDTH)


def conv_mixer(a, gate, dw_k, dw_b, ln_g, ln_b):
    xg = a * jax.nn.sigmoid(gate)
    y = lax.conv_general_dilated(
        xg, dw_k, window_strides=(1,), padding=[(CONV_KERNEL - 1, 0)],
        dimension_numbers=('NWC', 'WIO', 'NWC'), feature_group_count=CONV_WIDTH) + dw_b
    return jax.nn.silu(layer_norm(y, ln_g, ln_b))


def _fwd_setup_inputs(seed: int = 0) -> dict:
    key = jax.random.key(seed)
    ks = iter(jax.random.split(key, 40))
    L = DEPTH

    def w(shape, fan_in):
        return jax.random.normal(next(ks), shape, jnp.float32) * (fan_in ** -0.5)

    def gain(shape):
        return 1.0 + 0.05 * jax.random.normal(next(ks), shape, jnp.float32)

    def bias(shape):
        return 0.02 * jax.random.normal(next(ks), shape, jnp.float32)

    return {
        "x": jax.random.normal(next(ks), (BATCH, SEQ, D_MODEL), jnp.float32),
        "p": jax.random.normal(next(ks), (DEPTH, BATCH, SEQ, PLE_DIM), jnp.float32),
        "ffn1_pre_g": gain((L, D_MODEL)),
        "ffn1_w_gate": w((L, D_MODEL, D_FF), D_MODEL),
        "ffn1_w_up": w((L, D_MODEL, D_FF), D_MODEL),
        "ffn1_w_down": w((L, D_FF, D_MODEL), D_FF),
        "ffn1_post_g": gain((L, D_MODEL)),
        "mix_pre_g": gain((L, D_MODEL)),
        "w_in": w((L, D_MODEL, IN_COLS), D_MODEL),
        "pool_w": w((L, len(POOL_WINDOWS), POOL_GROUP, POOL_GROUP), POOL_GROUP),
        "pool_scale": gain((L, POOL_WIDTH)),
        "w_pool_out": w((L, POOL_WIDTH, D_MODEL), POOL_WIDTH),
        "sgu_ln_g": gain((L, SGU_WIDTH)),
        "sgu_ln_b": bias((L, SGU_WIDTH)),
        "sgu_w_s": w((L, SGU_HEADS, CHUNK, CHUNK), CHUNK),
        "sgu_b_s": 1.0 + 0.1 * jax.random.normal(next(ks), (L, SGU_HEADS, CHUNK), jnp.float32),
        "w_sgu_out": w((L, SGU_WIDTH, D_MODEL), SGU_WIDTH),
        "conv_dw_k": w((L, CONV_KERNEL, 1, CONV_WIDTH), CONV_KERNEL),
        "conv_dw_b": bias((L, CONV_WIDTH)),
        "conv_ln_g": gain((L, CONV_WIDTH)),
        "conv_ln_b": bias((L, CONV_WIDTH)),
        "w_conv_out": w((L, CONV_WIDTH, D_MODEL), CONV_WIDTH),
        "w_out": w((L, D_MODEL, D_MODEL), D_MODEL),
        "mix_post_g": gain((L, D_MODEL)),
        "ffn2_pre_g": gain((L, D_MODEL)),
        "ffn2_w_gate": w((L, D_MODEL, D_FF), D_MODEL),
        "ffn2_w_up": w((L, D_MODEL, D_FF), D_MODEL),
        "ffn2_w_down": w((L, D_FF, D_MODEL), D_FF),
        "ffn2_post_g": gain((L, D_MODEL)),
        "ple_w_proj": w((L, PLE_DIM, D_MODEL), PLE_DIM),
        "ple_pre_g": gain((L, D_MODEL)),
        "ple_w_gate": w((L, D_MODEL, D_MODEL), D_MODEL),
        "ple_post_g": gain((L, D_MODEL)),
    }


def _fwd_reference(x, p, ffn1_pre_g, ffn1_w_gate, ffn1_w_up, ffn1_w_down, ffn1_post_g,
              mix_pre_g, w_in, pool_w, pool_scale, w_pool_out,
              sgu_ln_g, sgu_ln_b, sgu_w_s, sgu_b_s, w_sgu_out,
              conv_dw_k, conv_dw_b, conv_ln_g, conv_ln_b, w_conv_out,
              w_out, mix_post_g,
              ffn2_pre_g, ffn2_w_gate, ffn2_w_up, ffn2_w_down, ffn2_post_g,
              ple_w_proj, ple_pre_g, ple_w_gate, ple_post_g):
    h = x
    b_, s_, _ = x.shape
    for i in range(DEPTH):
        f = swiglu(rms_norm(h, ffn1_pre_g[i]), ffn1_w_gate[i], ffn1_w_up[i], ffn1_w_down[i])
        h = h + 0.5 * rms_norm(f, ffn1_post_g[i])

        n = rms_norm(h, mix_pre_g[i])
        z = n @ w_in[i]
        z_pool = z[..., OFF_POOL:OFF_U]
        z_u = z[..., OFF_U:OFF_V]
        z_v = z[..., OFF_V:OFF_GLU_A]
        z_a = z[..., OFF_GLU_A:OFF_GLU_B]
        z_b = z[..., OFF_GLU_B:OFF_GATES]
        gates = jax.nn.sigmoid(z[..., OFF_GATES:]).reshape(b_, s_, N_BRANCH, D_MODEL)

        y_pool = pool_mixer(z_pool, pool_w[i], pool_scale[i]) @ w_pool_out[i]
        y_sgu = sgu_mixer(z_u, z_v, sgu_ln_g[i], sgu_ln_b[i], sgu_w_s[i], sgu_b_s[i]) @ w_sgu_out[i]
        y_conv = conv_mixer(z_a, z_b, conv_dw_k[i], conv_dw_b[i],
                            conv_ln_g[i], conv_ln_b[i]) @ w_conv_out[i]
        merged = gates[:, :, 0] * y_pool + gates[:, :, 1] * y_sgu + gates[:, :, 2] * y_conv
        h = h + rms_norm(merged @ w_out[i], mix_post_g[i])

        f = swiglu(rms_norm(h, ffn2_pre_g[i]), ffn2_w_gate[i], ffn2_w_up[i], ffn2_w_down[i])
        h = h + 0.5 * rms_norm(f, ffn2_post_g[i])

        e = p[i] @ ple_w_proj[i]
        g = jax.nn.sigmoid(rms_norm(h, ple_pre_g[i]) @ ple_w_gate[i])
        h = h + rms_norm(g * e, ple_post_g[i])
    return h


import jax as _jax
import jax.numpy as _jnp

TWIN_FORMAT = 'train_step'
FWD_PARAMS = ['x', 'p', 'ffn1_pre_g', 'ffn1_w_gate', 'ffn1_w_up', 'ffn1_w_down', 'ffn1_post_g', 'mix_pre_g', 'w_in', 'pool_w', 'pool_scale', 'w_pool_out', 'sgu_ln_g', 'sgu_ln_b', 'sgu_w_s', 'sgu_b_s', 'w_sgu_out', 'conv_dw_k', 'conv_dw_b', 'conv_ln_g', 'conv_ln_b', 'w_conv_out', 'w_out', 'mix_post_g', 'ffn2_pre_g', 'ffn2_w_gate', 'ffn2_w_up', 'ffn2_w_down', 'ffn2_post_g', 'ple_w_proj', 'ple_pre_g', 'ple_w_gate', 'ple_post_g']
TWIN_WEIGHTS = ['ffn1_pre_g', 'ffn1_w_gate', 'ffn1_w_up', 'ffn1_w_down', 'ffn1_post_g', 'mix_pre_g', 'w_in', 'pool_w', 'pool_scale', 'w_pool_out', 'sgu_ln_g', 'sgu_ln_b', 'sgu_w_s', 'sgu_b_s', 'w_sgu_out', 'conv_dw_k', 'conv_dw_b', 'conv_ln_g', 'conv_ln_b', 'w_conv_out', 'w_out', 'mix_post_g', 'ffn2_pre_g', 'ffn2_w_gate', 'ffn2_w_up', 'ffn2_w_down', 'ffn2_post_g', 'ple_w_proj', 'ple_pre_g', 'ple_w_gate', 'ple_post_g']
TWIN_DIFF_INPUT = 'x'
TWIN_INPUTS = ['x', 'p', 'ffn1_pre_g', 'ffn1_w_gate', 'ffn1_w_up', 'ffn1_w_down', 'ffn1_post_g', 'mix_pre_g', 'w_in', 'pool_w', 'pool_scale', 'w_pool_out', 'sgu_ln_g', 'sgu_ln_b', 'sgu_w_s', 'sgu_b_s', 'w_sgu_out', 'conv_dw_k', 'conv_dw_b', 'conv_ln_g', 'conv_ln_b', 'w_conv_out', 'w_out', 'mix_post_g', 'ffn2_pre_g', 'ffn2_w_gate', 'ffn2_w_up', 'ffn2_w_down', 'ffn2_post_g', 'ple_w_proj', 'ple_pre_g', 'ple_w_gate', 'ple_post_g', 'loss_target', 'm_ffn1_pre_g', 'm_ffn1_w_gate', 'm_ffn1_w_up', 'm_ffn1_w_down', 'm_ffn1_post_g', 'm_mix_pre_g', 'm_w_in', 'm_pool_w', 'm_pool_scale', 'm_w_pool_out', 'm_sgu_ln_g', 'm_sgu_ln_b', 'm_sgu_w_s', 'm_sgu_b_s', 'm_w_sgu_out', 'm_conv_dw_k', 'm_conv_dw_b', 'm_conv_ln_g', 'm_conv_ln_b', 'm_w_conv_out', 'm_w_out', 'm_mix_post_g', 'm_ffn2_pre_g', 'm_ffn2_w_gate', 'm_ffn2_w_up', 'm_ffn2_w_down', 'm_ffn2_post_g', 'm_ple_w_proj', 'm_ple_pre_g', 'm_ple_w_gate', 'm_ple_post_g', 'v_ffn1_pre_g', 'v_ffn1_w_gate', 'v_ffn1_w_up', 'v_ffn1_w_down', 'v_ffn1_post_g', 'v_mix_pre_g', 'v_w_in', 'v_pool_w', 'v_pool_scale', 'v_w_pool_out', 'v_sgu_ln_g', 'v_sgu_ln_b', 'v_sgu_w_s', 'v_sgu_b_s', 'v_w_sgu_out', 'v_conv_dw_k', 'v_conv_dw_b', 'v_conv_ln_g', 'v_conv_ln_b', 'v_w_conv_out', 'v_w_out', 'v_mix_post_g', 'v_ffn2_pre_g', 'v_ffn2_w_gate', 'v_ffn2_w_up', 'v_ffn2_w_down', 'v_ffn2_post_g', 'v_ple_w_proj', 'v_ple_pre_g', 'v_ple_w_gate', 'v_ple_post_g']
TWIN_OUTPUTS = ['loss', 'grad_x', 'grad_ffn1_pre_g', 'grad_ffn1_w_gate', 'grad_ffn1_w_up', 'grad_ffn1_w_down', 'grad_ffn1_post_g', 'grad_mix_pre_g', 'grad_w_in', 'grad_pool_w', 'grad_pool_scale', 'grad_w_pool_out', 'grad_sgu_ln_g', 'grad_sgu_ln_b', 'grad_sgu_w_s', 'grad_sgu_b_s', 'grad_w_sgu_out', 'grad_conv_dw_k', 'grad_conv_dw_b', 'grad_conv_ln_g', 'grad_conv_ln_b', 'grad_w_conv_out', 'grad_w_out', 'grad_mix_post_g', 'grad_ffn2_pre_g', 'grad_ffn2_w_gate', 'grad_ffn2_w_up', 'grad_ffn2_w_down', 'grad_ffn2_post_g', 'grad_ple_w_proj', 'grad_ple_pre_g', 'grad_ple_w_gate', 'grad_ple_post_g', 'delta_ffn1_pre_g', 'delta_ffn1_w_gate', 'delta_ffn1_w_up', 'delta_ffn1_w_down', 'delta_ffn1_post_g', 'delta_mix_pre_g', 'delta_w_in', 'delta_pool_w', 'delta_pool_scale', 'delta_w_pool_out', 'delta_sgu_ln_g', 'delta_sgu_ln_b', 'delta_sgu_w_s', 'delta_sgu_b_s', 'delta_w_sgu_out', 'delta_conv_dw_k', 'delta_conv_dw_b', 'delta_conv_ln_g', 'delta_conv_ln_b', 'delta_w_conv_out', 'delta_w_out', 'delta_mix_post_g', 'delta_ffn2_pre_g', 'delta_ffn2_w_gate', 'delta_ffn2_w_up', 'delta_ffn2_w_down', 'delta_ffn2_post_g', 'delta_ple_w_proj', 'delta_ple_pre_g', 'delta_ple_w_gate', 'delta_ple_post_g', 'new_m_ffn1_pre_g', 'new_m_ffn1_w_gate', 'new_m_ffn1_w_up', 'new_m_ffn1_w_down', 'new_m_ffn1_post_g', 'new_m_mix_pre_g', 'new_m_w_in', 'new_m_pool_w', 'new_m_pool_scale', 'new_m_w_pool_out', 'new_m_sgu_ln_g', 'new_m_sgu_ln_b', 'new_m_sgu_w_s', 'new_m_sgu_b_s', 'new_m_w_sgu_out', 'new_m_conv_dw_k', 'new_m_conv_dw_b', 'new_m_conv_ln_g', 'new_m_conv_ln_b', 'new_m_w_conv_out', 'new_m_w_out', 'new_m_mix_post_g', 'new_m_ffn2_pre_g', 'new_m_ffn2_w_gate', 'new_m_ffn2_w_up', 'new_m_ffn2_w_down', 'new_m_ffn2_post_g', 'new_m_ple_w_proj', 'new_m_ple_pre_g', 'new_m_ple_w_gate', 'new_m_ple_post_g', 'new_v_ffn1_pre_g', 'new_v_ffn1_w_gate', 'new_v_ffn1_w_up', 'new_v_ffn1_w_down', 'new_v_ffn1_post_g', 'new_v_mix_pre_g', 'new_v_w_in', 'new_v_pool_w', 'new_v_pool_scale', 'new_v_w_pool_out', 'new_v_sgu_ln_g', 'new_v_sgu_ln_b', 'new_v_sgu_w_s', 'new_v_sgu_b_s', 'new_v_w_sgu_out', 'new_v_conv_dw_k', 'new_v_conv_dw_b', 'new_v_conv_ln_g', 'new_v_conv_ln_b', 'new_v_w_conv_out', 'new_v_w_out', 'new_v_mix_post_g', 'new_v_ffn2_pre_g', 'new_v_ffn2_w_gate', 'new_v_ffn2_w_up', 'new_v_ffn2_w_down', 'new_v_ffn2_post_g', 'new_v_ple_w_proj', 'new_v_ple_pre_g', 'new_v_ple_w_gate', 'new_v_ple_post_g']
TWIN_LEAF_KINDS = {'loss': 'loss', 'grad_x': 'grad_x', 'grad_ffn1_pre_g': 'grad_w', 'grad_ffn1_w_gate': 'grad_w', 'grad_ffn1_w_up': 'grad_w', 'grad_ffn1_w_down': 'grad_w', 'grad_ffn1_post_g': 'grad_w', 'grad_mix_pre_g': 'grad_w', 'grad_w_in': 'grad_w', 'grad_pool_w': 'grad_w', 'grad_pool_scale': 'grad_w', 'grad_w_pool_out': 'grad_w', 'grad_sgu_ln_g': 'grad_w', 'grad_sgu_ln_b': 'grad_w', 'grad_sgu_w_s': 'grad_w', 'grad_sgu_b_s': 'grad_w', 'grad_w_sgu_out': 'grad_w', 'grad_conv_dw_k': 'grad_w', 'grad_conv_dw_b': 'grad_w', 'grad_conv_ln_g': 'grad_w', 'grad_conv_ln_b': 'grad_w', 'grad_w_conv_out': 'grad_w', 'grad_w_out': 'grad_w', 'grad_mix_post_g': 'grad_w', 'grad_ffn2_pre_g': 'grad_w', 'grad_ffn2_w_gate': 'grad_w', 'grad_ffn2_w_up': 'grad_w', 'grad_ffn2_w_down': 'grad_w', 'grad_ffn2_post_g': 'grad_w', 'grad_ple_w_proj': 'grad_w', 'grad_ple_pre_g': 'grad_w', 'grad_ple_w_gate': 'grad_w', 'grad_ple_post_g': 'grad_w', 'delta_ffn1_pre_g': 'delta_w', 'delta_ffn1_w_gate': 'delta_w', 'delta_ffn1_w_up': 'delta_w', 'delta_ffn1_w_down': 'delta_w', 'delta_ffn1_post_g': 'delta_w', 'delta_mix_pre_g': 'delta_w', 'delta_w_in': 'delta_w', 'delta_pool_w': 'delta_w', 'delta_pool_scale': 'delta_w', 'delta_w_pool_out': 'delta_w', 'delta_sgu_ln_g': 'delta_w', 'delta_sgu_ln_b': 'delta_w', 'delta_sgu_w_s': 'delta_w', 'delta_sgu_b_s': 'delta_w', 'delta_w_sgu_out': 'delta_w', 'delta_conv_dw_k': 'delta_w', 'delta_conv_dw_b': 'delta_w', 'delta_conv_ln_g': 'delta_w', 'delta_conv_ln_b': 'delta_w', 'delta_w_conv_out': 'delta_w', 'delta_w_out': 'delta_w', 'delta_mix_post_g': 'delta_w', 'delta_ffn2_pre_g': 'delta_w', 'delta_ffn2_w_gate': 'delta_w', 'delta_ffn2_w_up': 'delta_w', 'delta_ffn2_w_down': 'delta_w', 'delta_ffn2_post_g': 'delta_w', 'delta_ple_w_proj': 'delta_w', 'delta_ple_pre_g': 'delta_w', 'delta_ple_w_gate': 'delta_w', 'delta_ple_post_g': 'delta_w', 'new_m_ffn1_pre_g': 'new_m', 'new_m_ffn1_w_gate': 'new_m', 'new_m_ffn1_w_up': 'new_m', 'new_m_ffn1_w_down': 'new_m', 'new_m_ffn1_post_g': 'new_m', 'new_m_mix_pre_g': 'new_m', 'new_m_w_in': 'new_m', 'new_m_pool_w': 'new_m', 'new_m_pool_scale': 'new_m', 'new_m_w_pool_out': 'new_m', 'new_m_sgu_ln_g': 'new_m', 'new_m_sgu_ln_b': 'new_m', 'new_m_sgu_w_s': 'new_m', 'new_m_sgu_b_s': 'new_m', 'new_m_w_sgu_out': 'new_m', 'new_m_conv_dw_k': 'new_m', 'new_m_conv_dw_b': 'new_m', 'new_m_conv_ln_g': 'new_m', 'new_m_conv_ln_b': 'new_m', 'new_m_w_conv_out': 'new_m', 'new_m_w_out': 'new_m', 'new_m_mix_post_g': 'new_m', 'new_m_ffn2_pre_g': 'new_m', 'new_m_ffn2_w_gate': 'new_m', 'new_m_ffn2_w_up': 'new_m', 'new_m_ffn2_w_down': 'new_m', 'new_m_ffn2_post_g': 'new_m', 'new_m_ple_w_proj': 'new_m', 'new_m_ple_pre_g': 'new_m', 'new_m_ple_w_gate': 'new_m', 'new_m_ple_post_g': 'new_m', 'new_v_ffn1_pre_g': 'new_v', 'new_v_ffn1_w_gate': 'new_v', 'new_v_ffn1_w_up': 'new_v', 'new_v_ffn1_w_down': 'new_v', 'new_v_ffn1_post_g': 'new_v', 'new_v_mix_pre_g': 'new_v', 'new_v_w_in': 'new_v', 'new_v_pool_w': 'new_v', 'new_v_pool_scale': 'new_v', 'new_v_w_pool_out': 'new_v', 'new_v_sgu_ln_g': 'new_v', 'new_v_sgu_ln_b': 'new_v', 'new_v_sgu_w_s': 'new_v', 'new_v_sgu_b_s': 'new_v', 'new_v_w_sgu_out': 'new_v', 'new_v_conv_dw_k': 'new_v', 'new_v_conv_dw_b': 'new_v', 'new_v_conv_ln_g': 'new_v', 'new_v_conv_ln_b': 'new_v', 'new_v_w_conv_out': 'new_v', 'new_v_w_out': 'new_v', 'new_v_mix_post_g': 'new_v', 'new_v_ffn2_pre_g': 'new_v', 'new_v_ffn2_w_gate': 'new_v', 'new_v_ffn2_w_up': 'new_v', 'new_v_ffn2_w_down': 'new_v', 'new_v_ffn2_post_g': 'new_v', 'new_v_ple_w_proj': 'new_v', 'new_v_ple_pre_g': 'new_v', 'new_v_ple_w_gate': 'new_v', 'new_v_ple_post_g': 'new_v'}


def _forward(args):
    return _fwd_reference(*[args[k] for k in FWD_PARAMS])


def _output_shape():
    out = _jax.eval_shape(lambda: _forward(_fwd_setup_inputs(0)))
    return out.shape, out.dtype

N_MICROBATCH = 1
ADAM_LR = 0.001
ADAM_B1 = 0.9
ADAM_B2 = 0.999
ADAM_EPS = 1e-08
ADAM_WD = 0.01
ADAM_STEP = 10
PER_EXAMPLE_BATCH_AXIS = {'x': 0, 'p': 1, 'loss_target': 0}
SHARED_INPUTS = []
_WEIGHT_DTYPES = {'ffn1_pre_g': _jnp.float32, 'ffn1_w_gate': _jnp.float32, 'ffn1_w_up': _jnp.float32, 'ffn1_w_down': _jnp.float32, 'ffn1_post_g': _jnp.float32, 'mix_pre_g': _jnp.float32, 'w_in': _jnp.float32, 'pool_w': _jnp.float32, 'pool_scale': _jnp.float32, 'w_pool_out': _jnp.float32, 'sgu_ln_g': _jnp.float32, 'sgu_ln_b': _jnp.float32, 'sgu_w_s': _jnp.float32, 'sgu_b_s': _jnp.float32, 'w_sgu_out': _jnp.float32, 'conv_dw_k': _jnp.float32, 'conv_dw_b': _jnp.float32, 'conv_ln_g': _jnp.float32, 'conv_ln_b': _jnp.float32, 'w_conv_out': _jnp.float32, 'w_out': _jnp.float32, 'mix_post_g': _jnp.float32, 'ffn2_pre_g': _jnp.float32, 'ffn2_w_gate': _jnp.float32, 'ffn2_w_up': _jnp.float32, 'ffn2_w_down': _jnp.float32, 'ffn2_post_g': _jnp.float32, 'ple_w_proj': _jnp.float32, 'ple_pre_g': _jnp.float32, 'ple_w_gate': _jnp.float32, 'ple_post_g': _jnp.float32}
MOMENT_SCALE = {'ffn1_pre_g': 1.000113e+00, 'ffn1_w_gate': 3.894843e-01, 'ffn1_w_up': 4.328422e-01, 'ffn1_w_down': 7.201189e-01, 'ffn1_post_g': 7.439551e+00, 'mix_pre_g': 1.681049e+00, 'w_in': 6.964353e-01, 'pool_w': 1.660367e+00, 'pool_scale': 1.780892e+00, 'w_pool_out': 1.184940e+00, 'sgu_ln_g': 4.545538e-01, 'sgu_ln_b': 4.899680e-01, 'sgu_w_s': 4.326459e-01, 'sgu_b_s': 6.355974e-01, 'w_sgu_out': 2.620896e+00, 'conv_dw_k': 9.383483e-01, 'conv_dw_b': 1.142113e+01, 'conv_ln_g': 4.453951e+00, 'conv_ln_b': 7.110368e+00, 'w_conv_out': 1.870824e+00, 'w_out': 3.292028e+00, 'mix_post_g': 3.225027e+01, 'ffn2_pre_g': 8.525554e-01, 'ffn2_w_gate': 3.069533e-01, 'ffn2_w_up': 3.985483e-01, 'ffn2_w_down': 6.645511e-01, 'ffn2_post_g': 7.906830e+00, 'ple_w_proj': 9.657320e-01, 'ple_pre_g': 3.320879e-01, 'ple_w_gate': 3.352148e-01, 'ple_post_g': 3.199753e+01}


def _to_microbatches(a, axis):
    t = _jnp.moveaxis(a, axis, 0)
    t = t.reshape((N_MICROBATCH, t.shape[0] // N_MICROBATCH) + t.shape[1:])
    return _jnp.moveaxis(t, 1, axis + 1)


def setup_inputs(seed: int = 0) -> dict:
    inp = _fwd_setup_inputs(seed)
    key = _jax.random.fold_in(_jax.random.key(seed), 7919)
    shape, _ = _output_shape()
    out = dict(inp)
    out["loss_target"] = _jax.random.normal(_jax.random.fold_in(key, 0), shape, _jnp.float32)
    for i, name in enumerate(TWIN_WEIGHTS):
        w = inp[name].astype(_jnp.float32)
        if MOMENT_SCALE is None:
            s = _jnp.sqrt(_jnp.mean(_jnp.square(w)) + 1e-30)
        else:
            s = MOMENT_SCALE[name]
        km, kv = _jax.random.split(_jax.random.fold_in(key, i + 1))
        out[name] = w
        out["m_" + name] = s * _jax.random.normal(km, w.shape, _jnp.float32)
        out["v_" + name] = (s * s) * _jax.random.uniform(kv, w.shape, _jnp.float32, 0.5, 1.5)
    if N_MICROBATCH > 1:
        for name, axis in PER_EXAMPLE_BATCH_AXIS.items():
            out[name] = _to_microbatches(out[name], axis)
    return {'x': out['x'], 'p': out['p'], 'ffn1_pre_g': out['ffn1_pre_g'], 'ffn1_w_gate': out['ffn1_w_gate'], 'ffn1_w_up': out['ffn1_w_up'], 'ffn1_w_down': out['ffn1_w_down'], 'ffn1_post_g': out['ffn1_post_g'], 'mix_pre_g': out['mix_pre_g'], 'w_in': out['w_in'], 'pool_w': out['pool_w'], 'pool_scale': out['pool_scale'], 'w_pool_out': out['w_pool_out'], 'sgu_ln_g': out['sgu_ln_g'], 'sgu_ln_b': out['sgu_ln_b'], 'sgu_w_s': out['sgu_w_s'], 'sgu_b_s': out['sgu_b_s'], 'w_sgu_out': out['w_sgu_out'], 'conv_dw_k': out['conv_dw_k'], 'conv_dw_b': out['conv_dw_b'], 'conv_ln_g': out['conv_ln_g'], 'conv_ln_b': out['conv_ln_b'], 'w_conv_out': out['w_conv_out'], 'w_out': out['w_out'], 'mix_post_g': out['mix_post_g'], 'ffn2_pre_g': out['ffn2_pre_g'], 'ffn2_w_gate': out['ffn2_w_gate'], 'ffn2_w_up': out['ffn2_w_up'], 'ffn2_w_down': out['ffn2_w_down'], 'ffn2_post_g': out['ffn2_post_g'], 'ple_w_proj': out['ple_w_proj'], 'ple_pre_g': out['ple_pre_g'], 'ple_w_gate': out['ple_w_gate'], 'ple_post_g': out['ple_post_g'], 'loss_target': out['loss_target'], 'm_ffn1_pre_g': out['m_ffn1_pre_g'], 'm_ffn1_w_gate': out['m_ffn1_w_gate'], 'm_ffn1_w_up': out['m_ffn1_w_up'], 'm_ffn1_w_down': out['m_ffn1_w_down'], 'm_ffn1_post_g': out['m_ffn1_post_g'], 'm_mix_pre_g': out['m_mix_pre_g'], 'm_w_in': out['m_w_in'], 'm_pool_w': out['m_pool_w'], 'm_pool_scale': out['m_pool_scale'], 'm_w_pool_out': out['m_w_pool_out'], 'm_sgu_ln_g': out['m_sgu_ln_g'], 'm_sgu_ln_b': out['m_sgu_ln_b'], 'm_sgu_w_s': out['m_sgu_w_s'], 'm_sgu_b_s': out['m_sgu_b_s'], 'm_w_sgu_out': out['m_w_sgu_out'], 'm_conv_dw_k': out['m_conv_dw_k'], 'm_conv_dw_b': out['m_conv_dw_b'], 'm_conv_ln_g': out['m_conv_ln_g'], 'm_conv_ln_b': out['m_conv_ln_b'], 'm_w_conv_out': out['m_w_conv_out'], 'm_w_out': out['m_w_out'], 'm_mix_post_g': out['m_mix_post_g'], 'm_ffn2_pre_g': out['m_ffn2_pre_g'], 'm_ffn2_w_gate': out['m_ffn2_w_gate'], 'm_ffn2_w_up': out['m_ffn2_w_up'], 'm_ffn2_w_down': out['m_ffn2_w_down'], 'm_ffn2_post_g': out['m_ffn2_post_g'], 'm_ple_w_proj': out['m_ple_w_proj'], 'm_ple_pre_g': out['m_ple_pre_g'], 'm_ple_w_gate': out['m_ple_w_gate'], 'm_ple_post_g': out['m_ple_post_g'], 'v_ffn1_pre_g': out['v_ffn1_pre_g'], 'v_ffn1_w_gate': out['v_ffn1_w_gate'], 'v_ffn1_w_up': out['v_ffn1_w_up'], 'v_ffn1_w_down': out['v_ffn1_w_down'], 'v_ffn1_post_g': out['v_ffn1_post_g'], 'v_mix_pre_g': out['v_mix_pre_g'], 'v_w_in': out['v_w_in'], 'v_pool_w': out['v_pool_w'], 'v_pool_scale': out['v_pool_scale'], 'v_w_pool_out': out['v_w_pool_out'], 'v_sgu_ln_g': out['v_sgu_ln_g'], 'v_sgu_ln_b': out['v_sgu_ln_b'], 'v_sgu_w_s': out['v_sgu_w_s'], 'v_sgu_b_s': out['v_sgu_b_s'], 'v_w_sgu_out': out['v_w_sgu_out'], 'v_conv_dw_k': out['v_conv_dw_k'], 'v_conv_dw_b': out['v_conv_dw_b'], 'v_conv_ln_g': out['v_conv_ln_g'], 'v_conv_ln_b': out['v_conv_ln_b'], 'v_w_conv_out': out['v_w_conv_out'], 'v_w_out': out['v_w_out'], 'v_mix_post_g': out['v_mix_post_g'], 'v_ffn2_pre_g': out['v_ffn2_pre_g'], 'v_ffn2_w_gate': out['v_ffn2_w_gate'], 'v_ffn2_w_up': out['v_ffn2_w_up'], 'v_ffn2_w_down': out['v_ffn2_w_down'], 'v_ffn2_post_g': out['v_ffn2_post_g'], 'v_ple_w_proj': out['v_ple_w_proj'], 'v_ple_pre_g': out['v_ple_pre_g'], 'v_ple_w_gate': out['v_ple_w_gate'], 'v_ple_post_g': out['v_ple_post_g']}


def _loss(weights, diff, rest, loss_target):
    with _jax.named_scope("forward"):
        args = {**rest, TWIN_DIFF_INPUT: diff, **{k: w.astype(_WEIGHT_DTYPES[k]) for k, w in weights.items()}}
        y = _forward(args)
    with _jax.named_scope("loss_head"):
        err = _jnp.square(y.astype(_jnp.float32) - loss_target)
        return 0.5 * _jnp.sum(_jnp.mean(err, axis=-1)) if err.ndim else 0.5 * err


def _adamw(w, g, m, v):
    m = ADAM_B1 * m + (1.0 - ADAM_B1) * g
    v = ADAM_B2 * v + (1.0 - ADAM_B2) * _jnp.square(g)
    m_hat = m / (1.0 - ADAM_B1 ** ADAM_STEP)
    v_hat = v / (1.0 - ADAM_B2 ** ADAM_STEP)
    delta = -ADAM_LR * (m_hat / (_jnp.sqrt(v_hat) + ADAM_EPS) + ADAM_WD * w)
    return delta, m, v


def reference(x, p, ffn1_pre_g, ffn1_w_gate, ffn1_w_up, ffn1_w_down, ffn1_post_g, mix_pre_g, w_in, pool_w, pool_scale, w_pool_out, sgu_ln_g, sgu_ln_b, sgu_w_s, sgu_b_s, w_sgu_out, conv_dw_k, conv_dw_b, conv_ln_g, conv_ln_b, w_conv_out, w_out, mix_post_g, ffn2_pre_g, ffn2_w_gate, ffn2_w_up, ffn2_w_down, ffn2_post_g, ple_w_proj, ple_pre_g, ple_w_gate, ple_post_g, loss_target, m_ffn1_pre_g, m_ffn1_w_gate, m_ffn1_w_up, m_ffn1_w_down, m_ffn1_post_g, m_mix_pre_g, m_w_in, m_pool_w, m_pool_scale, m_w_pool_out, m_sgu_ln_g, m_sgu_ln_b, m_sgu_w_s, m_sgu_b_s, m_w_sgu_out, m_conv_dw_k, m_conv_dw_b, m_conv_ln_g, m_conv_ln_b, m_w_conv_out, m_w_out, m_mix_post_g, m_ffn2_pre_g, m_ffn2_w_gate, m_ffn2_w_up, m_ffn2_w_down, m_ffn2_post_g, m_ple_w_proj, m_ple_pre_g, m_ple_w_gate, m_ple_post_g, v_ffn1_pre_g, v_ffn1_w_gate, v_ffn1_w_up, v_ffn1_w_down, v_ffn1_post_g, v_mix_pre_g, v_w_in, v_pool_w, v_pool_scale, v_w_pool_out, v_sgu_ln_g, v_sgu_ln_b, v_sgu_w_s, v_sgu_b_s, v_w_sgu_out, v_conv_dw_k, v_conv_dw_b, v_conv_ln_g, v_conv_ln_b, v_w_conv_out, v_w_out, v_mix_post_g, v_ffn2_pre_g, v_ffn2_w_gate, v_ffn2_w_up, v_ffn2_w_down, v_ffn2_post_g, v_ple_w_proj, v_ple_pre_g, v_ple_w_gate, v_ple_post_g):
    given = dict(x=x, p=p, ffn1_pre_g=ffn1_pre_g, ffn1_w_gate=ffn1_w_gate, ffn1_w_up=ffn1_w_up, ffn1_w_down=ffn1_w_down, ffn1_post_g=ffn1_post_g, mix_pre_g=mix_pre_g, w_in=w_in, pool_w=pool_w, pool_scale=pool_scale, w_pool_out=w_pool_out, sgu_ln_g=sgu_ln_g, sgu_ln_b=sgu_ln_b, sgu_w_s=sgu_w_s, sgu_b_s=sgu_b_s, w_sgu_out=w_sgu_out, conv_dw_k=conv_dw_k, conv_dw_b=conv_dw_b, conv_ln_g=conv_ln_g, conv_ln_b=conv_ln_b, w_conv_out=w_conv_out, w_out=w_out, mix_post_g=mix_post_g, ffn2_pre_g=ffn2_pre_g, ffn2_w_gate=ffn2_w_gate, ffn2_w_up=ffn2_w_up, ffn2_w_down=ffn2_w_down, ffn2_post_g=ffn2_post_g, ple_w_proj=ple_w_proj, ple_pre_g=ple_pre_g, ple_w_gate=ple_w_gate, ple_post_g=ple_post_g, loss_target=loss_target, m_ffn1_pre_g=m_ffn1_pre_g, m_ffn1_w_gate=m_ffn1_w_gate, m_ffn1_w_up=m_ffn1_w_up, m_ffn1_w_down=m_ffn1_w_down, m_ffn1_post_g=m_ffn1_post_g, m_mix_pre_g=m_mix_pre_g, m_w_in=m_w_in, m_pool_w=m_pool_w, m_pool_scale=m_pool_scale, m_w_pool_out=m_w_pool_out, m_sgu_ln_g=m_sgu_ln_g, m_sgu_ln_b=m_sgu_ln_b, m_sgu_w_s=m_sgu_w_s, m_sgu_b_s=m_sgu_b_s, m_w_sgu_out=m_w_sgu_out, m_conv_dw_k=m_conv_dw_k, m_conv_dw_b=m_conv_dw_b, m_conv_ln_g=m_conv_ln_g, m_conv_ln_b=m_conv_ln_b, m_w_conv_out=m_w_conv_out, m_w_out=m_w_out, m_mix_post_g=m_mix_post_g, m_ffn2_pre_g=m_ffn2_pre_g, m_ffn2_w_gate=m_ffn2_w_gate, m_ffn2_w_up=m_ffn2_w_up, m_ffn2_w_down=m_ffn2_w_down, m_ffn2_post_g=m_ffn2_post_g, m_ple_w_proj=m_ple_w_proj, m_ple_pre_g=m_ple_pre_g, m_ple_w_gate=m_ple_w_gate, m_ple_post_g=m_ple_post_g, v_ffn1_pre_g=v_ffn1_pre_g, v_ffn1_w_gate=v_ffn1_w_gate, v_ffn1_w_up=v_ffn1_w_up, v_ffn1_w_down=v_ffn1_w_down, v_ffn1_post_g=v_ffn1_post_g, v_mix_pre_g=v_mix_pre_g, v_w_in=v_w_in, v_pool_w=v_pool_w, v_pool_scale=v_pool_scale, v_w_pool_out=v_w_pool_out, v_sgu_ln_g=v_sgu_ln_g, v_sgu_ln_b=v_sgu_ln_b, v_sgu_w_s=v_sgu_w_s, v_sgu_b_s=v_sgu_b_s, v_w_sgu_out=v_w_sgu_out, v_conv_dw_k=v_conv_dw_k, v_conv_dw_b=v_conv_dw_b, v_conv_ln_g=v_conv_ln_g, v_conv_ln_b=v_conv_ln_b, v_w_conv_out=v_w_conv_out, v_w_out=v_w_out, v_mix_post_g=v_mix_post_g, v_ffn2_pre_g=v_ffn2_pre_g, v_ffn2_w_gate=v_ffn2_w_gate, v_ffn2_w_up=v_ffn2_w_up, v_ffn2_w_down=v_ffn2_w_down, v_ffn2_post_g=v_ffn2_post_g, v_ple_w_proj=v_ple_w_proj, v_ple_pre_g=v_ple_pre_g, v_ple_w_gate=v_ple_w_gate, v_ple_post_g=v_ple_post_g)
    weights = {n: given[n] for n in TWIN_WEIGHTS}
    shared = {n: given[n] for n in SHARED_INPUTS}
    per_example = {n: given[n] for n in ['x', 'p']}
    grad_fn = _jax.value_and_grad(_loss, argnums=(0, 1))

    def one_microbatch(ex, loss_target):
        ex = dict(ex)
        diff = ex.pop(TWIN_DIFF_INPUT)
        return grad_fn(weights, diff, {**shared, **ex}, loss_target)

    if N_MICROBATCH == 1:
        loss, (grad_w, grad_x) = one_microbatch(per_example, given["loss_target"])
    else:
        def body(carry, xs):
            loss_sum, grad_sum = carry
            l_k, (gw_k, gx_k) = one_microbatch(xs[0], xs[1])
            with _jax.named_scope("update"):
                return (loss_sum + l_k, _jax.tree.map(_jnp.add, grad_sum, gw_k)), gx_k

        init = (_jnp.zeros((), _jnp.float32), _jax.tree.map(_jnp.zeros_like, weights))
        (loss, grad_w), grad_x = _jax.lax.scan(body, init, (per_example, given["loss_target"]))
    with _jax.named_scope("update"):
        delta_w, new_m, new_v = {}, {}, {}
        for n in TWIN_WEIGHTS:
            delta_w[n], new_m[n], new_v[n] = _adamw(weights[n], grad_w[n], given["m_" + n], given["v_" + n])
    return (loss, grad_x, *[grad_w[n] for n in TWIN_WEIGHTS], *[delta_w[n] for n in TWIN_WEIGHTS],
            *[new_m[n] for n in TWIN_WEIGHTS], *[new_v[n] for n in TWIN_WEIGHTS])
```

```python
import functools

import jax
import jax.numpy as jnp
from jax import lax
from jax.experimental import pallas as pl
from jax.experimental.pallas import tpu as pltpu

D = 1024
F = 2816
C = 5632
PW = 512
PLE = 256
NDEV = 8
CHUNK = 128
POOL_WINDOWS = (2, 4, 8, 16)
CONV_K = 31
POOL_HALO = 16
CONV_HALO = 32
EPS = 1e-6
OFF_U, OFF_V, OFF_A, OFF_B, OFF_G = 512, 1024, 1536, 2048, 2560

ADAM_LR, ADAM_B1, ADAM_B2, ADAM_EPS, ADAM_WD, ADAM_STEP = 0.001, 0.9, 0.999, 1e-08, 0.01, 10

BF = jnp.bfloat16
F32 = jnp.float32
VMEM_LIMIT = 56 * 1024 * 1024
MESH = pl.DeviceIdType.MESH
INV_SQRT2 = 0.7071067811865476
INV_SQRT_2PI = 0.3989422804014327

ROWS = 3328
LAYOUT = {
    "w_in": (0, 704, 0, 1024, True),
    "ffn1_w_gate": (704, 352, 0, 1024, True),
    "ffn1_w_up": (1056, 352, 0, 1024, True),
    "ffn1_w_down": (1408, 352, 0, 1024, False),
    "ffn2_w_gate": (1760, 352, 0, 1024, True),
    "ffn2_w_up": (2112, 352, 0, 1024, True),
    "ffn2_w_down": (2464, 352, 0, 1024, False),
    "w_pool_out": (2816, 128, 0, 512, True),
    "w_sgu_out": (2816, 128, 512, 512, True),
    "w_conv_out": (2944, 128, 0, 512, True),
    "ple_w_proj": (2944, 128, 512, 256, True),
    "w_out": (3072, 128, 0, 1024, False),
    "ple_w_gate": (3200, 128, 0, 1024, False),
}
BIG = tuple(LAYOUT)
GATHERED = {
    "win": (C, D), "gu1": (2 * F, D), "d1": (F, D), "gu2": (2 * F, D), "d2": (F, D),
    "po": (D, PW), "so": (D, PW), "co": (D, PW), "pr": (D, PW), "wo": (D, D), "pg": (D, D),
}
PIECES = (
    ("win", 0, 0, 704, 0, 1024), ("gu1", 0, 704, 352, 0, 1024), ("gu1", F, 1056, 352, 0, 1024),
    ("d1", 0, 1408, 352, 0, 1024), ("gu2", 0, 1760, 352, 0, 1024), ("gu2", F, 2112, 352, 0, 1024),
    ("d2", 0, 2464, 352, 0, 1024), ("po", 0, 2816, 128, 0, 512), ("so", 0, 2816, 128, 512, 512),
    ("co", 0, 2944, 128, 0, 512), ("pr", 0, 2944, 128, 512, 512), ("wo", 0, 3072, 128, 0, 1024),
    ("pg", 0, 3200, 128, 0, 1024),
)
GAINS = ("ffn1_pre_g", "ffn1_post_g", "mix_pre_g", "mix_post_g", "ffn2_pre_g", "ffn2_post_g", "ple_pre_g", "ple_post_g")
HALVES = ("pool_scale", "sgu_ln_g", "sgu_ln_b", "sgu_b_s", "conv_dw_b", "conv_ln_g", "conv_ln_b")
SMALL_ROWS = 160


def _cparams(sem=None, **kw):
    if sem is not None:
        kw["dimension_semantics"] = sem
    return pltpu.CompilerParams(vmem_limit_bytes=VMEM_LIMIT, **kw)


def _whole(l, shape):
    nd = len(shape)
    return pl.BlockSpec((None,) + tuple(shape), lambda *_: (l,) + (0,) * nd, pipeline_mode=pl.Buffered(1))


def _row(l, n):
    return pl.BlockSpec((None, 1, n), lambda *_: (l, 0, 0))


def _dot(a, b):
    return jnp.dot(a, b, preferred_element_type=F32)


def _dot_nt(a, b):
    return lax.dot_general(a, b, (((1,), (1,)), ((), ())), preferred_element_type=F32)


def _dot_tn(a, b):
    return lax.dot_general(a, b, (((0,), (0,)), ((), ())), preferred_element_type=F32)


def _mean(x):
    return jnp.mean(x, axis=-1, keepdims=True)


def _colsum(x):
    return jnp.sum(x, axis=0, keepdims=True)


def _rms(x):
    r = lax.rsqrt(_mean(x * x) + EPS)
    return x * r, r


def _rms_bwd(xh, r, g, dy):
    dxh = dy * g
    return r * (dxh - xh * _mean(dxh * xh)), _colsum(dy * xh)


def _ln(x):
    xc = x - _mean(x)
    r = lax.rsqrt(_mean(xc * xc) + EPS)
    return xc * r, r


def _ln_bwd(xh, r, g, dy):
    dxh = dy * g
    return r * (dxh - _mean(dxh) - xh * _mean(dxh * xh)), _colsum(dy * xh), _colsum(dy)


def _sigmoid(x):
    return jax.nn.sigmoid(x)


def _gelu(x):
    return 0.5 * x * (1.0 + lax.erf(x * INV_SQRT2))


def _gelu_grad(x):
    return 0.5 * (1.0 + lax.erf(x * INV_SQRT2)) + x * jnp.exp(-0.5 * x * x) * INV_SQRT_2PI


def _prep(w):
    L = w["w_in"].shape[0]

    def body(*refs):
        ins, out = dict(zip(BIG, refs[:-1])), refs[-1]
        for name, (off, rows, col0, width, tr) in LAYOUT.items():
            v = ins[name][...]
            if tr:
                v = v.T
            out[pl.ds(off, rows), pl.ds(col0, width)] = v.astype(BF)
        out[pl.ds(2944, 128), pl.ds(768, 256)] = jnp.zeros((128, 256), BF)

    in_specs = [pl.BlockSpec((None,) + w[n].shape[1:], lambda l: (l, 0, 0)) for n in BIG]
    return pl.pallas_call(
        body, name="prep", grid=(L,), in_specs=in_specs,
        out_specs=pl.BlockSpec((None, ROWS, D), lambda l: (l, 0, 0)),
        out_shape=jax.ShapeDtypeStruct((L, ROWS, D), BF),
        compiler_params=_cparams(("parallel",)),
    )(*[w[n] for n in BIG])


def _place():
    x, y, c = lax.axis_index("x"), lax.axis_index("y"), lax.axis_index("c")
    chips = [(1 - x, y), (x, 1 - y), (1 - x, 1 - y)]
    return x, y, c, chips


def _allgather(name, src, dsts, pieces):
    npc = len(pieces)

    def body(src_ref, *rest):
        outs, (send_sems, recv_sems, local_sems) = rest[:len(dsts)], rest[len(dsts):]
        x, y, c, chips = _place()
        me, sibling = (x, y, c), (x, y, 1 - c)

        def shard_of(dev):
            return 4 * dev[0] + 2 * dev[1] + dev[2]

        def copies(k, block, to, from_src):
            res = []
            for di, dst_fn, src_sl in pieces:
                dst = outs[di].at[dst_fn(shard_of(block))]
                s = src_ref.at[src_sl] if from_src else dst
                res.append(pltpu.make_async_remote_copy(src_ref=s, dst_ref=dst, send_sem=send_sems.at[k],
                                                        recv_sem=recv_sems.at[k], device_id=to, device_id_type=MESH))
            return res

        def whole(k):
            return pltpu.make_async_remote_copy(src_ref=src_ref, dst_ref=src_ref, send_sem=send_sems.at[k],
                                                recv_sem=recv_sems.at[k], device_id=me, device_id_type=MESH)

        mine = [pltpu.make_async_copy(src_ref.at[src_sl], outs[di].at[dst_fn(shard_of(me))], local_sems.at[i])
                for i, (di, dst_fn, src_sl) in enumerate(pieces)]
        for cp in mine:
            cp.start()
        for cp in copies(0, me, sibling, True):
            cp.start()
        for j, chip in enumerate(chips):
            for cp in copies(1 + j, me, (*chip, c), True):
                cp.start()
        for j, chip in enumerate(chips):
            whole(1 + j).wait_recv()
            for cp in copies(4 + j, (*chip, c), sibling, False):
                cp.start()
        for k in (0, 4, 5, 6):
            whole(k).wait_recv()
        for k in range(7):
            whole(k).wait_send()
        for cp in mine:
            cp.wait()

    any_spec = pl.BlockSpec(memory_space=pl.ANY)
    return pl.pallas_call(
        body, name=name, in_specs=[any_spec], out_specs=[any_spec] * len(dsts), out_shape=dsts,
        scratch_shapes=[pltpu.SemaphoreType.DMA((7,)), pltpu.SemaphoreType.DMA((7,)), pltpu.SemaphoreType.DMA((npc,))],
    )(src)


def _gather_weights(packed):
    L = packed.shape[0]
    names = list(GATHERED)
    dsts = [jax.ShapeDtypeStruct((L,) + GATHERED[n], BF) for n in names]
    pieces = []
    for gname, row0, poff, prow, pcol, width in PIECES:
        def dst_fn(shard, row0=row0, prow=prow):
            return (slice(None), pl.ds(row0 + shard * prow, prow), slice(None))
        pieces.append((names.index(gname), dst_fn, (slice(None), pl.ds(poff, prow), pl.ds(pcol, width))))
    outs = _allgather("allgather_weights", packed, dsts, pieces)
    return dict(zip(names, outs))


def _gather_rows(name, src):
    dst = jax.ShapeDtypeStruct((NDEV,) + src.shape, src.dtype)
    full = (slice(None), slice(None), slice(None))
    pieces = [(0, lambda shard: (pl.ds(shard, 1), slice(None), slice(None)), full)]
    return _allgather(name, src.reshape((1,) + src.shape), [dst], pieces)[0]


def _exchange_sibling(slab):
    _, L, rows, cols = slab.shape

    def body(slab_ref, out_ref, send_sems, recv_sems):
        x, y, c, _ = _place()
        sibling = (x, y, 1 - c)
        cps = []
        for q in range(4):
            shard = 2 * q + (1 - c)
            cps.append(pltpu.make_async_remote_copy(src_ref=slab_ref.at[shard], dst_ref=out_ref.at[q],
                                                    send_sem=send_sems.at[q], recv_sem=recv_sems.at[q],
                                                    device_id=sibling, device_id_type=MESH))
        for cp in cps:
            cp.start()
        for cp in cps:
            cp.wait_recv()
            cp.wait_send()

    any_spec = pl.BlockSpec(memory_space=pl.ANY)
    return pl.pallas_call(
        body, name="exchange_sibling", in_specs=[any_spec], out_specs=any_spec,
        out_shape=jax.ShapeDtypeStruct((4, L, rows, cols), slab.dtype),
        scratch_shapes=[pltpu.SemaphoreType.DMA((4,)), pltpu.SemaphoreType.DMA((4,))],
    )(slab)


def _exchange_chips(q):
    _, L, rows, cols = q.shape

    def body(q_ref, out_ref, send_sems, recv_sems):
        x, y, c, chips = _place()
        cps = []
        for j, chip in enumerate(chips):
            cps.append(pltpu.make_async_remote_copy(src_ref=q_ref.at[1 + j], dst_ref=out_ref.at[j],
                                                    send_sem=send_sems.at[j], recv_sem=recv_sems.at[j],
                                                    device_id=(*chip, c), device_id_type=MESH))
        for cp in cps:
            cp.start()
        for cp in cps:
            cp.wait_recv()
            cp.wait_send()

    any_spec = pl.BlockSpec(memory_space=pl.ANY)
    return pl.pallas_call(
        body, name="exchange_chips", in_specs=[any_spec], out_specs=any_spec,
        out_shape=jax.ShapeDtypeStruct((3, L, rows, cols), q.dtype),
        scratch_shapes=[pltpu.SemaphoreType.DMA((3,)), pltpu.SemaphoreType.DMA((3,))],
    )(q)


def _chip_sums(slab, recv):
    _, L, rows, cols = slab.shape
    x, y, c, chips = _place()
    order = [(x, y)] + chips
    shard_idx = jnp.stack([4 * cx + 2 * cy + c for cx, cy in order]).astype(jnp.int32)
    chip_idx = jnp.stack([2 * cx + cy for cx, cy in order]).astype(jnp.int32)
    tr = 416

    def body(sh_ref, ch_ref, a_ref, b_ref, o_ref):
        o_ref[...] = (a_ref[...] + b_ref[...]).astype(BF)

    grid_spec = pltpu.PrefetchScalarGridSpec(
        num_scalar_prefetch=2, grid=(4, L, rows // tr),
        in_specs=[pl.BlockSpec((None, None, tr, cols), lambda q, l, r, sh, ch: (sh[q], l, r, 0)),
                  pl.BlockSpec((None, None, tr, cols), lambda q, l, r, sh, ch: (ch[q], l, r, 0))],
        out_specs=pl.BlockSpec((None, None, tr, cols), lambda q, l, r, sh, ch: (q, l, r, 0)),
    )
    return pl.pallas_call(
        body, name="chip_sums", grid_spec=grid_spec, out_shape=jax.ShapeDtypeStruct((4, L, rows, cols), BF),
        compiler_params=_cparams(("parallel", "parallel", "parallel")),
    )(shard_idx, chip_idx, slab, recv)


FC = 1408


def _ffn_fwd(h, l, pre_g, post_g, wgu, wd):
    T = h.shape[0]
    R = min(512, T)

    def body(h_ref, pg_ref, qg_ref, wgu_ref, wd_ref, out_ref, n_ref, ab_ref, f_ref):
        hh = h_ref[...]
        xh, _ = _rms(hh)
        n = (xh * pg_ref[...]).astype(BF)
        n_ref[...] = n
        f = jnp.zeros((R, D), F32)
        for ci in range(F // FC):
            a = _dot_nt(n, wgu_ref[pl.ds(ci * FC, FC), :])
            b = _dot_nt(n, wgu_ref[pl.ds(F + ci * FC, FC), :])
            ab_ref[:, pl.ds(ci * FC, FC)] = a.astype(BF)
            ab_ref[:, pl.ds(F + ci * FC, FC)] = b.astype(BF)
            s = (a * _sigmoid(a) * b).astype(BF)
            f = f + _dot(s, wd_ref[pl.ds(ci * FC, FC), :])
        f_ref[...] = f
        fh, _ = _rms(f)
        out_ref[...] = hh + 0.5 * (fh * qg_ref[...])

    tile = lambda n: pl.BlockSpec((R, n), lambda i: (i, 0))
    return pl.pallas_call(
        body, name="ffn_fwd", grid=(T // R,),
        in_specs=[tile(D), _row(l, D), _row(l, D), _whole(l, (2 * F, D)), _whole(l, (F, D))],
        out_specs=[tile(D), tile(D), tile(2 * F), tile(D)],
        out_shape=[jax.ShapeDtypeStruct((T, D), F32), jax.ShapeDtypeStruct((T, D), BF),
                   jax.ShapeDtypeStruct((T, 2 * F), BF), jax.ShapeDtypeStruct((T, D), F32)],
        compiler_params=_cparams(("parallel",)),
    )(h, pre_g, post_g, wgu, wd)


def _ffn_bwd(dout, h, ab, f, l, pre_g, post_g, wgu, wd):
    T = h.shape[0]
    R = min(256, T)

    def body(do_ref, h_ref, ab_ref, f_ref, pg_ref, qg_ref, wgu_ref, wd_ref,
             dh_ref, dab_ref, s_ref, df_ref, dpg_ref, dqg_ref):
        i = pl.program_id(0)

        @pl.when(i == 0)
        def _():
            dpg_ref[...] = jnp.zeros_like(dpg_ref)
            dqg_ref[...] = jnp.zeros_like(dqg_ref)

        do = do_ref[...]
        fh, fr = _rms(f_ref[...])
        df, dq = _rms_bwd(fh, fr, qg_ref[...], 0.5 * do)
        dqg_ref[...] += dq
        df = df.astype(BF)
        df_ref[...] = df
        dn = jnp.zeros((R, D), F32)
        for ci in range(F // FC):
            ga, gb = pl.ds(ci * FC, FC), pl.ds(F + ci * FC, FC)
            ds = _dot_nt(df, wd_ref[ga, :])
            a = ab_ref[:, ga].astype(F32)
            b = ab_ref[:, gb].astype(F32)
            sg = _sigmoid(a)
            sil = a * sg
            s_ref[:, ga] = (sil * b).astype(BF)
            da = (ds * b * (sg * (1.0 + a * (1.0 - sg)))).astype(BF)
            db = (ds * sil).astype(BF)
            dab_ref[:, ga] = da
            dab_ref[:, gb] = db
            dn = dn + _dot(da, wgu_ref[ga, :]) + _dot(db, wgu_ref[gb, :])
        xh, xr = _rms(h_ref[...])
        dx, dp = _rms_bwd(xh, xr, pg_ref[...], dn)
        dpg_ref[...] += dp
        dh_ref[...] = do + dx

    tile = lambda n: pl.BlockSpec((R, n), lambda i: (i, 0))
    acc = pl.BlockSpec((1, D), lambda i: (0, 0))
    return pl.pallas_call(
        body, name="ffn_bwd", grid=(T // R,),
        in_specs=[tile(D), tile(D), tile(2 * F), tile(D), _row(l, D), _row(l, D), _whole(l, (2 * F, D)), _whole(l, (F, D))],
        out_specs=[tile(D), tile(2 * F), tile(F), tile(D), acc, acc],
        out_shape=[jax.ShapeDtypeStruct((T, D), F32), jax.ShapeDtypeStruct((T, 2 * F), BF),
                   jax.ShapeDtypeStruct((T, F), BF), jax.ShapeDtypeStruct((T, D), BF),
                   jax.ShapeDtypeStruct((1, D), F32), jax.ShapeDtypeStruct((1, D), F32)],
        compiler_params=_cparams(("arbitrary",)),
    )(dout, h, ab, f, pre_g, post_g, wgu, wd)


def _mix_in_fwd(h, l, pre_g, win):
    T = h.shape[0]
    R = min(256, T)

    def body(h_ref, pg_ref, w_ref, n_ref, z_ref):
        xh, _ = _rms(h_ref[...])
        n = (xh * pg_ref[...]).astype(BF)
        n_ref[...] = n
        for ci in range(C // FC):
            z_ref[:, pl.ds(ci * FC, FC)] = _dot_nt(n, w_ref[pl.ds(ci * FC, FC), :])

    tile = lambda n: pl.BlockSpec((R, n), lambda i: (i, 0))
    return pl.pallas_call(
        body, name="mix_in_fwd", grid=(T // R,),
        in_specs=[tile(D), _row(l, D), _whole(l, (C, D))],
        out_specs=[tile(D), tile(C)],
        out_shape=[jax.ShapeDtypeStruct((T, D), BF), jax.ShapeDtypeStruct((T, C), F32)],
        compiler_params=_cparams(("parallel",)),
    )(h, pre_g, win)


def _mix_in_bwd(dout, dz, h, l, pre_g, win):
    T = h.shape[0]
    R = min(512, T)

    def body(do_ref, dz_ref, h_ref, pg_ref, w_ref, dh_ref, dpg_ref):
        @pl.when(pl.program_id(0) == 0)
        def _():
            dpg_ref[...] = jnp.zeros_like(dpg_ref)

        dn = _dot(dz_ref[...], w_ref[...])
        xh, xr = _rms(h_ref[...])
        dx, dp = _rms_bwd(xh, xr, pg_ref[...], dn)
        dpg_ref[...] += dp
        dh_ref[...] = do_ref[...] + dx

    tile = lambda n: pl.BlockSpec((R, n), lambda i: (i, 0))
    return pl.pallas_call(
        body, name="mix_in_bwd", grid=(T // R,),
        in_specs=[tile(D), tile(C), tile(D), _row(l, D), _whole(l, (C, D))],
        out_specs=[tile(D), pl.BlockSpec((1, D), lambda i: (0, 0))],
        out_shape=[jax.ShapeDtypeStruct((T, D), F32), jax.ShapeDtypeStruct((1, D), F32)],
        compiler_params=_cparams(("arbitrary",)),
    )(dout, dz, h, pre_g, win)


def _mix_specs(l, R, tile_of):
    def halo(rows, col_block):
        per = R // rows
        return pl.BlockSpec((rows, PW), lambda i: (jnp.maximum(tile_of(i) * per - 1, 0), col_block))
    return [
        pl.BlockSpec((R, C), lambda i: (tile_of(i), 0)),
        halo(POOL_HALO, 0), halo(CONV_HALO, 3), halo(CONV_HALO, 4),
        pl.BlockSpec((R, D), lambda i: (tile_of(i), 0)),
        _whole(l, (4, CHUNK, CHUNK)), _row(l, PW),
        _row(l, PW), _row(l, PW), _whole(l, (4, CHUNK, CHUNK)), _whole(l, (CHUNK, 4)),
        _whole(l, (CONV_HALO, PW)), _row(l, PW), _row(l, PW), _row(l, PW),
        _row(l, D),
        _whole(l, (D, PW)), _whole(l, (D, PW)), _whole(l, (D, PW)), _whole(l, (D, D)),
    ]


class _MixFwd:
    def __init__(self, R, tile, refs, scratch):
        (z_ref, zph_ref, zah_ref, zbh_ref, _h, pw_ref, ps_ref, lg_ref, lb_ref, ws_ref, bst_ref,
         ck_ref, cb_ref, cg_ref, cbb_ref, _qg, wpo_ref, wso_ref, wco_ref, wout_ref) = refs
        pbuf, xbuf, sbuf, pm_ref, sg_ref, cv_ref = scratch
        first = tile == 0
        tglob = tile * R + lax.broadcasted_iota(jnp.int32, (R, 1), 0)
        pbuf[pl.ds(0, POOL_HALO), :] = jnp.where(first, 0.0, zph_ref[...])
        pbuf[pl.ds(POOL_HALO, R), :] = z_ref[:, pl.ds(0, PW)]
        self.pooled, self.yg, self.cnt = [], [], []
        for gi, w in enumerate(POOL_WINDOWS):
            cols = pl.ds(gi * CHUNK, CHUNK)
            x = pbuf[pl.ds(POOL_HALO, R), cols]
            acc = x
            for j in range(1, w):
                acc = acc + pbuf[pl.ds(POOL_HALO - j, R), cols]
            cnt = jnp.minimum(tglob + 1, w).astype(F32)
            pooled = (acc / cnt - x).astype(BF)
            yg = _dot(pooled, pw_ref[gi].astype(BF))
            pm_ref[:, cols] = (yg * ps_ref[:, cols]).astype(BF)
            self.pooled.append(pooled)
            self.yg.append(yg)
            self.cnt.append(cnt)
        zu, zv = z_ref[:, pl.ds(OFF_U, PW)], z_ref[:, pl.ds(OFF_V, PW)]
        self.u = _gelu(zu)
        self.vh, self.vr = _ln(_gelu(zv))
        self.vln = (self.vh * lg_ref[...] + lb_ref[...]).astype(BF)
        tt = lax.broadcasted_iota(jnp.int32, (CHUNK, CHUNK), 0)
        ss = lax.broadcasted_iota(jnp.int32, (CHUNK, CHUNK), 1)
        self.causal = tt >= ss
        self.wc = [jnp.where(self.causal, ws_ref[hd], 0.0).astype(BF) for hd in range(4)]
        for ck in range(R // CHUNK):
            for hd in range(4):
                rows, cols = pl.ds(ck * CHUNK, CHUNK), pl.ds(hd * CHUNK, CHUNK)
                blk = self.vln[ck * CHUNK:(ck + 1) * CHUNK, hd * CHUNK:(hd + 1) * CHUNK]
                sbuf[rows, cols] = _dot(self.wc[hd], blk) + bst_ref[:, pl.ds(hd, 1)]
        self.s = sbuf[...]
        sg_ref[...] = (self.u * self.s).astype(BF)
        self.za = z_ref[:, pl.ds(OFF_A, PW)]
        self.sgb = _sigmoid(z_ref[:, pl.ds(OFF_B, PW)])
        xbuf[pl.ds(0, CONV_HALO), :] = jnp.where(first, 0.0, zah_ref[...] * _sigmoid(zbh_ref[...]))
        xbuf[pl.ds(CONV_HALO, R), :] = self.za * self.sgb
        y = jnp.zeros((R, PW), F32) + cb_ref[...]
        for k in range(CONV_K):
            y = y + xbuf[pl.ds(CONV_HALO - (CONV_K - 1) + k, R), :] * ck_ref[pl.ds(k, 1), :]
        self.yh, self.yr = _ln(y)
        self.yl = self.yh * cg_ref[...] + cbb_ref[...]
        self.sy = _sigmoid(self.yl)
        cv_ref[...] = (self.yl * self.sy).astype(BF)
        self.g = [_sigmoid(z_ref[:, pl.ds(OFF_G + j * D, D)]) for j in range(3)]
        self.y = [_dot_nt(pm_ref[...], wpo_ref[...]), _dot_nt(sg_ref[...], wso_ref[...]), _dot_nt(cv_ref[...], wco_ref[...])]
        self.merged = (self.g[0] * self.y[0] + self.g[1] * self.y[1] + self.g[2] * self.y[2]).astype(BF)
        self.o = _dot(self.merged, wout_ref[...])


def _mix_scratch(R):
    return [pltpu.VMEM((R + POOL_HALO, PW), F32), pltpu.VMEM((R + CONV_HALO, PW), F32), pltpu.VMEM((R, PW), F32)]


def _mix_core_fwd(z, h, l, sm, gw):
    T = h.shape[0]
    R = min(256, T)

    def body(*refs):
        ins, out_ref, scratch = refs[:20], refs[20], refs[21:]
        fw = _MixFwd(R, pl.program_id(0), ins, scratch)
        oh, _ = _rms(fw.o)
        out_ref[...] = ins[4][...] + oh * ins[15][...]

    act = pltpu.VMEM((R, PW), BF)
    return pl.pallas_call(
        body, name="mix_core_fwd", grid=(T // R,),
        in_specs=_mix_specs(l, R, lambda i: i),
        out_specs=pl.BlockSpec((R, D), lambda i: (i, 0)),
        out_shape=jax.ShapeDtypeStruct((T, D), F32),
        scratch_shapes=_mix_scratch(R) + [act, act, act],
        compiler_params=_cparams(("arbitrary",)),
    )(z, z, z, z, h, sm["pool_w"], sm["pool_scale"], sm["sgu_ln_g"], sm["sgu_ln_b"], sm["sgu_w_s"], sm["sgu_b_sT"],
      sm["conv_k"], sm["conv_dw_b"], sm["conv_ln_g"], sm["conv_ln_b"], sm["mix_post_g"],
      gw["po"], gw["so"], gw["co"], gw["wo"])


MIX_SMALL_GRADS = (("pool_w", (4, CHUNK, CHUNK)), ("pool_scale", (1, PW)), ("sgu_ln_g", (1, PW)), ("sgu_ln_b", (1, PW)),
                   ("sgu_w_s", (4, CHUNK, CHUNK)), ("sgu_b_sT", (CHUNK, 4)), ("conv_k", (CONV_HALO, PW)),
                   ("conv_dw_b", (1, PW)), ("conv_ln_g", (1, PW)), ("conv_ln_b", (1, PW)), ("mix_post_g", (1, D)))


def _mix_core_bwd(dout, z, h, l, sm, gw):
    T = h.shape[0]
    R = min(128, T)
    nt = T // R
    tile_of = lambda i: nt - 1 - i

    def body(*refs):
        do_ref, ins = refs[0], refs[1:21]
        (dz_ref, mg_ref, dob_ref, dy0_ref, dy1_ref, dy2_ref, pm_ref, sg_ref, cv_ref,
         dpw_ref, dps_ref, dlg_ref, dlb_ref, dws_ref, dbs_ref, dck_ref, dcb_ref, dcg_ref, dcbb_ref, dqg_ref) = refs[21:41]
        pbuf, xbuf, sbuf, qbuf, dybuf, dvbuf = refs[41:]
        (_z, _zp, _za, _zb, h_ref, pw_ref, ps_ref, lg_ref, lb_ref, ws_ref, bst_ref,
         ck_ref, cb_ref, cg_ref, cbb_ref, qg_ref, wpo_ref, wso_ref, wco_ref, wout_ref) = ins
        i = pl.program_id(0)
        small = (dpw_ref, dps_ref, dlg_ref, dlb_ref, dws_ref, dbs_ref, dck_ref, dcb_ref, dcg_ref, dcbb_ref, dqg_ref)

        @pl.when(i == 0)
        def _():
            for r in small:
                r[...] = jnp.zeros_like(r)
            qbuf[pl.ds(R, POOL_HALO), :] = jnp.zeros((POOL_HALO, PW), F32)
            dybuf[pl.ds(R, CONV_HALO), :] = jnp.zeros((CONV_HALO, PW), F32)

        fw = _MixFwd(R, tile_of(i), ins, (pbuf, xbuf, sbuf, pm_ref, sg_ref, cv_ref))
        mg_ref[...] = fw.merged
        oh, orr = _rms(fw.o)
        do, dq = _rms_bwd(oh, orr, qg_ref[...], do_ref[...])
        dqg_ref[...] += dq
        do = do.astype(BF)
        dob_ref[...] = do
        dm = _dot_nt(do, wout_ref[...])
        dys = []
        for j, dyj_ref in enumerate((dy0_ref, dy1_ref, dy2_ref)):
            g = fw.g[j]
            dz_ref[:, pl.ds(OFF_G + j * D, D)] = (dm * fw.y[j] * g * (1.0 - g)).astype(BF)
            dyj = (dm * g).astype(BF)
            dyj_ref[...] = dyj
            dys.append(dyj)
        dpm = _dot(dys[0], wpo_ref[...])
        dsg = _dot(dys[1], wso_ref[...])
        dcv = _dot(dys[2], wco_ref[...])
        for gi, w in enumerate(POOL_WINDOWS):
            cols = pl.ds(gi * CHUNK, CHUNK)
            dpm_g = dpm[:, gi * CHUNK:(gi + 1) * CHUNK]
            dps_ref[:, cols] += _colsum(dpm_g * fw.yg[gi])
            dyg = (dpm_g * ps_ref[:, cols]).astype(BF)
            dpw_ref[gi] += _dot_tn(fw.pooled[gi], dyg)
            dpooled = _dot_nt(dyg, pw_ref[gi].astype(BF))
            qbuf[pl.ds(0, R), cols] = dpooled / fw.cnt[gi]
            acc = -dpooled
            for j in range(w):
                acc = acc + qbuf[pl.ds(j, R), cols]
            dz_ref[:, cols] = acc.astype(BF)
        qbuf[pl.ds(R, POOL_HALO), :] = qbuf[pl.ds(0, POOL_HALO), :]
        ds = dsg * fw.u
        du = dsg * fw.s
        for ck in range(R // CHUNK):
            for hd in range(4):
                rows, cols = pl.ds(ck * CHUNK, CHUNK), pl.ds(hd * CHUNK, CHUNK)
                ds_f = ds[ck * CHUNK:(ck + 1) * CHUNK, hd * CHUNK:(hd + 1) * CHUNK]
                ds_blk = ds_f.astype(BF)
                v_blk = fw.vln[ck * CHUNK:(ck + 1) * CHUNK, hd * CHUNK:(hd + 1) * CHUNK]
                dbs_ref[:, pl.ds(hd, 1)] += jnp.sum(ds_f, axis=1, keepdims=True)
                dws_ref[hd] += jnp.where(fw.causal, _dot_nt(ds_blk, v_blk), 0.0)
                dvbuf[rows, cols] = _dot_tn(fw.wc[hd], ds_blk)
        dgv, dg, db = _ln_bwd(fw.vh, fw.vr, lg_ref[...], dvbuf[...])
        dlg_ref[...] += dg
        dlb_ref[...] += db
        dz_ref[:, pl.ds(OFF_V, PW)] = (dgv * _gelu_grad(_z[:, pl.ds(OFF_V, PW)])).astype(BF)
        dz_ref[:, pl.ds(OFF_U, PW)] = (du * _gelu_grad(_z[:, pl.ds(OFF_U, PW)])).astype(BF)
        dyl = dcv * (fw.sy * (1.0 + fw.yl * (1.0 - fw.sy)))
        dy, dg, db = _ln_bwd(fw.yh, fw.yr, cg_ref[...], dyl)
        dcg_ref[...] += dg
        dcbb_ref[...] += db
        dcb_ref[...] += _colsum(dy)
        dybuf[pl.ds(0, R), :] = dy
        dxg = jnp.zeros((R, PW), F32)
        for k in range(CONV_K):
            dck_ref[pl.ds(k, 1), :] += _colsum(dy * xbuf[pl.ds(CONV_HALO - (CONV_K - 1) + k, R), :])
            dxg = dxg + dybuf[pl.ds(CONV_K - 1 - k, R), :] * ck_ref[pl.ds(k, 1), :]
        dybuf[pl.ds(R, CONV_HALO), :] = dybuf[pl.ds(0, CONV_HALO), :]
        dz_ref[:, pl.ds(OFF_A, PW)] = (dxg * fw.sgb).astype(BF)
        dz_ref[:, pl.ds(OFF_B, PW)] = (dxg * fw.za * fw.sgb * (1.0 - fw.sgb)).astype(BF)

    tile = lambda n: pl.BlockSpec((R, n), lambda i: (tile_of(i), 0))
    small_specs = [pl.BlockSpec(shape, lambda i, nd=len(shape): (0,) * nd) for _, shape in MIX_SMALL_GRADS]
    outs = pl.pallas_call(
        body, name="mix_core_bwd", grid=(nt,),
        in_specs=[tile(D)] + _mix_specs(l, R, tile_of),
        out_specs=[tile(C), tile(D), tile(D), tile(D), tile(D), tile(D), tile(PW), tile(PW), tile(PW)] + small_specs,
        out_shape=[jax.ShapeDtypeStruct((T, C), BF)] + [jax.ShapeDtypeStruct((T, D), BF)] * 5
        + [jax.ShapeDtypeStruct((T, PW), BF)] * 3 + [jax.ShapeDtypeStruct(shape, F32) for _, shape in MIX_SMALL_GRADS],
        scratch_shapes=_mix_scratch(R) + [pltpu.VMEM((R + POOL_HALO, PW), F32), pltpu.VMEM((R + CONV_HALO, PW), F32),
                                          pltpu.VMEM((R, PW), F32)],
        compiler_params=_cparams(("arbitrary",)),
    )(dout, z, z, z, z, h, sm["pool_w"], sm["pool_scale"], sm["sgu_ln_g"], sm["sgu_ln_b"], sm["sgu_w_s"], sm["sgu_b_sT"],
      sm["conv_k"], sm["conv_dw_b"], sm["conv_ln_g"], sm["conv_ln_b"], sm["mix_post_g"],
      gw["po"], gw["so"], gw["co"], gw["wo"])
    return outs[:9], dict(zip([n for n, _ in MIX_SMALL_GRADS], outs[9:]))


def _ple_fwd(h, p, l, pre_g, post_g, wpr, wpg):
    T = h.shape[0]
    R = min(512, T)

    def body(h_ref, p_ref, pg_ref, qg_ref, wpr_ref, wpg_ref, out_ref, n_ref, pb_ref, gp_ref, e_ref):
        hh = h_ref[...]
        xh, _ = _rms(hh)
        n = (xh * pg_ref[...]).astype(BF)
        n_ref[...] = n
        pb = p_ref[...].astype(BF)
        pb_ref[...] = pb
        e = _dot_nt(pb, wpr_ref[:, pl.ds(0, PLE)])
        gp = _dot(n, wpg_ref[...])
        gp_ref[...] = gp
        e_ref[...] = e
        qh, _ = _rms(_sigmoid(gp) * e)
        out_ref[...] = hh + qh * qg_ref[...]

    tile = lambda n: pl.BlockSpec((R, n), lambda i: (i, 0))
    return pl.pallas_call(
        body, name="ple_fwd", grid=(T // R,),
        in_specs=[tile(D), pl.BlockSpec((None, None, R, PLE), lambda i: (l, 0, i, 0)), _row(l, D), _row(l, D),
                  _whole(l, (D, PW)), _whole(l, (D, D))],
        out_specs=[tile(D), tile(D), tile(PLE), tile(D), tile(D)],
        out_shape=[jax.ShapeDtypeStruct((T, D), F32), jax.ShapeDtypeStruct((T, D), BF), jax.ShapeDtypeStruct((T, PLE), BF),
                   jax.ShapeDtypeStruct((T, D), F32), jax.ShapeDtypeStruct((T, D), F32)],
        compiler_params=_cparams(("parallel",)),
    )(h, p, pre_g, post_g, wpr, wpg)


def _ple_bwd(dout, h, gp, e, l, pre_g, post_g, wpg):
    T = h.shape[0]
    R = min(512, T)

    def body(do_ref, h_ref, gp_ref, e_ref, pg_ref, qg_ref, wpg_ref, dh_ref, de_ref, dgp_ref, dpg_ref, dqg_ref):
        @pl.when(pl.program_id(0) == 0)
        def _():
            dpg_ref[...] = jnp.zeros_like(dpg_ref)
            dqg_ref[...] = jnp.zeros_like(dqg_ref)

        do = do_ref[...]
        g = _sigmoid(gp_ref[...])
        e = e_ref[...]
        qh, qr = _rms(g * e)
        dq, dqg = _rms_bwd(qh, qr, qg_ref[...], do)
        dqg_ref[...] += dqg
        de_ref[...] = (dq * g).astype(BF)
        dgp = (dq * e * g * (1.0 - g)).astype(BF)
        dgp_ref[...] = dgp
        dn = _dot_nt(dgp, wpg_ref[...])
        xh, xr = _rms(h_ref[...])
        dx, dp = _rms_bwd(xh, xr, pg_ref[...], dn)
        dpg_ref[...] += dp
        dh_ref[...] = do + dx

    tile = lambda n: pl.BlockSpec((R, n), lambda i: (i, 0))
    acc = pl.BlockSpec((1, D), lambda i: (0, 0))
    return pl.pallas_call(
        body, name="ple_bwd", grid=(T // R,),
        in_specs=[tile(D), tile(D), tile(D), tile(D), _row(l, D), _row(l, D), _whole(l, (D, D))],
        out_specs=[tile(D), tile(D), tile(D), acc, acc],
        out_shape=[jax.ShapeDtypeStruct((T, D), F32), jax.ShapeDtypeStruct((T, D), BF), jax.ShapeDtypeStruct((T, D), BF),
                   jax.ShapeDtypeStruct((1, D), F32), jax.ShapeDtypeStruct((1, D), F32)],
        compiler_params=_cparams(("arbitrary",)),
    )(dout, h, gp, e, pre_g, post_g, wpg)


def _loss_head(y, target):
    T = y.shape[0]
    R = min(512, T)

    def body(y_ref, t_ref, loss_ref, dy_ref):
        @pl.when(pl.program_id(0) == 0)
        def _():
            loss_ref[...] = jnp.zeros_like(loss_ref)

        err = y_ref[...] - t_ref[0]
        dy_ref[...] = err * (1.0 / D)
        loss_ref[...] += 0.5 * jnp.sum(_mean(err * err), axis=0, keepdims=True)

    tile = pl.BlockSpec((R, D), lambda i: (i, 0))
    return pl.pallas_call(
        body, name="loss_head", grid=(T // R,),
        in_specs=[tile, pl.BlockSpec((1, R, D), lambda i: (0, i, 0))],
        out_specs=[pl.BlockSpec((1, 1), lambda i: (0, 0)), tile],
        out_shape=[jax.ShapeDtypeStruct((1, 1), F32), jax.ShapeDtypeStruct((T, D), F32)],
        compiler_params=_cparams(("arbitrary",)),
    )(y, target)


def _wgrad_f(slab, a, b, l, name, a_col0=0):
    off, r = LAYOUT[name][0], LAYOUT[name][1]
    per = FC // r
    nblk = NDEV // per
    a0 = a_col0 // FC

    T = a.shape[0]
    n = b.shape[1]
    kt = min(1024, T)

    def body(a_ref, b_ref, slab_ref, out_ref, acc_ref):
        k = pl.program_id(1)

        @pl.when(k == 0)
        def _():
            acc_ref[...] = jnp.zeros_like(acc_ref)

        acc_ref[...] += _dot_tn(a_ref[...], b_ref[...])

        @pl.when(k == pl.num_programs(1) - 1)
        def _():
            out_ref[...] = acc_ref[...].reshape(per, r, n)

    return pl.pallas_call(
        body, name="wgrad_" + name, grid=(nblk, T // kt),
        in_specs=[pl.BlockSpec((kt, FC), lambda i, k: (k, i + a0)), pl.BlockSpec((kt, n), lambda i, k: (k, 0)),
                  pl.BlockSpec(memory_space=pl.ANY)],
        out_specs=pl.BlockSpec((per, None, r, n), lambda i, k: (i, l, off // r, 0)),
        out_shape=jax.ShapeDtypeStruct(slab.shape, slab.dtype),
        scratch_shapes=[pltpu.VMEM((FC, n), F32)],
        input_output_aliases={2: 0},
        compiler_params=_cparams(("parallel", "arbitrary")),
    )(a, b, slab)


def _wgrad_d(slab, a, b, l, name):
    off, r, col0, width = LAYOUT[name][:4]
    T = a.shape[0]
    n = b.shape[1]
    kt = min(1024, T)

    def body(a_ref, b_ref, slab_ref, out_ref, acc_ref):
        k = pl.program_id(0)

        @pl.when(k == 0)
        def _():
            acc_ref[...] = jnp.zeros_like(acc_ref)

        acc_ref[...] += _dot_tn(a_ref[...], b_ref[...])

        @pl.when(k == pl.num_programs(0) - 1)
        def _():
            out_ref[...] = acc_ref[...].reshape(NDEV, r, n)

    return pl.pallas_call(
        body, name="wgrad_" + name, grid=(T // kt,),
        in_specs=[pl.BlockSpec((kt, D), lambda k: (k, 0)), pl.BlockSpec((kt, n), lambda k: (k, 0)),
                  pl.BlockSpec(memory_space=pl.ANY)],
        out_specs=pl.BlockSpec((NDEV, None, r, n), lambda k: (0, l, off // r, col0 // n)),
        out_shape=jax.ShapeDtypeStruct(slab.shape, slab.dtype),
        scratch_shapes=[pltpu.VMEM((D, n), F32)],
        input_output_aliases={2: 0},
        compiler_params=_cparams(("arbitrary",)),
    )(a, b, slab)


def _adamw(w, g, m, v):
    m = ADAM_B1 * m + (1.0 - ADAM_B1) * g
    v = ADAM_B2 * v + (1.0 - ADAM_B2) * (g * g)
    m_hat = m / (1.0 - ADAM_B1 ** ADAM_STEP)
    v_hat = v / (1.0 - ADAM_B2 ** ADAM_STEP)
    delta = -ADAM_LR * (m_hat / (jnp.sqrt(v_hat) + ADAM_EPS) + ADAM_WD * w)
    return delta, m, v


def _adam_big(name, q, recv, w, m, v):
    off, rows, col0, width, tr = LAYOUT[name]
    L = w.shape[0]
    wshape = w.shape[1:]
    nt = 4 if name == "w_in" else 1
    if tr:
        pblk, wblk = (rows, width // nt), (wshape[0] // nt, wshape[1])
        pmap = lambda l, t: (l, off // rows, col0 // (width // nt) + t)
    else:
        pblk, wblk = (rows // nt, width), (wshape[0] // nt, wshape[1])
        pmap = lambda l, t: (l, off // (rows // nt) + t, 0)

    def body(q_ref, r_ref, w_ref, m_ref, v_ref, g_out, d_out, m_out, v_out):
        g = q_ref[...].astype(F32) + r_ref[0].astype(F32) + r_ref[1].astype(F32) + r_ref[2].astype(F32)
        if tr:
            g = g.T
        d, mm, vv = _adamw(w_ref[...], g, m_ref[...], v_ref[...])
        g_out[...] = g
        d_out[...] = d
        m_out[...] = mm
        v_out[...] = vv

    wspec = pl.BlockSpec((None,) + wblk, lambda l, t: (l, t, 0))
    return pl.pallas_call(
        body, name="adam_" + name, grid=(L, nt),
        in_specs=[pl.BlockSpec((None, None) + pblk, lambda l, t: (0,) + pmap(l, t)),
                  pl.BlockSpec((3, None) + pblk, lambda l, t: (0,) + pmap(l, t)), wspec, wspec, wspec],
        out_specs=[wspec] * 4, out_shape=[jax.ShapeDtypeStruct(w.shape, F32)] * 4,
        compiler_params=_cparams(("parallel", "parallel")),
    )(q, recv, w, m, v)


def _adam_small(gall, w, m, v):
    rows = w.shape[0]
    tr = 32

    def body(g_ref, w_ref, m_ref, v_ref, g_out, d_out, m_out, v_out):
        g = g_ref[0]
        for k in range(1, NDEV):
            g = g + g_ref[k]
        d, mm, vv = _adamw(w_ref[...], g, m_ref[...], v_ref[...])
        g_out[...] = g
        d_out[...] = d
        m_out[...] = mm
        v_out[...] = vv

    spec = pl.BlockSpec((tr, D), lambda i: (i, 0))
    return pl.pallas_call(
        body, name="adam_small", grid=(rows // tr,),
        in_specs=[pl.BlockSpec((NDEV, tr, D), lambda i: (0, i, 0)), spec, spec, spec],
        out_specs=[spec] * 4, out_shape=[jax.ShapeDtypeStruct(w.shape, F32)] * 4,
        compiler_params=_cparams(("parallel",)),
    )(gall, w, m, v)


def _adam_plain(g, w, m, v):
    def body(g_ref, w_ref, m_ref, v_ref, d_out, m_out, v_out):
        d, mm, vv = _adamw(w_ref[...], g_ref[...], m_ref[...], v_ref[...])
        d_out[...] = d
        m_out[...] = mm
        v_out[...] = vv

    vm = pl.BlockSpec(memory_space=pltpu.VMEM)
    return pl.pallas_call(
        body, name="adam_conv_k", in_specs=[vm] * 4, out_specs=[vm] * 3,
        out_shape=[jax.ShapeDtypeStruct(w.shape, F32)] * 3,
    )(g, w, m, v)


def _pad_cols(a, n):
    return jnp.pad(a, [(0, 0)] * (a.ndim - 1) + [(0, n - a.shape[-1])])


def _pack_small(d, conv_k_full):
    L = d["pool_w"].shape[0]
    gains = jnp.stack([d[n].reshape(L, D) for n in GAINS], axis=1)
    halves = [_pad_cols(d[n].reshape(L, PW), D) for n in HALVES] + [jnp.zeros((L, D), F32)]
    halves = jnp.stack(halves, axis=1)
    ck = jnp.zeros((L, 16, D), F32) if conv_k_full is None else conv_k_full.reshape(L, 16, D)
    out = jnp.concatenate([gains, halves, d["pool_w"].reshape(L, 64, D), d["sgu_w_s"].reshape(L, 64, D), ck], axis=1)
    return out.reshape(L * SMALL_ROWS, D)


def _unpack_small(a, like):
    L = a.shape[0] // SMALL_ROWS
    a = a.reshape(L, SMALL_ROWS, D)
    out = {}
    for i, n in enumerate(GAINS):
        out[n] = a[:, i, :].reshape(like[n].shape)
    for i, n in enumerate(HALVES):
        out[n] = a[:, 8 + i, :PW].reshape(like[n].shape)
    out["pool_w"] = a[:, 16:80, :].reshape(like["pool_w"].shape)
    out["sgu_w_s"] = a[:, 80:144, :].reshape(like["sgu_w_s"].shape)
    out["conv_k_full"] = a[:, 144:160, :].reshape(L, CONV_HALO, PW)
    return out


WEIGHTS = ("ffn1_pre_g", "ffn1_w_gate", "ffn1_w_up", "ffn1_w_down", "ffn1_post_g", "mix_pre_g", "w_in", "pool_w", "pool_scale",
           "w_pool_out", "sgu_ln_g", "sgu_ln_b", "sgu_w_s", "sgu_b_s", "w_sgu_out", "conv_dw_k", "conv_dw_b", "conv_ln_g",
           "conv_ln_b", "w_conv_out", "w_out", "mix_post_g", "ffn2_pre_g", "ffn2_w_gate", "ffn2_w_up", "ffn2_w_down",
           "ffn2_post_g", "ple_w_proj", "ple_pre_g", "ple_w_gate", "ple_post_g")
SMALL = GAINS + HALVES + ("pool_w", "sgu_w_s")


def _fwd_bwd(x, p, target, w, conv_k, gw):
    L = w["w_in"].shape[0]
    T = x.shape[1]
    row = lambda a: a.reshape(L, 1, a.shape[-1])
    sm = {n: row(w[n]) for n in GAINS + ("pool_scale", "sgu_ln_g", "sgu_ln_b", "conv_dw_b", "conv_ln_g", "conv_ln_b")}
    sm.update(pool_w=w["pool_w"], sgu_w_s=w["sgu_w_s"], sgu_b_sT=w["sgu_b_s"].transpose(0, 2, 1), conv_k=conv_k)

    h = x[0]
    saved = []
    for l in range(L):
        s = {"h0": h}
        h, s["n1"], s["ab1"], s["f1"] = _ffn_fwd(h, l, sm["ffn1_pre_g"], sm["ffn1_post_g"], gw["gu1"], gw["d1"])
        s["h1"] = h
        s["nm"], s["z"] = _mix_in_fwd(h, l, sm["mix_pre_g"], gw["win"])
        h = _mix_core_fwd(s["z"], h, l, sm, gw)
        s["h2"] = h
        h, s["n2"], s["ab2"], s["f2"] = _ffn_fwd(h, l, sm["ffn2_pre_g"], sm["ffn2_post_g"], gw["gu2"], gw["d2"])
        s["h3"] = h
        h, s["np"], s["pb"], s["gp"], s["e"] = _ple_fwd(h, p, l, sm["ple_pre_g"], sm["ple_post_g"], gw["pr"], gw["pg"])
        saved.append(s)

    loss_part, dh = _loss_head(h, target)

    slab = jnp.zeros((NDEV, L, ROWS, D), F32)
    sg = {n: [None] * L for n in SMALL + ("conv_k", "sgu_b_sT")}
    for l in reversed(range(L)):
        s = saved[l]
        dh, de, dgp, sg["ple_pre_g"][l], sg["ple_post_g"][l] = _ple_bwd(
            dh, s["h3"], s["gp"], s["e"], l, sm["ple_pre_g"], sm["ple_post_g"], gw["pg"])
        slab = _wgrad_d(slab, s["np"], dgp, l, "ple_w_gate")
        slab = _wgrad_d(slab, de, s["pb"], l, "ple_w_proj")

        dh, dab, ss, df, sg["ffn2_pre_g"][l], sg["ffn2_post_g"][l] = _ffn_bwd(
            dh, s["h2"], s["ab2"], s["f2"], l, sm["ffn2_pre_g"], sm["ffn2_post_g"], gw["gu2"], gw["d2"])
        slab = _wgrad_f(slab, dab, s["n2"], l, "ffn2_w_gate")
        slab = _wgrad_f(slab, dab, s["n2"], l, "ffn2_w_up", a_col0=F)
        slab = _wgrad_f(slab, ss, df, l, "ffn2_w_down")

        (dz, mg, dob, dy0, dy1, dy2, pm, sgv, cv), g_mix = _mix_core_bwd(dh, s["z"], s["h1"], l, sm, gw)
        for n in g_mix:
            sg[n][l] = g_mix[n]
        dh, sg["mix_pre_g"][l] = _mix_in_bwd(dh, dz, s["h1"], l, sm["mix_pre_g"], gw["win"])
        slab = _wgrad_f(slab, dz, s["nm"], l, "w_in")
        slab = _wgrad_d(slab, mg, dob, l, "w_out")
        slab = _wgrad_d(slab, dy0, pm, l, "w_pool_out")
        slab = _wgrad_d(slab, dy1, sgv, l, "w_sgu_out")
        slab = _wgrad_d(slab, dy2, cv, l, "w_conv_out")

        dh, dab, ss, df, sg["ffn1_pre_g"][l], sg["ffn1_post_g"][l] = _ffn_bwd(
            dh, s["h0"], s["ab1"], s["f1"], l, sm["ffn1_pre_g"], sm["ffn1_post_g"], gw["gu1"], gw["d1"])
        slab = _wgrad_f(slab, dab, s["n1"], l, "ffn1_w_gate")
        slab = _wgrad_f(slab, dab, s["n1"], l, "ffn1_w_up", a_col0=F)
        slab = _wgrad_f(slab, ss, df, l, "ffn1_w_down")
    return loss_part, dh.reshape(1, T, D), slab, sg


def _step(x, p, target, w, m, v):
    L = w["w_in"].shape[0]
    ix, iy, ic = lax.axis_index("x"), lax.axis_index("y"), lax.axis_index("c")
    me = 4 * ix + 2 * iy + ic

    gw = _gather_weights(_prep(w))
    ck_local = jnp.pad(w["conv_dw_k"].reshape(L, CONV_K, 64), ((0, 0), (0, 1), (0, 0))).reshape(L * 2, D)
    ck_all = _gather_rows("allgather_conv_k", _pad_rows8(ck_local))[:, :L * 2]
    conv_k = ck_all.reshape(NDEV, L, CONV_HALO, 64).transpose(1, 2, 0, 3).reshape(L, CONV_HALO, PW)

    loss_part, grad_x, slab, sg = _fwd_bwd(x, p, target, w, conv_k, gw)
    loss = lax.psum(loss_part[0, 0], ("x", "y", "c"))

    q = _chip_sums(slab, _exchange_sibling(slab))
    recv = _exchange_chips(q)
    res = {n: _adam_big(n, q, recv, w[n], m[n], v[n]) for n in BIG}

    sgrads = {n: jnp.stack(sg[n]) for n in SMALL if n != "sgu_b_s"}
    sgrads["sgu_b_s"] = jnp.stack(sg["sgu_b_sT"]).transpose(0, 2, 1)
    gall = _gather_rows("allgather_small_grads", _pack_small(sgrads, jnp.stack(sg["conv_k"])))
    outs = _adam_small(gall, _pack_small(w, None), _pack_small(m, None), _pack_small(v, None))
    unpacked = [_unpack_small(o, w) for o in outs]
    for n in SMALL:
        res[n] = tuple(u[n] for u in unpacked)

    gk = lax.dynamic_slice_in_dim(unpacked[0]["conv_k_full"][:, :CONV_K, :], me * 64, 64, axis=2)
    shp = w["conv_dw_k"].shape
    flat = lambda a: a.reshape(L * CONV_K, 64)
    dk, mk, vk = _adam_plain(flat(gk), flat(w["conv_dw_k"]), flat(m["conv_dw_k"]), flat(v["conv_dw_k"]))
    res["conv_dw_k"] = (gk.reshape(shp), dk.reshape(shp), mk.reshape(shp), vk.reshape(shp))

    return (loss, grad_x, *[res[n][0] for n in WEIGHTS], *[res[n][1] for n in WEIGHTS],
            *[res[n][2] for n in WEIGHTS], *[res[n][3] for n in WEIGHTS])


def _pad_rows8(a):
    return jnp.pad(a, ((0, (-a.shape[0]) % 8), (0, 0)))


def kernel(x, p, ffn1_pre_g, ffn1_w_gate, ffn1_w_up, ffn1_w_down, ffn1_post_g, mix_pre_g, w_in, pool_w, pool_scale, w_pool_out, sgu_ln_g, sgu_ln_b, sgu_w_s, sgu_b_s, w_sgu_out, conv_dw_k, conv_dw_b, conv_ln_g, conv_ln_b, w_conv_out, w_out, mix_post_g, ffn2_pre_g, ffn2_w_gate, ffn2_w_up, ffn2_w_down, ffn2_post_g, ple_w_proj, ple_pre_g, ple_w_gate, ple_post_g, loss_target, m_ffn1_pre_g, m_ffn1_w_gate, m_ffn1_w_up, m_ffn1_w_down, m_ffn1_post_g, m_mix_pre_g, m_w_in, m_pool_w, m_pool_scale, m_w_pool_out, m_sgu_ln_g, m_sgu_ln_b, m_sgu_w_s, m_sgu_b_s, m_w_sgu_out, m_conv_dw_k, m_conv_dw_b, m_conv_ln_g, m_conv_ln_b, m_w_conv_out, m_w_out, m_mix_post_g, m_ffn2_pre_g, m_ffn2_w_gate, m_ffn2_w_up, m_ffn2_w_down, m_ffn2_post_g, m_ple_w_proj, m_ple_pre_g, m_ple_w_gate, m_ple_post_g, v_ffn1_pre_g, v_ffn1_w_gate, v_ffn1_w_up, v_ffn1_w_down, v_ffn1_post_g, v_mix_pre_g, v_w_in, v_pool_w, v_pool_scale, v_w_pool_out, v_sgu_ln_g, v_sgu_ln_b, v_sgu_w_s, v_sgu_b_s, v_w_sgu_out, v_conv_dw_k, v_conv_dw_b, v_conv_ln_g, v_conv_ln_b, v_w_conv_out, v_w_out, v_mix_post_g, v_ffn2_pre_g, v_ffn2_w_gate, v_ffn2_w_up, v_ffn2_w_down, v_ffn2_post_g, v_ple_w_proj, v_ple_pre_g, v_ple_w_gate, v_ple_post_g):
    args = dict(locals())
    w = {n: args[n] for n in WEIGHTS}
    m = {n: args["m_" + n] for n in WEIGHTS}
    v = {n: args["v_" + n] for n in WEIGHTS}
    return _step(x, p, loss_target, w, m, v)
```

```python
import functools

import jax
import jax.numpy as jnp
from jax import lax
from jax.experimental import pallas as pl
from jax.experimental.pallas import tpu as pltpu

D = 1024
F = 2816
C = 5632
PW = 512
PLE = 256
NDEV = 8
CHUNK = 128
POOL_WINDOWS = (2, 4, 8, 16)
CONV_K = 31
POOL_HALO = 16
CONV_HALO = 32
EPS = 1e-6
OFF_U, OFF_V, OFF_A, OFF_B, OFF_G = 512, 1024, 1536, 2048, 2560

ADAM_LR, ADAM_B1, ADAM_B2, ADAM_EPS, ADAM_WD, ADAM_STEP = 0.001, 0.9, 0.999, 1e-08, 0.01, 10

BF = jnp.bfloat16
F32 = jnp.float32
VMEM_LIMIT = 56 * 1024 * 1024
MESH = pl.DeviceIdType.MESH
INV_SQRT2 = 0.7071067811865476
INV_SQRT_2PI = 0.3989422804014327

ROWS = 3328
LAYOUT = {
    "w_in": (0, 704, 0, 1024, True),
    "ffn1_w_gate": (704, 352, 0, 1024, True),
    "ffn1_w_up": (1056, 352, 0, 1024, True),
    "ffn1_w_down": (1408, 352, 0, 1024, False),
    "ffn2_w_gate": (1760, 352, 0, 1024, True),
    "ffn2_w_up": (2112, 352, 0, 1024, True),
    "ffn2_w_down": (2464, 352, 0, 1024, False),
    "w_pool_out": (2816, 128, 0, 512, True),
    "w_sgu_out": (2816, 128, 512, 512, True),
    "w_conv_out": (2944, 128, 0, 512, True),
    "ple_w_proj": (2944, 128, 512, 256, True),
    "w_out": (3072, 128, 0, 1024, False),
    "ple_w_gate": (3200, 128, 0, 1024, False),
}
BIG = tuple(LAYOUT)
GATHERED = {
    "win": (C, D), "gu1": (2 * F, D), "d1": (F, D), "gu2": (2 * F, D), "d2": (F, D),
    "po": (D, PW), "so": (D, PW), "co": (D, PW), "pr": (D, PW), "wo": (D, D), "pg": (D, D),
}
PIECES = (
    ("win", 0, 0, 704, 0, 1024), ("gu1", 0, 704, 352, 0, 1024), ("gu1", F, 1056, 352, 0, 1024),
    ("d1", 0, 1408, 352, 0, 1024), ("gu2", 0, 1760, 352, 0, 1024), ("gu2", F, 2112, 352, 0, 1024),
    ("d2", 0, 2464, 352, 0, 1024), ("po", 0, 2816, 128, 0, 512), ("so", 0, 2816, 128, 512, 512),
    ("co", 0, 2944, 128, 0, 512), ("pr", 0, 2944, 128, 512, 512), ("wo", 0, 3072, 128, 0, 1024),
    ("pg", 0, 3200, 128, 0, 1024),
)
GAINS = ("ffn1_pre_g", "ffn1_post_g", "mix_pre_g", "mix_post_g", "ffn2_pre_g", "ffn2_post_g", "ple_pre_g", "ple_post_g")
HALVES = ("pool_scale", "sgu_ln_g", "sgu_ln_b", "sgu_b_s", "conv_dw_b", "conv_ln_g", "conv_ln_b")
SMALL_ROWS = 160


def _cparams(sem=None, **kw):
    if sem is not None:
        kw["dimension_semantics"] = sem
    return pltpu.CompilerParams(vmem_limit_bytes=VMEM_LIMIT, **kw)


def _whole(l, shape):
    nd = len(shape)
    return pl.BlockSpec((None,) + tuple(shape), lambda *_: (l,) + (0,) * nd, pipeline_mode=pl.Buffered(1))


def _full(shape):
    nd = len(shape)
    return pl.BlockSpec(tuple(shape), lambda *_: (0,) * nd, pipeline_mode=pl.Buffered(1))


def _row(l, n):
    return pl.BlockSpec((None, 1, n), lambda *_: (l, 0, 0))


def _dot(a, b):
    return jnp.dot(a, b, preferred_element_type=F32)


def _dot_nt(a, b):
    return lax.dot_general(a, b, (((1,), (1,)), ((), ())), preferred_element_type=F32)


def _dot_tn(a, b):
    return lax.dot_general(a, b, (((0,), (0,)), ((), ())), preferred_element_type=F32)


def _mean(x):
    return jnp.mean(x, axis=-1, keepdims=True)


def _colsum(x):
    return jnp.sum(x, axis=0, keepdims=True)


def _rms(x):
    r = lax.rsqrt(_mean(x * x) + EPS)
    return x * r, r


def _rms_bwd(xh, r, g, dy):
    dxh = dy * g
    return r * (dxh - xh * _mean(dxh * xh)), _colsum(dy * xh)


def _ln(x):
    xc = x - _mean(x)
    r = lax.rsqrt(_mean(xc * xc) + EPS)
    return xc * r, r


def _ln_bwd(xh, r, g, dy):
    dxh = dy * g
    return r * (dxh - _mean(dxh) - xh * _mean(dxh * xh)), _colsum(dy * xh), _colsum(dy)


def _sigmoid(x):
    return jax.nn.sigmoid(x)


def _gelu(x):
    return 0.5 * x * (1.0 + lax.erf(x * INV_SQRT2))


def _gelu_grad(x):
    return 0.5 * (1.0 + lax.erf(x * INV_SQRT2)) + x * jnp.exp(-0.5 * x * x) * INV_SQRT_2PI


def _prep(w, l):
    def body(*refs):
        ins, out = dict(zip(BIG, refs[:-1])), refs[-1]
        for name, (off, rows, col0, width, tr) in LAYOUT.items():
            v = ins[name][...]
            if tr:
                v = v.T
            out[pl.ds(off, rows), pl.ds(col0, width)] = v.astype(BF)
        out[pl.ds(2944, 128), pl.ds(768, 256)] = jnp.zeros((128, 256), BF)

    in_specs = [pl.BlockSpec((None,) + w[n].shape[1:], lambda i: (l, 0, 0)) for n in BIG]
    return pl.pallas_call(
        body, name="prep", grid=(1,), in_specs=in_specs,
        out_specs=pl.BlockSpec((ROWS, D), lambda i: (0, 0)),
        out_shape=jax.ShapeDtypeStruct((ROWS, D), BF),
        compiler_params=_cparams(("arbitrary",)),
    )(*[w[n] for n in BIG])


def _place():
    x, y, c = lax.axis_index("x"), lax.axis_index("y"), lax.axis_index("c")
    chips = [(1 - x, y), (x, 1 - y), (1 - x, 1 - y)]
    return x, y, c, chips


def _allgather(name, src, dsts, pieces):
    npc = len(pieces)

    def body(src_ref, *rest):
        outs, (send_sems, recv_sems, local_sems) = rest[:len(dsts)], rest[len(dsts):]
        x, y, c, chips = _place()
        me, sibling = (x, y, c), (x, y, 1 - c)

        def shard_of(dev):
            return 4 * dev[0] + 2 * dev[1] + dev[2]

        def copies(k, block, to, from_src):
            res = []
            for di, dst_fn, src_sl in pieces:
                dst = outs[di].at[dst_fn(shard_of(block))]
                s = src_ref.at[src_sl] if from_src else dst
                res.append(pltpu.make_async_remote_copy(src_ref=s, dst_ref=dst, send_sem=send_sems.at[k],
                                                        recv_sem=recv_sems.at[k], device_id=to, device_id_type=MESH))
            return res

        def whole(k):
            return pltpu.make_async_remote_copy(src_ref=src_ref, dst_ref=src_ref, send_sem=send_sems.at[k],
                                                recv_sem=recv_sems.at[k], device_id=me, device_id_type=MESH)

        mine = [pltpu.make_async_copy(src_ref.at[src_sl], outs[di].at[dst_fn(shard_of(me))], local_sems.at[i])
                for i, (di, dst_fn, src_sl) in enumerate(pieces)]
        for cp in mine:
            cp.start()
        for cp in copies(0, me, sibling, True):
            cp.start()
        for j, chip in enumerate(chips):
            for cp in copies(1 + j, me, (*chip, c), True):
                cp.start()
        for j, chip in enumerate(chips):
            whole(1 + j).wait_recv()
            for cp in copies(4 + j, (*chip, c), sibling, False):
                cp.start()
        for k in (0, 4, 5, 6):
            whole(k).wait_recv()
        for k in range(7):
            whole(k).wait_send()
        for cp in mine:
            cp.wait()

    any_spec = pl.BlockSpec(memory_space=pl.ANY)
    return pl.pallas_call(
        body, name=name, in_specs=[any_spec], out_specs=[any_spec] * len(dsts), out_shape=dsts,
        scratch_shapes=[pltpu.SemaphoreType.DMA((7,)), pltpu.SemaphoreType.DMA((7,)), pltpu.SemaphoreType.DMA((npc,))],
    )(src)


HBM = pl.BlockSpec(memory_space=pltpu.HBM)
SEM = pl.BlockSpec(memory_space=pltpu.SEMAPHORE)
ANY = pl.BlockSpec(memory_space=pl.ANY)
EFFECT = pltpu.SideEffectType.DATAFLOW_SIDE_EFFECTING
GNAMES = tuple(GATHERED)
NG = len(GNAMES)


def _in_hbm(a):
    return pltpu.with_memory_space_constraint(a, pltpu.HBM)


def _piece_refs(packed_ref, land_refs, shard):
    out = []
    for gname, row0, poff, prow, pcol, width in PIECES:
        land = land_refs[GNAMES.index(gname)]
        out.append((packed_ref.at[pl.ds(poff, prow), pl.ds(pcol, width)], land.at[pl.ds(row0 + shard * prow, prow), :]))
    return out


def _gather_start(l, packed, lands, after):
    def body(packed_ref, *rest):
        land_refs, send_sems, recv_sems = rest[:NG], rest[NG + 1], rest[NG + 2]
        token = rest[-1]
        x, y, c, chips = _place()
        targets = [(x, y, 1 - c)] + [(*chip, c) for chip in chips]
        for k, to in enumerate(targets):
            for src, dst in _piece_refs(packed_ref, land_refs, 4 * x + 2 * y + c):
                pltpu.make_async_remote_copy(src_ref=src, dst_ref=dst, send_sem=send_sems.at[k], recv_sem=recv_sems.at[k],
                                             device_id=to, device_id_type=MESH).start()
        token[...] = jnp.zeros_like(token)

    hbm = lambda a: pltpu.HBM(a.shape, a.dtype)
    outs = pl.pallas_call(
        body, name=f"gather_start_{l}",
        out_shape=(pltpu.SemaphoreType.DMA((4,)), pltpu.SemaphoreType.DMA((4,)), hbm(packed), *[hbm(a) for a in lands],
                   jax.ShapeDtypeStruct((8, 128), F32)),
        in_specs=(HBM,) * (1 + NG) + (ANY,),
        out_specs=(SEM, SEM) + (HBM,) * (1 + NG) + (pl.BlockSpec(memory_space=pltpu.VMEM),),
        input_output_aliases={i: 2 + i for i in range(1 + NG)},
        compiler_params=pltpu.CompilerParams(has_side_effects=EFFECT),
    )(_in_hbm(packed), *[_in_hbm(a) for a in lands], after)
    return outs[0], outs[1], outs[2], list(outs[3:3 + NG]), outs[-1]


def _gather_wait(l, send_sems, recv_sems, packed, lands, after):
    def body(packed_ref, *rest):
        send_sems, recv_sems = rest[NG], rest[NG + 1]
        x, y, c, _ = _place()
        for k in range(4):
            cp = pltpu.make_async_remote_copy(src_ref=packed_ref, dst_ref=packed_ref, send_sem=send_sems.at[k],
                                              recv_sem=recv_sems.at[k], device_id=(x, y, c), device_id_type=MESH)
            cp.wait_send()
            cp.wait_recv()

    hbm = lambda a: pltpu.HBM(a.shape, a.dtype)
    outs = pl.pallas_call(
        body, name=f"gather_wait_{l}",
        out_shape=(hbm(packed), *[hbm(a) for a in lands]),
        in_specs=(HBM,) * (1 + NG) + (SEM, SEM, ANY), out_specs=(HBM,) * (1 + NG),
        input_output_aliases={i: i for i in range(1 + NG)},
        compiler_params=pltpu.CompilerParams(has_side_effects=EFFECT),
    )(packed, *lands, send_sems, recv_sems, after)
    return outs[0], list(outs[1:])


def _gather_forward(l, packed, lands):
    npc = len(PIECES)

    def body(packed_ref, *rest):
        land_refs, (send_sems, recv_sems, local_sems) = rest[NG:2 * NG], rest[2 * NG:]
        x, y, c, chips = _place()
        mine = [pltpu.make_async_copy(src, dst, local_sems.at[i])
                for i, (src, dst) in enumerate(_piece_refs(packed_ref, land_refs, 4 * x + 2 * y + c))]
        for cp in mine:
            cp.start()
        for j, (cx, cy) in enumerate(chips):
            for _, rows in _piece_refs(packed_ref, land_refs, 4 * cx + 2 * cy + c):
                pltpu.make_async_remote_copy(src_ref=rows, dst_ref=rows, send_sem=send_sems.at[j], recv_sem=recv_sems.at[j],
                                             device_id=(x, y, 1 - c), device_id_type=MESH).start()
        for j in range(3):
            cp = pltpu.make_async_remote_copy(src_ref=packed_ref, dst_ref=packed_ref, send_sem=send_sems.at[j],
                                              recv_sem=recv_sems.at[j], device_id=(x, y, c), device_id_type=MESH)
            cp.wait_recv()
            cp.wait_send()
        for cp in mine:
            cp.wait()

    outs = pl.pallas_call(
        body, name=f"gather_forward_{l}", in_specs=[ANY] * (1 + NG), out_specs=[ANY] * NG,
        out_shape=[jax.ShapeDtypeStruct(a.shape, a.dtype) for a in lands],
        input_output_aliases={1 + i: i for i in range(NG)},
        scratch_shapes=[pltpu.SemaphoreType.DMA((3,)), pltpu.SemaphoreType.DMA((3,)), pltpu.SemaphoreType.DMA((npc,))],
    )(packed, *lands)
    return dict(zip(GNAMES, outs))


RELATIONS = ((0, 0, 1), (1, 0, 0), (0, 1, 0), (1, 1, 0), (1, 0, 1), (0, 1, 1), (1, 1, 1))


def _peers():
    x, y, c = lax.axis_index("x"), lax.axis_index("y"), lax.axis_index("c")
    return [((1 - x) if fx else x, (1 - y) if fy else y, (1 - c) if fc else c) for fx, fy, fc in RELATIONS]


def _scatter_start(l, slab, recv):
    def body(slab_ref, recv_ref, send_sems, recv_sems, slab_out, recv_out, token):
        for k, to in enumerate(_peers()):
            pltpu.make_async_remote_copy(src_ref=slab_ref.at[4 * to[0] + 2 * to[1] + to[2], l], dst_ref=recv_ref.at[l, k],
                                         send_sem=send_sems.at[k], recv_sem=recv_sems.at[k],
                                         device_id=to, device_id_type=MESH).start()
        token[...] = jnp.zeros_like(token)

    outs = pl.pallas_call(
        body, name=f"scatter_start_{l}",
        out_shape=(pltpu.SemaphoreType.DMA((7,)), pltpu.SemaphoreType.DMA((7,)), pltpu.HBM(slab.shape, slab.dtype),
                   pltpu.HBM(recv.shape, recv.dtype), jax.ShapeDtypeStruct((8, 128), F32)),
        in_specs=(HBM, HBM), out_specs=(SEM, SEM, HBM, HBM, pl.BlockSpec(memory_space=pltpu.VMEM)),
        input_output_aliases={0: 2, 1: 3},
        compiler_params=pltpu.CompilerParams(has_side_effects=EFFECT),
    )(_in_hbm(slab), _in_hbm(recv))
    return outs[:4]


def _scatter_wait(l, send_sems, recv_sems, slab, recv):
    def body(slab_ref, recv_ref, send_sems, recv_sems, slab_out, recv_out):
        for k, to in enumerate(_peers()):
            cp = pltpu.make_async_remote_copy(src_ref=slab_ref.at[0, l], dst_ref=recv_ref.at[l, k], send_sem=send_sems.at[k],
                                              recv_sem=recv_sems.at[k], device_id=to, device_id_type=MESH)
            cp.wait_send()
            cp.wait_recv()

    return pl.pallas_call(
        body, name=f"scatter_wait_{l}",
        out_shape=(pltpu.HBM(slab.shape, slab.dtype), pltpu.HBM(recv.shape, recv.dtype)),
        in_specs=(HBM, HBM, SEM, SEM), out_specs=(HBM, HBM), input_output_aliases={0: 0, 1: 1},
        compiler_params=pltpu.CompilerParams(has_side_effects=EFFECT),
    )(slab, recv, send_sems, recv_sems)


def _gather_rows(name, src):
    dst = jax.ShapeDtypeStruct((NDEV,) + src.shape, src.dtype)
    full = (slice(None), slice(None), slice(None))
    pieces = [(0, lambda shard: (pl.ds(shard, 1), slice(None), slice(None)), full)]
    return _allgather(name, src.reshape((1,) + src.shape), [dst], pieces)[0]


FC = 1408


def _ffn_fwd(h, l, pre_g, post_g, wgu, wd):
    T = h.shape[0]
    R = min(512, T)

    def body(h_ref, pg_ref, qg_ref, wgu_ref, wd_ref, out_ref, n_ref, ab_ref, f_ref):
        hh = h_ref[...]
        xh, _ = _rms(hh)
        n = (xh * pg_ref[...]).astype(BF)
        n_ref[...] = n
        f = jnp.zeros((R, D), F32)
        for ci in range(F // FC):
            a = _dot_nt(n, wgu_ref[pl.ds(ci * FC, FC), :])
            b = _dot_nt(n, wgu_ref[pl.ds(F + ci * FC, FC), :])
            ab_ref[:, pl.ds(ci * FC, FC)] = a.astype(BF)
            ab_ref[:, pl.ds(F + ci * FC, FC)] = b.astype(BF)
            s = (a * _sigmoid(a) * b).astype(BF)
            f = f + _dot(s, wd_ref[pl.ds(ci * FC, FC), :])
        f_ref[...] = f
        fh, _ = _rms(f)
        out_ref[...] = hh + 0.5 * (fh * qg_ref[...])

    tile = lambda n: pl.BlockSpec((R, n), lambda i: (i, 0))
    return pl.pallas_call(
        body, name="ffn_fwd", grid=(T // R,),
        in_specs=[tile(D), _row(l, D), _row(l, D), _full((2 * F, D)), _full((F, D))],
        out_specs=[tile(D), tile(D), tile(2 * F), tile(D)],
        out_shape=[jax.ShapeDtypeStruct((T, D), F32), jax.ShapeDtypeStruct((T, D), BF),
                   jax.ShapeDtypeStruct((T, 2 * F), BF), jax.ShapeDtypeStruct((T, D), F32)],
        compiler_params=_cparams(("parallel",)),
    )(h, pre_g, post_g, wgu, wd)


def _ffn_bwd(dout, h, ab, f, l, pre_g, post_g, wgu, wd):
    T = h.shape[0]
    R = min(256, T)

    def body(do_ref, h_ref, ab_ref, f_ref, pg_ref, qg_ref, wgu_ref, wd_ref,
             dh_ref, dab_ref, s_ref, df_ref, dpg_ref, dqg_ref):
        i = pl.program_id(0)

        @pl.when(i == 0)
        def _():
            dpg_ref[...] = jnp.zeros_like(dpg_ref)
            dqg_ref[...] = jnp.zeros_like(dqg_ref)

        do = do_ref[...]
        fh, fr = _rms(f_ref[...])
        df, dq = _rms_bwd(fh, fr, qg_ref[...], 0.5 * do)
        dqg_ref[...] += dq
        df = df.astype(BF)
        df_ref[...] = df
        dn = jnp.zeros((R, D), F32)
        for ci in range(F // FC):
            ga, gb = pl.ds(ci * FC, FC), pl.ds(F + ci * FC, FC)
            ds = _dot_nt(df, wd_ref[ga, :])
            a = ab_ref[:, ga].astype(F32)
            b = ab_ref[:, gb].astype(F32)
            sg = _sigmoid(a)
            sil = a * sg
            s_ref[:, ga] = (sil * b).astype(BF)
            da = (ds * b * (sg * (1.0 + a * (1.0 - sg)))).astype(BF)
            db = (ds * sil).astype(BF)
            dab_ref[:, ga] = da
            dab_ref[:, gb] = db
            dn = dn + _dot(da, wgu_ref[ga, :]) + _dot(db, wgu_ref[gb, :])
        xh, xr = _rms(h_ref[...])
        dx, dp = _rms_bwd(xh, xr, pg_ref[...], dn)
        dpg_ref[...] += dp
        dh_ref[...] = do + dx

    tile = lambda n: pl.BlockSpec((R, n), lambda i: (i, 0))
    acc = pl.BlockSpec((1, D), lambda i: (0, 0))
    return pl.pallas_call(
        body, name="ffn_bwd", grid=(T // R,),
        in_specs=[tile(D), tile(D), tile(2 * F), tile(D), _row(l, D), _row(l, D), _full((2 * F, D)), _full((F, D))],
        out_specs=[tile(D), tile(2 * F), tile(F), tile(D), acc, acc],
        out_shape=[jax.ShapeDtypeStruct((T, D), F32), jax.ShapeDtypeStruct((T, 2 * F), BF),
                   jax.ShapeDtypeStruct((T, F), BF), jax.ShapeDtypeStruct((T, D), BF),
                   jax.ShapeDtypeStruct((1, D), F32), jax.ShapeDtypeStruct((1, D), F32)],
        compiler_params=_cparams(("arbitrary",)),
    )(dout, h, ab, f, pre_g, post_g, wgu, wd)


def _mix_in_fwd(h, l, pre_g, win):
    T = h.shape[0]
    R = min(256, T)

    def body(h_ref, pg_ref, w_ref, n_ref, z_ref):
        xh, _ = _rms(h_ref[...])
        n = (xh * pg_ref[...]).astype(BF)
        n_ref[...] = n
        for ci in range(C // FC):
            z_ref[:, pl.ds(ci * FC, FC)] = _dot_nt(n, w_ref[pl.ds(ci * FC, FC), :])

    tile = lambda n: pl.BlockSpec((R, n), lambda i: (i, 0))
    return pl.pallas_call(
        body, name="mix_in_fwd", grid=(T // R,),
        in_specs=[tile(D), _row(l, D), _full((C, D))],
        out_specs=[tile(D), tile(C)],
        out_shape=[jax.ShapeDtypeStruct((T, D), BF), jax.ShapeDtypeStruct((T, C), F32)],
        compiler_params=_cparams(("parallel",)),
    )(h, pre_g, win)


def _mix_in_bwd(dout, dz, h, l, pre_g, win):
    T = h.shape[0]
    R = min(512, T)

    def body(do_ref, dz_ref, h_ref, pg_ref, w_ref, dh_ref, dpg_ref):
        @pl.when(pl.program_id(0) == 0)
        def _():
            dpg_ref[...] = jnp.zeros_like(dpg_ref)

        dn = _dot(dz_ref[...], w_ref[...])
        xh, xr = _rms(h_ref[...])
        dx, dp = _rms_bwd(xh, xr, pg_ref[...], dn)
        dpg_ref[...] += dp
        dh_ref[...] = do_ref[...] + dx

    tile = lambda n: pl.BlockSpec((R, n), lambda i: (i, 0))
    return pl.pallas_call(
        body, name="mix_in_bwd", grid=(T // R,),
        in_specs=[tile(D), tile(C), tile(D), _row(l, D), _full((C, D))],
        out_specs=[tile(D), pl.BlockSpec((1, D), lambda i: (0, 0))],
        out_shape=[jax.ShapeDtypeStruct((T, D), F32), jax.ShapeDtypeStruct((1, D), F32)],
        compiler_params=_cparams(("arbitrary",)),
    )(dout, dz, h, pre_g, win)


def _mix_specs(l, R, tile_of):
    def halo(rows, col_block):
        per = R // rows
        return pl.BlockSpec((rows, PW), lambda i: (jnp.maximum(tile_of(i) * per - 1, 0), col_block))
    return [
        pl.BlockSpec((R, C), lambda i: (tile_of(i), 0)),
        halo(POOL_HALO, 0), halo(CONV_HALO, 3), halo(CONV_HALO, 4),
        pl.BlockSpec((R, D), lambda i: (tile_of(i), 0)),
        _whole(l, (4, CHUNK, CHUNK)), _row(l, PW),
        _row(l, PW), _row(l, PW), _whole(l, (4, CHUNK, CHUNK)), _whole(l, (CHUNK, 4)),
        _whole(l, (CONV_HALO, PW)), _row(l, PW), _row(l, PW), _row(l, PW),
        _row(l, D),
        _full((D, PW)), _full((D, PW)), _full((D, PW)), _full((D, D)),
    ]


class _MixFwd:
    def __init__(self, R, tile, refs, scratch):
        (z_ref, zph_ref, zah_ref, zbh_ref, _h, pw_ref, ps_ref, lg_ref, lb_ref, ws_ref, bst_ref,
         ck_ref, cb_ref, cg_ref, cbb_ref, _qg, wpo_ref, wso_ref, wco_ref, wout_ref) = refs
        pbuf, xbuf, sbuf, pm_ref, sg_ref, cv_ref = scratch
        first = tile == 0
        tglob = tile * R + lax.broadcasted_iota(jnp.int32, (R, 1), 0)
        pbuf[pl.ds(0, POOL_HALO), :] = jnp.where(first, 0.0, zph_ref[...])
        pbuf[pl.ds(POOL_HALO, R), :] = z_ref[:, pl.ds(0, PW)]
        self.pooled, self.yg, self.cnt = [], [], []
        for gi, w in enumerate(POOL_WINDOWS):
            cols = pl.ds(gi * CHUNK, CHUNK)
            x = pbuf[pl.ds(POOL_HALO, R), cols]
            acc = x
            for j in range(1, w):
                acc = acc + pbuf[pl.ds(POOL_HALO - j, R), cols]
            cnt = jnp.minimum(tglob + 1, w).astype(F32)
            pooled = (acc / cnt - x).astype(BF)
            yg = _dot(pooled, pw_ref[gi].astype(BF))
            pm_ref[:, cols] = (yg * ps_ref[:, cols]).astype(BF)
            self.pooled.append(pooled)
            self.yg.append(yg)
            self.cnt.append(cnt)
        zu, zv = z_ref[:, pl.ds(OFF_U, PW)], z_ref[:, pl.ds(OFF_V, PW)]
        self.u = _gelu(zu)
        self.vh, self.vr = _ln(_gelu(zv))
        self.vln = (self.vh * lg_ref[...] + lb_ref[...]).astype(BF)
        tt = lax.broadcasted_iota(jnp.int32, (CHUNK, CHUNK), 0)
        ss = lax.broadcasted_iota(jnp.int32, (CHUNK, CHUNK), 1)
        self.causal = tt >= ss
        self.wc = [jnp.where(self.causal, ws_ref[hd], 0.0).astype(BF) for hd in range(4)]
        for ck in range(R // CHUNK):
            for hd in range(4):
                rows, cols = pl.ds(ck * CHUNK, CHUNK), pl.ds(hd * CHUNK, CHUNK)
                blk = self.vln[ck * CHUNK:(ck + 1) * CHUNK, hd * CHUNK:(hd + 1) * CHUNK]
                sbuf[rows, cols] = _dot(self.wc[hd], blk) + bst_ref[:, pl.ds(hd, 1)]
        self.s = sbuf[...]
        sg_ref[...] = (self.u * self.s).astype(BF)
        self.za = z_ref[:, pl.ds(OFF_A, PW)]
        self.sgb = _sigmoid(z_ref[:, pl.ds(OFF_B, PW)])
        xbuf[pl.ds(0, CONV_HALO), :] = jnp.where(first, 0.0, zah_ref[...] * _sigmoid(zbh_ref[...]))
        xbuf[pl.ds(CONV_HALO, R), :] = self.za * self.sgb
        y = jnp.zeros((R, PW), F32) + cb_ref[...]
        for k in range(CONV_K):
            y = y + xbuf[pl.ds(CONV_HALO - (CONV_K - 1) + k, R), :] * ck_ref[pl.ds(k, 1), :]
        self.yh, self.yr = _ln(y)
        self.yl = self.yh * cg_ref[...] + cbb_ref[...]
        self.sy = _sigmoid(self.yl)
        cv_ref[...] = (self.yl * self.sy).astype(BF)
        self.g = [_sigmoid(z_ref[:, pl.ds(OFF_G + j * D, D)]) for j in range(3)]
        self.y = [_dot_nt(pm_ref[...], wpo_ref[...]), _dot_nt(sg_ref[...], wso_ref[...]), _dot_nt(cv_ref[...], wco_ref[...])]
        self.merged = (self.g[0] * self.y[0] + self.g[1] * self.y[1] + self.g[2] * self.y[2]).astype(BF)
        self.o = _dot(self.merged, wout_ref[...])


def _mix_scratch(R):
    return [pltpu.VMEM((R + POOL_HALO, PW), F32), pltpu.VMEM((R + CONV_HALO, PW), F32), pltpu.VMEM((R, PW), F32)]


def _mix_core_fwd(z, h, l, sm, gw):
    T = h.shape[0]
    R = min(256, T)

    def body(*refs):
        ins, out_ref, scratch = refs[:20], refs[20], refs[21:]
        fw = _MixFwd(R, pl.program_id(0), ins, scratch)
        oh, _ = _rms(fw.o)
        out_ref[...] = ins[4][...] + oh * ins[15][...]

    act = pltpu.VMEM((R, PW), BF)
    return pl.pallas_call(
        body, name="mix_core_fwd", grid=(T // R,),
        in_specs=_mix_specs(l, R, lambda i: i),
        out_specs=pl.BlockSpec((R, D), lambda i: (i, 0)),
        out_shape=jax.ShapeDtypeStruct((T, D), F32),
        scratch_shapes=_mix_scratch(R) + [act, act, act],
        compiler_params=_cparams(("arbitrary",)),
    )(z, z, z, z, h, sm["pool_w"], sm["pool_scale"], sm["sgu_ln_g"], sm["sgu_ln_b"], sm["sgu_w_s"], sm["sgu_b_sT"],
      sm["conv_k"], sm["conv_dw_b"], sm["conv_ln_g"], sm["conv_ln_b"], sm["mix_post_g"],
      gw["po"], gw["so"], gw["co"], gw["wo"])


MIX_SMALL_GRADS = (("pool_w", (4, CHUNK, CHUNK)), ("pool_scale", (1, PW)), ("sgu_ln_g", (1, PW)), ("sgu_ln_b", (1, PW)),
                   ("sgu_w_s", (4, CHUNK, CHUNK)), ("sgu_b_sT", (CHUNK, 4)), ("conv_k", (CONV_HALO, PW)),
                   ("conv_dw_b", (1, PW)), ("conv_ln_g", (1, PW)), ("conv_ln_b", (1, PW)), ("mix_post_g", (1, D)))


def _mix_core_bwd(dout, z, h, l, sm, gw):
    T = h.shape[0]
    R = min(128, T)
    nt = T // R
    tile_of = lambda i: nt - 1 - i

    def body(*refs):
        do_ref, ins = refs[0], refs[1:21]
        (dz_ref, mg_ref, dob_ref, dy0_ref, dy1_ref, dy2_ref, pm_ref, sg_ref, cv_ref,
         dpw_ref, dps_ref, dlg_ref, dlb_ref, dws_ref, dbs_ref, dck_ref, dcb_ref, dcg_ref, dcbb_ref, dqg_ref) = refs[21:41]
        pbuf, xbuf, sbuf, qbuf, dybuf, dvbuf = refs[41:]
        (_z, _zp, _za, _zb, h_ref, pw_ref, ps_ref, lg_ref, lb_ref, ws_ref, bst_ref,
         ck_ref, cb_ref, cg_ref, cbb_ref, qg_ref, wpo_ref, wso_ref, wco_ref, wout_ref) = ins
        i = pl.program_id(0)
        small = (dpw_ref, dps_ref, dlg_ref, dlb_ref, dws_ref, dbs_ref, dck_ref, dcb_ref, dcg_ref, dcbb_ref, dqg_ref)

        @pl.when(i == 0)
        def _():
            for r in small:
                r[...] = jnp.zeros_like(r)
            qbuf[pl.ds(R, POOL_HALO), :] = jnp.zeros((POOL_HALO, PW), F32)
            dybuf[pl.ds(R, CONV_HALO), :] = jnp.zeros((CONV_HALO, PW), F32)

        fw = _MixFwd(R, tile_of(i), ins, (pbuf, xbuf, sbuf, pm_ref, sg_ref, cv_ref))
        mg_ref[...] = fw.merged
        oh, orr = _rms(fw.o)
        do, dq = _rms_bwd(oh, orr, qg_ref[...], do_ref[...])
        dqg_ref[...] += dq
        do = do.astype(BF)
        dob_ref[...] = do
        dm = _dot_nt(do, wout_ref[...])
        dys = []
        for j, dyj_ref in enumerate((dy0_ref, dy1_ref, dy2_ref)):
            g = fw.g[j]
            dz_ref[:, pl.ds(OFF_G + j * D, D)] = (dm * fw.y[j] * g * (1.0 - g)).astype(BF)
            dyj = (dm * g).astype(BF)
            dyj_ref[...] = dyj
            dys.append(dyj)
        dpm = _dot(dys[0], wpo_ref[...])
        dsg = _dot(dys[1], wso_ref[...])
        dcv = _dot(dys[2], wco_ref[...])
        for gi, w in enumerate(POOL_WINDOWS):
            cols = pl.ds(gi * CHUNK, CHUNK)
            dpm_g = dpm[:, gi * CHUNK:(gi + 1) * CHUNK]
            dps_ref[:, cols] += _colsum(dpm_g * fw.yg[gi])
            dyg = (dpm_g * ps_ref[:, cols]).astype(BF)
            dpw_ref[gi] += _dot_tn(fw.pooled[gi], dyg)
            dpooled = _dot_nt(dyg, pw_ref[gi].astype(BF))
            qbuf[pl.ds(0, R), cols] = dpooled / fw.cnt[gi]
            acc = -dpooled
            for j in range(w):
                acc = acc + qbuf[pl.ds(j, R), cols]
            dz_ref[:, cols] = acc.astype(BF)
        qbuf[pl.ds(R, POOL_HALO), :] = qbuf[pl.ds(0, POOL_HALO), :]
        ds = dsg * fw.u
        du = dsg * fw.s
        for ck in range(R // CHUNK):
            for hd in range(4):
                rows, cols = pl.ds(ck * CHUNK, CHUNK), pl.ds(hd * CHUNK, CHUNK)
                ds_f = ds[ck * CHUNK:(ck + 1) * CHUNK, hd * CHUNK:(hd + 1) * CHUNK]
                ds_blk = ds_f.astype(BF)
                v_blk = fw.vln[ck * CHUNK:(ck + 1) * CHUNK, hd * CHUNK:(hd + 1) * CHUNK]
                dbs_ref[:, pl.ds(hd, 1)] += jnp.sum(ds_f, axis=1, keepdims=True)
                dws_ref[hd] += jnp.where(fw.causal, _dot_nt(ds_blk, v_blk), 0.0)
                dvbuf[rows, cols] = _dot_tn(fw.wc[hd], ds_blk)
        dgv, dg, db = _ln_bwd(fw.vh, fw.vr, lg_ref[...], dvbuf[...])
        dlg_ref[...] += dg
        dlb_ref[...] += db
        dz_ref[:, pl.ds(OFF_V, PW)] = (dgv * _gelu_grad(_z[:, pl.ds(OFF_V, PW)])).astype(BF)
        dz_ref[:, pl.ds(OFF_U, PW)] = (du * _gelu_grad(_z[:, pl.ds(OFF_U, PW)])).astype(BF)
        dyl = dcv * (fw.sy * (1.0 + fw.yl * (1.0 - fw.sy)))
        dy, dg, db = _ln_bwd(fw.yh, fw.yr, cg_ref[...], dyl)
        dcg_ref[...] += dg
        dcbb_ref[...] += db
        dcb_ref[...] += _colsum(dy)
        dybuf[pl.ds(0, R), :] = dy
        dxg = jnp.zeros((R, PW), F32)
        for k in range(CONV_K):
            dck_ref[pl.ds(k, 1), :] += _colsum(dy * xbuf[pl.ds(CONV_HALO - (CONV_K - 1) + k, R), :])
            dxg = dxg + dybuf[pl.ds(CONV_K - 1 - k, R), :] * ck_ref[pl.ds(k, 1), :]
        dybuf[pl.ds(R, CONV_HALO), :] = dybuf[pl.ds(0, CONV_HALO), :]
        dz_ref[:, pl.ds(OFF_A, PW)] = (dxg * fw.sgb).astype(BF)
        dz_ref[:, pl.ds(OFF_B, PW)] = (dxg * fw.za * fw.sgb * (1.0 - fw.sgb)).astype(BF)

    tile = lambda n: pl.BlockSpec((R, n), lambda i: (tile_of(i), 0))
    small_specs = [pl.BlockSpec(shape, lambda i, nd=len(shape): (0,) * nd) for _, shape in MIX_SMALL_GRADS]
    outs = pl.pallas_call(
        body, name="mix_core_bwd", grid=(nt,),
        in_specs=[tile(D)] + _mix_specs(l, R, tile_of),
        out_specs=[tile(C), tile(D), tile(D), tile(D), tile(D), tile(D), tile(PW), tile(PW), tile(PW)] + small_specs,
        out_shape=[jax.ShapeDtypeStruct((T, C), BF)] + [jax.ShapeDtypeStruct((T, D), BF)] * 5
        + [jax.ShapeDtypeStruct((T, PW), BF)] * 3 + [jax.ShapeDtypeStruct(shape, F32) for _, shape in MIX_SMALL_GRADS],
        scratch_shapes=_mix_scratch(R) + [pltpu.VMEM((R + POOL_HALO, PW), F32), pltpu.VMEM((R + CONV_HALO, PW), F32),
                                          pltpu.VMEM((R, PW), F32)],
        compiler_params=_cparams(("arbitrary",)),
    )(dout, z, z, z, z, h, sm["pool_w"], sm["pool_scale"], sm["sgu_ln_g"], sm["sgu_ln_b"], sm["sgu_w_s"], sm["sgu_b_sT"],
      sm["conv_k"], sm["conv_dw_b"], sm["conv_ln_g"], sm["conv_ln_b"], sm["mix_post_g"],
      gw["po"], gw["so"], gw["co"], gw["wo"])
    return outs[:9], dict(zip([n for n, _ in MIX_SMALL_GRADS], outs[9:]))


def _ple_fwd(h, p, l, pre_g, post_g, wpr, wpg):
    T = h.shape[0]
    R = min(512, T)

    def body(h_ref, p_ref, pg_ref, qg_ref, wpr_ref, wpg_ref, out_ref, n_ref, pb_ref, gp_ref, e_ref):
        hh = h_ref[...]
        xh, _ = _rms(hh)
        n = (xh * pg_ref[...]).astype(BF)
        n_ref[...] = n
        pb = p_ref[...].astype(BF)
        pb_ref[:, pl.ds(0, PLE)] = pb
        pb_ref[:, pl.ds(PLE, PW - PLE)] = jnp.zeros((R, PW - PLE), BF)
        e = _dot_nt(pb, wpr_ref[:, pl.ds(0, PLE)])
        gp = _dot(n, wpg_ref[...])
        gp_ref[...] = gp
        e_ref[...] = e
        qh, _ = _rms(_sigmoid(gp) * e)
        out_ref[...] = hh + qh * qg_ref[...]

    tile = lambda n: pl.BlockSpec((R, n), lambda i: (i, 0))
    return pl.pallas_call(
        body, name="ple_fwd", grid=(T // R,),
        in_specs=[tile(D), pl.BlockSpec((None, None, R, PLE), lambda i: (l, 0, i, 0)), _row(l, D), _row(l, D),
                  _full((D, PW)), _full((D, D))],
        out_specs=[tile(D), tile(D), tile(PW), tile(D), tile(D)],
        out_shape=[jax.ShapeDtypeStruct((T, D), F32), jax.ShapeDtypeStruct((T, D), BF), jax.ShapeDtypeStruct((T, PW), BF),
                   jax.ShapeDtypeStruct((T, D), F32), jax.ShapeDtypeStruct((T, D), F32)],
        compiler_params=_cparams(("parallel",)),
    )(h, p, pre_g, post_g, wpr, wpg)


def _ple_bwd(dout, h, gp, e, l, pre_g, post_g, wpg):
    T = h.shape[0]
    R = min(512, T)

    def body(do_ref, h_ref, gp_ref, e_ref, pg_ref, qg_ref, wpg_ref, dh_ref, de_ref, dgp_ref, dpg_ref, dqg_ref):
        @pl.when(pl.program_id(0) == 0)
        def _():
            dpg_ref[...] = jnp.zeros_like(dpg_ref)
            dqg_ref[...] = jnp.zeros_like(dqg_ref)

        do = do_ref[...]
        g = _sigmoid(gp_ref[...])
        e = e_ref[...]
        qh, qr = _rms(g * e)
        dq, dqg = _rms_bwd(qh, qr, qg_ref[...], do)
        dqg_ref[...] += dqg
        de_ref[...] = (dq * g).astype(BF)
        dgp = (dq * e * g * (1.0 - g)).astype(BF)
        dgp_ref[...] = dgp
        dn = _dot_nt(dgp, wpg_ref[...])
        xh, xr = _rms(h_ref[...])
        dx, dp = _rms_bwd(xh, xr, pg_ref[...], dn)
        dpg_ref[...] += dp
        dh_ref[...] = do + dx

    tile = lambda n: pl.BlockSpec((R, n), lambda i: (i, 0))
    acc = pl.BlockSpec((1, D), lambda i: (0, 0))
    return pl.pallas_call(
        body, name="ple_bwd", grid=(T // R,),
        in_specs=[tile(D), tile(D), tile(D), tile(D), _row(l, D), _row(l, D), _full((D, D))],
        out_specs=[tile(D), tile(D), tile(D), acc, acc],
        out_shape=[jax.ShapeDtypeStruct((T, D), F32), jax.ShapeDtypeStruct((T, D), BF), jax.ShapeDtypeStruct((T, D), BF),
                   jax.ShapeDtypeStruct((1, D), F32), jax.ShapeDtypeStruct((1, D), F32)],
        compiler_params=_cparams(("arbitrary",)),
    )(dout, h, gp, e, pre_g, post_g, wpg)


def _loss_head(y, target):
    T = y.shape[0]
    R = min(512, T)

    def body(y_ref, t_ref, loss_ref, dy_ref):
        @pl.when(pl.program_id(0) == 0)
        def _():
            loss_ref[...] = jnp.zeros_like(loss_ref)

        err = y_ref[...] - t_ref[0]
        dy_ref[...] = err * (1.0 / D)
        loss_ref[...] += 0.5 * jnp.sum(_mean(err * err), axis=0, keepdims=True)

    tile = pl.BlockSpec((R, D), lambda i: (i, 0))
    return pl.pallas_call(
        body, name="loss_head", grid=(T // R,),
        in_specs=[tile, pl.BlockSpec((1, R, D), lambda i: (0, i, 0))],
        out_specs=[pl.BlockSpec((1, 1), lambda i: (0, 0)), tile],
        out_shape=[jax.ShapeDtypeStruct((1, 1), F32), jax.ShapeDtypeStruct((T, D), F32)],
        compiler_params=_cparams(("arbitrary",)),
    )(y, target)


def _wgrad_f(slab, a, b, l, name, a_col0=0):
    off, r = LAYOUT[name][0], LAYOUT[name][1]
    per = FC // r
    nblk = NDEV // per
    a0 = a_col0 // FC

    T = a.shape[0]
    n = b.shape[1]
    kt = min(1024, T)

    def body(a_ref, b_ref, slab_ref, out_ref, acc_ref):
        k = pl.program_id(1)

        @pl.when(k == 0)
        def _():
            acc_ref[...] = jnp.zeros_like(acc_ref)

        acc_ref[...] += _dot_tn(a_ref[...], b_ref[...])

        @pl.when(k == pl.num_programs(1) - 1)
        def _():
            out_ref[...] = acc_ref[...].reshape(per, r, n).astype(out_ref.dtype)

    return pl.pallas_call(
        body, name="wgrad_" + name, grid=(nblk, T // kt),
        in_specs=[pl.BlockSpec((kt, FC), lambda i, k: (k, i + a0)), pl.BlockSpec((kt, n), lambda i, k: (k, 0)),
                  pl.BlockSpec(memory_space=pl.ANY)],
        out_specs=pl.BlockSpec((per, None, r, n), lambda i, k: (i, l, off // r, 0)),
        out_shape=jax.ShapeDtypeStruct(slab.shape, slab.dtype),
        scratch_shapes=[pltpu.VMEM((FC, n), F32)],
        input_output_aliases={2: 0},
        compiler_params=_cparams(("parallel", "arbitrary")),
    )(a, b, slab)


def _wgrad_d(slab, a, b, l, name):
    off, r, col0, width = LAYOUT[name][:4]
    T = a.shape[0]
    n = b.shape[1]
    kt = min(1024, T)

    def body(a_ref, b_ref, slab_ref, out_ref, acc_ref):
        k = pl.program_id(0)

        @pl.when(k == 0)
        def _():
            acc_ref[...] = jnp.zeros_like(acc_ref)

        acc_ref[...] += _dot_tn(a_ref[...], b_ref[...])

        @pl.when(k == pl.num_programs(0) - 1)
        def _():
            out_ref[...] = acc_ref[...].reshape(NDEV, r, n).astype(out_ref.dtype)

    return pl.pallas_call(
        body, name="wgrad_" + name, grid=(T // kt,),
        in_specs=[pl.BlockSpec((kt, D), lambda k: (k, 0)), pl.BlockSpec((kt, n), lambda k: (k, 0)),
                  pl.BlockSpec(memory_space=pl.ANY)],
        out_specs=pl.BlockSpec((NDEV, None, r, n), lambda k: (0, l, off // r, col0 // n)),
        out_shape=jax.ShapeDtypeStruct(slab.shape, slab.dtype),
        scratch_shapes=[pltpu.VMEM((D, n), F32)],
        input_output_aliases={2: 0},
        compiler_params=_cparams(("arbitrary",)),
    )(a, b, slab)


def _adamw(w, g, m, v):
    m = ADAM_B1 * m + (1.0 - ADAM_B1) * g
    v = ADAM_B2 * v + (1.0 - ADAM_B2) * (g * g)
    m_hat = m / (1.0 - ADAM_B1 ** ADAM_STEP)
    v_hat = v / (1.0 - ADAM_B2 ** ADAM_STEP)
    delta = -ADAM_LR * (m_hat / (jnp.sqrt(v_hat) + ADAM_EPS) + ADAM_WD * w)
    return delta, m, v


def _adam_big(name, me, slab, recv, w, m, v):
    off, rows, col0, width, tr = LAYOUT[name]
    L = w.shape[0]
    wshape = w.shape[1:]
    nt = 4 if name == "w_in" else 1
    if tr:
        pblk, wblk = (rows, width // nt), (wshape[0] // nt, wshape[1])
        pmap = lambda t: (off // rows, col0 // (width // nt) + t)
    else:
        pblk, wblk = (rows // nt, width), (wshape[0] // nt, wshape[1])
        pmap = lambda t: (off // (rows // nt) + t, 0)

    def body(me_ref, s_ref, r_ref, w_ref, m_ref, v_ref, g_out, d_out, m_out, v_out):
        g = s_ref[...].astype(F32)
        for k in range(NDEV - 1):
            g = g + r_ref[k].astype(F32)
        if tr:
            g = g.T
        d, mm, vv = _adamw(w_ref[...], g, m_ref[...], v_ref[...])
        g_out[...] = g
        d_out[...] = d
        m_out[...] = mm
        v_out[...] = vv

    wspec = pl.BlockSpec((None,) + wblk, lambda l, t, me: (l, t, 0))
    grid_spec = pltpu.PrefetchScalarGridSpec(
        num_scalar_prefetch=1, grid=(L, nt),
        in_specs=[pl.BlockSpec((None, None) + pblk, lambda l, t, me: (me[0], l) + pmap(t)),
                  pl.BlockSpec((None, NDEV - 1) + pblk, lambda l, t, me: (l, 0) + pmap(t)), wspec, wspec, wspec],
        out_specs=[wspec] * 4)
    return pl.pallas_call(
        body, name="adam_" + name, grid_spec=grid_spec, out_shape=[jax.ShapeDtypeStruct(w.shape, F32)] * 4,
        compiler_params=_cparams(("parallel", "parallel")),
    )(me, slab, recv, w, m, v)


def _adam_small(gall, w, m, v):
    rows = w.shape[0]
    tr = 32

    def body(g_ref, w_ref, m_ref, v_ref, g_out, d_out, m_out, v_out):
        g = g_ref[0]
        for k in range(1, NDEV):
            g = g + g_ref[k]
        d, mm, vv = _adamw(w_ref[...], g, m_ref[...], v_ref[...])
        g_out[...] = g
        d_out[...] = d
        m_out[...] = mm
        v_out[...] = vv

    spec = pl.BlockSpec((tr, D), lambda i: (i, 0))
    return pl.pallas_call(
        body, name="adam_small", grid=(rows // tr,),
        in_specs=[pl.BlockSpec((NDEV, tr, D), lambda i: (0, i, 0)), spec, spec, spec],
        out_specs=[spec] * 4, out_shape=[jax.ShapeDtypeStruct(w.shape, F32)] * 4,
        compiler_params=_cparams(("parallel",)),
    )(gall, w, m, v)


def _adam_plain(g, w, m, v):
    def body(g_ref, w_ref, m_ref, v_ref, d_out, m_out, v_out):
        d, mm, vv = _adamw(w_ref[...], g_ref[...], m_ref[...], v_ref[...])
        d_out[...] = d
        m_out[...] = mm
        v_out[...] = vv

    vm = pl.BlockSpec(memory_space=pltpu.VMEM)
    return pl.pallas_call(
        body, name="adam_conv_k", in_specs=[vm] * 4, out_specs=[vm] * 3,
        out_shape=[jax.ShapeDtypeStruct(w.shape, F32)] * 3,
    )(g, w, m, v)


def _pad_cols(a, n):
    return jnp.pad(a, [(0, 0)] * (a.ndim - 1) + [(0, n - a.shape[-1])])


def _pack_small(d, conv_k_full):
    L = d["pool_w"].shape[0]
    gains = jnp.stack([d[n].reshape(L, D) for n in GAINS], axis=1)
    halves = [_pad_cols(d[n].reshape(L, PW), D) for n in HALVES] + [jnp.zeros((L, D), F32)]
    halves = jnp.stack(halves, axis=1)
    ck = jnp.zeros((L, 16, D), F32) if conv_k_full is None else conv_k_full.reshape(L, 16, D)
    out = jnp.concatenate([gains, halves, d["pool_w"].reshape(L, 64, D), d["sgu_w_s"].reshape(L, 64, D), ck], axis=1)
    return out.reshape(L * SMALL_ROWS, D)


def _unpack_small(a, like):
    L = a.shape[0] // SMALL_ROWS
    a = a.reshape(L, SMALL_ROWS, D)
    out = {}
    for i, n in enumerate(GAINS):
        out[n] = a[:, i, :].reshape(like[n].shape)
    for i, n in enumerate(HALVES):
        out[n] = a[:, 8 + i, :PW].reshape(like[n].shape)
    out["pool_w"] = a[:, 16:80, :].reshape(like["pool_w"].shape)
    out["sgu_w_s"] = a[:, 80:144, :].reshape(like["sgu_w_s"].shape)
    out["conv_k_full"] = a[:, 144:160, :].reshape(L, CONV_HALO, PW)
    return out


WEIGHTS = ("ffn1_pre_g", "ffn1_w_gate", "ffn1_w_up", "ffn1_w_down", "ffn1_post_g", "mix_pre_g", "w_in", "pool_w", "pool_scale",
           "w_pool_out", "sgu_ln_g", "sgu_ln_b", "sgu_w_s", "sgu_b_s", "w_sgu_out", "conv_dw_k", "conv_dw_b", "conv_ln_g",
           "conv_ln_b", "w_conv_out", "w_out", "mix_post_g", "ffn2_pre_g", "ffn2_w_gate", "ffn2_w_up", "ffn2_w_down",
           "ffn2_post_g", "ple_w_proj", "ple_pre_g", "ple_w_gate", "ple_post_g")
SMALL = GAINS + HALVES + ("pool_w", "sgu_w_s")


class _Comm:
    def __init__(self, w):
        self.w = w

    def gather_start(self, l, after):
        packed = _prep(self.w, l)
        lands = [lax.empty(GATHERED[n], BF) for n in GNAMES]
        send_sems, recv_sems, packed, lands, token = _gather_start(l, packed, lands, packed if after is None else after)
        return (send_sems, recv_sems, packed, lands), token[0, 0]

    def gather_finish(self, l, state, after):
        send_sems, recv_sems, packed, lands = state
        packed, lands = _gather_wait(l, send_sems, recv_sems, packed, lands, packed if after is None else after)
        return _gather_forward(l, packed, lands)

    def scatter_start(self, l, slab, recv):
        send_sems, recv_sems, slab, recv = _scatter_start(l, slab, recv)
        return (send_sems, recv_sems), slab, recv

    def scatter_finish(self, l, state, slab, recv):
        return _scatter_wait(l, state[0], state[1], slab, recv)


def _fwd_bwd(x, p, target, w, conv_k, comm):
    L = w["w_in"].shape[0]
    T = x.shape[1]
    row = lambda a: a.reshape(L, 1, a.shape[-1])
    sm = {n: row(w[n]) for n in GAINS + ("pool_scale", "sgu_ln_g", "sgu_ln_b", "conv_dw_b", "conv_ln_g", "conv_ln_b")}
    sm.update(pool_w=w["pool_w"], sgu_w_s=w["sgu_w_s"], sgu_b_sT=w["sgu_b_s"].transpose(0, 2, 1), conv_k=conv_k)

    h = x[0]
    saved, gws = [], [None] * L
    state, _ = comm.gather_start(0, None)
    gws[0] = comm.gather_finish(0, state, None)
    for l in range(L):
        gw = gws[l]
        pre1 = sm["ffn1_pre_g"]
        if l + 1 < L:
            state, token = comm.gather_start(l + 1, gw["pg"])
            pre1 = pre1 + token
        s = {"h0": h}
        h, s["n1"], s["ab1"], s["f1"] = _ffn_fwd(h, l, pre1, sm["ffn1_post_g"], gw["gu1"], gw["d1"])
        s["h1"] = h
        s["nm"], s["z"] = _mix_in_fwd(h, l, sm["mix_pre_g"], gw["win"])
        h = _mix_core_fwd(s["z"], h, l, sm, gw)
        s["h2"] = h
        h, s["n2"], s["ab2"], s["f2"] = _ffn_fwd(h, l, sm["ffn2_pre_g"], sm["ffn2_post_g"], gw["gu2"], gw["d2"])
        s["h3"] = h
        h, s["np"], s["pb"], s["gp"], s["e"] = _ple_fwd(h, p, l, sm["ple_pre_g"], sm["ple_post_g"], gw["pr"], gw["pg"])
        saved.append(s)
        if l + 1 < L:
            gws[l + 1] = comm.gather_finish(l + 1, state, h)

    loss_part, dh = _loss_head(h, target)

    slab = lax.empty((NDEV, L, ROWS, D), BF)
    recv = lax.empty((L, NDEV - 1, ROWS, D), BF)
    sg = {n: [None] * L for n in SMALL + ("conv_k", "sgu_b_sT")}
    pending = None
    for l in reversed(range(L)):
        s, gw = saved[l], gws[l]
        dh, de, dgp, sg["ple_pre_g"][l], sg["ple_post_g"][l] = _ple_bwd(
            dh, s["h3"], s["gp"], s["e"], l, sm["ple_pre_g"], sm["ple_post_g"], gw["pg"])
        slab = _wgrad_d(slab, s["np"], dgp, l, "ple_w_gate")
        slab = _wgrad_d(slab, de, s["pb"], l, "ple_w_proj")

        dh, dab, ss, df, sg["ffn2_pre_g"][l], sg["ffn2_post_g"][l] = _ffn_bwd(
            dh, s["h2"], s["ab2"], s["f2"], l, sm["ffn2_pre_g"], sm["ffn2_post_g"], gw["gu2"], gw["d2"])
        slab = _wgrad_f(slab, dab, s["n2"], l, "ffn2_w_gate")
        slab = _wgrad_f(slab, dab, s["n2"], l, "ffn2_w_up", a_col0=F)
        slab = _wgrad_f(slab, ss, df, l, "ffn2_w_down")

        (dz, mg, dob, dy0, dy1, dy2, pm, sgv, cv), g_mix = _mix_core_bwd(dh, s["z"], s["h1"], l, sm, gw)
        for n in g_mix:
            sg[n][l] = g_mix[n]
        dh, sg["mix_pre_g"][l] = _mix_in_bwd(dh, dz, s["h1"], l, sm["mix_pre_g"], gw["win"])
        slab = _wgrad_f(slab, dz, s["nm"], l, "w_in")
        slab = _wgrad_d(slab, mg, dob, l, "w_out")
        slab = _wgrad_d(slab, dy0, pm, l, "w_pool_out")
        slab = _wgrad_d(slab, dy1, sgv, l, "w_sgu_out")
        slab = _wgrad_d(slab, dy2, cv, l, "w_conv_out")

        dh, dab, ss, df, sg["ffn1_pre_g"][l], sg["ffn1_post_g"][l] = _ffn_bwd(
            dh, s["h0"], s["ab1"], s["f1"], l, sm["ffn1_pre_g"], sm["ffn1_post_g"], gw["gu1"], gw["d1"])
        slab = _wgrad_f(slab, dab, s["n1"], l, "ffn1_w_gate")
        slab = _wgrad_f(slab, dab, s["n1"], l, "ffn1_w_up", a_col0=F)
        slab = _wgrad_f(slab, ss, df, l, "ffn1_w_down")
        if pending is not None:
            slab, recv = comm.scatter_finish(l + 1, pending, slab, recv)
        pending, slab, recv = comm.scatter_start(l, slab, recv)
    slab, recv = comm.scatter_finish(0, pending, slab, recv)
    return loss_part, dh.reshape(1, T, D), slab, recv, sg


def _step(x, p, target, w, m, v):
    L = w["w_in"].shape[0]
    ix, iy, ic = lax.axis_index("x"), lax.axis_index("y"), lax.axis_index("c")
    me = 4 * ix + 2 * iy + ic

    ck_local = jnp.pad(w["conv_dw_k"].reshape(L, CONV_K, 64), ((0, 0), (0, 1), (0, 0))).reshape(L * 2, D)
    ck_all = _gather_rows("allgather_conv_k", _pad_rows8(ck_local))[:, :L * 2]
    conv_k = ck_all.reshape(NDEV, L, CONV_HALO, 64).transpose(1, 2, 0, 3).reshape(L, CONV_HALO, PW)

    loss_part, grad_x, slab, recv, sg = _fwd_bwd(x, p, target, w, conv_k, _Comm(w))
    loss = lax.psum(loss_part[0, 0], ("x", "y", "c"))

    me_arr = me.astype(jnp.int32).reshape(1)
    res = {n: _adam_big(n, me_arr, slab, recv, w[n], m[n], v[n]) for n in BIG}

    sgrads = {n: jnp.stack(sg[n]) for n in SMALL if n != "sgu_b_s"}
    sgrads["sgu_b_s"] = jnp.stack(sg["sgu_b_sT"]).transpose(0, 2, 1)
    gall = _gather_rows("allgather_small_grads", _pack_small(sgrads, jnp.stack(sg["conv_k"])))
    outs = _adam_small(gall, _pack_small(w, None), _pack_small(m, None), _pack_small(v, None))
    unpacked = [_unpack_small(o, w) for o in outs]
    for n in SMALL:
        res[n] = tuple(u[n] for u in unpacked)

    gk = lax.dynamic_slice_in_dim(unpacked[0]["conv_k_full"][:, :CONV_K, :], me * 64, 64, axis=2)
    shp = w["conv_dw_k"].shape
    flat = lambda a: a.reshape(L * CONV_K, 64)
    dk, mk, vk = _adam_plain(flat(gk), flat(w["conv_dw_k"]), flat(m["conv_dw_k"]), flat(v["conv_dw_k"]))
    res["conv_dw_k"] = (gk.reshape(shp), dk.reshape(shp), mk.reshape(shp), vk.reshape(shp))

    return (loss, grad_x, *[res[n][0] for n in WEIGHTS], *[res[n][1] for n in WEIGHTS],
            *[res[n][2] for n in WEIGHTS], *[res[n][3] for n in WEIGHTS])


def _pad_rows8(a):
    return jnp.pad(a, ((0, (-a.shape[0]) % 8), (0, 0)))


def kernel(x, p, ffn1_pre_g, ffn1_w_gate, ffn1_w_up, ffn1_w_down, ffn1_post_g, mix_pre_g, w_in, pool_w, pool_scale, w_pool_out, sgu_ln_g, sgu_ln_b, sgu_w_s, sgu_b_s, w_sgu_out, conv_dw_k, conv_dw_b, conv_ln_g, conv_ln_b, w_conv_out, w_out, mix_post_g, ffn2_pre_g, ffn2_w_gate, ffn2_w_up, ffn2_w_down, ffn2_post_g, ple_w_proj, ple_pre_g, ple_w_gate, ple_post_g, loss_target, m_ffn1_pre_g, m_ffn1_w_gate, m_ffn1_w_up, m_ffn1_w_down, m_ffn1_post_g, m_mix_pre_g, m_w_in, m_pool_w, m_pool_scale, m_w_pool_out, m_sgu_ln_g, m_sgu_ln_b, m_sgu_w_s, m_sgu_b_s, m_w_sgu_out, m_conv_dw_k, m_conv_dw_b, m_conv_ln_g, m_conv_ln_b, m_w_conv_out, m_w_out, m_mix_post_g, m_ffn2_pre_g, m_ffn2_w_gate, m_ffn2_w_up, m_ffn2_w_down, m_ffn2_post_g, m_ple_w_proj, m_ple_pre_g, m_ple_w_gate, m_ple_post_g, v_ffn1_pre_g, v_ffn1_w_gate, v_ffn1_w_up, v_ffn1_w_down, v_ffn1_post_g, v_mix_pre_g, v_w_in, v_pool_w, v_pool_scale, v_w_pool_out, v_sgu_ln_g, v_sgu_ln_b, v_sgu_w_s, v_sgu_b_s, v_w_sgu_out, v_conv_dw_k, v_conv_dw_b, v_conv_ln_g, v_conv_ln_b, v_w_conv_out, v_w_out, v_mix_post_g, v_ffn2_pre_g, v_ffn2_w_gate, v_ffn2_w_up, v_ffn2_w_down, v_ffn2_post_g, v_ple_w_proj, v_ple_pre_g, v_ple_w_gate, v_ple_post_g):
    args = dict(locals())
    w = {n: args[n] for n in WEIGHTS}
    m = {n: args["m_" + n] for n in WEIGHTS}
    v = {n: args["v_" + n] for n in WEIGHTS}
    return _step(x, p, loss_target, w, m, v)
```

```python
import functools

import jax
import jax.numpy as jnp
from jax import lax
from jax.experimental import pallas as pl
from jax.experimental.pallas import tpu as pltpu

D = 1024
F = 2816
C = 5632
PW = 512
PLE = 256
NDEV = 8
CHUNK = 128
POOL_WINDOWS = (2, 4, 8, 16)
CONV_K = 31
POOL_HALO = 16
CONV_HALO = 32
EPS = 1e-6
OFF_U, OFF_V, OFF_A, OFF_B, OFF_G = 512, 1024, 1536, 2048, 2560

ADAM_LR, ADAM_B1, ADAM_B2, ADAM_EPS, ADAM_WD, ADAM_STEP = 0.001, 0.9, 0.999, 1e-08, 0.01, 10

BF = jnp.bfloat16
F32 = jnp.float32
VMEM_LIMIT = 56 * 1024 * 1024
MESH = pl.DeviceIdType.MESH
INV_SQRT2 = 0.7071067811865476
INV_SQRT_2PI = 0.3989422804014327

ROWS = 3328
LAYOUT = {
    "w_in": (0, 704, 0, 1024, True),
    "ffn1_w_gate": (704, 352, 0, 1024, True),
    "ffn1_w_up": (1056, 352, 0, 1024, True),
    "ffn1_w_down": (1408, 352, 0, 1024, False),
    "ffn2_w_gate": (1760, 352, 0, 1024, True),
    "ffn2_w_up": (2112, 352, 0, 1024, True),
    "ffn2_w_down": (2464, 352, 0, 1024, False),
    "w_pool_out": (2816, 128, 0, 512, True),
    "w_sgu_out": (2816, 128, 512, 512, True),
    "w_conv_out": (2944, 128, 0, 512, True),
    "ple_w_proj": (2944, 128, 512, 256, True),
    "w_out": (3072, 128, 0, 1024, False),
    "ple_w_gate": (3200, 128, 0, 1024, False),
}
BIG = tuple(LAYOUT)
GATHERED = {
    "win": (C, D), "gu1": (2 * F, D), "d1": (F, D), "gu2": (2 * F, D), "d2": (F, D),
    "po": (D, PW), "so": (D, PW), "co": (D, PW), "pr": (D, PW), "wo": (D, D), "pg": (D, D),
}
PIECES = (
    ("win", 0, 0, 704, 0, 1024), ("gu1", 0, 704, 352, 0, 1024), ("gu1", F, 1056, 352, 0, 1024),
    ("d1", 0, 1408, 352, 0, 1024), ("gu2", 0, 1760, 352, 0, 1024), ("gu2", F, 2112, 352, 0, 1024),
    ("d2", 0, 2464, 352, 0, 1024), ("po", 0, 2816, 128, 0, 512), ("so", 0, 2816, 128, 512, 512),
    ("co", 0, 2944, 128, 0, 512), ("pr", 0, 2944, 128, 512, 512), ("wo", 0, 3072, 128, 0, 1024),
    ("pg", 0, 3200, 128, 0, 1024),
)
GAINS = ("ffn1_pre_g", "ffn1_post_g", "mix_pre_g", "mix_post_g", "ffn2_pre_g", "ffn2_post_g", "ple_pre_g", "ple_post_g")
HALVES = ("pool_scale", "sgu_ln_g", "sgu_ln_b", "sgu_b_s", "conv_dw_b", "conv_ln_g", "conv_ln_b")
SMALL_ROWS = 160


def _cparams(sem=None, **kw):
    if sem is not None:
        kw["dimension_semantics"] = sem
    return pltpu.CompilerParams(vmem_limit_bytes=VMEM_LIMIT, **kw)


def _whole(l, shape):
    nd = len(shape)
    return pl.BlockSpec((None,) + tuple(shape), lambda *_: (l,) + (0,) * nd, pipeline_mode=pl.Buffered(1))


def _full(shape):
    nd = len(shape)
    return pl.BlockSpec(tuple(shape), lambda *_: (0,) * nd, pipeline_mode=pl.Buffered(1))


def _row(l, n):
    return pl.BlockSpec((None, 1, n), lambda *_: (l, 0, 0))


def _dot(a, b):
    return jnp.dot(a, b, preferred_element_type=F32)


def _dot_nt(a, b):
    return lax.dot_general(a, b, (((1,), (1,)), ((), ())), preferred_element_type=F32)


def _dot_tn(a, b):
    return lax.dot_general(a, b, (((0,), (0,)), ((), ())), preferred_element_type=F32)


def _mean(x):
    return jnp.mean(x, axis=-1, keepdims=True)


def _colsum(x):
    return jnp.sum(x, axis=0, keepdims=True)


def _rms(x):
    r = lax.rsqrt(_mean(x * x) + EPS)
    return x * r, r


def _rms_bwd(xh, r, g, dy):
    dxh = dy * g
    return r * (dxh - xh * _mean(dxh * xh)), _colsum(dy * xh)


def _ln(x):
    xc = x - _mean(x)
    r = lax.rsqrt(_mean(xc * xc) + EPS)
    return xc * r, r


def _ln_bwd(xh, r, g, dy):
    dxh = dy * g
    return r * (dxh - _mean(dxh) - xh * _mean(dxh * xh)), _colsum(dy * xh), _colsum(dy)


def _sigmoid(x):
    return jax.nn.sigmoid(x)


def _gelu(x):
    return 0.5 * x * (1.0 + lax.erf(x * INV_SQRT2))


def _gelu_grad(x):
    return 0.5 * (1.0 + lax.erf(x * INV_SQRT2)) + x * jnp.exp(-0.5 * x * x) * INV_SQRT_2PI


def _prep(w, l, me, lands):
    nb, npc = len(BIG), len(PIECES)

    def body(me_ref, *refs):
        ins, out = dict(zip(BIG, refs[:nb])), refs[nb + NG]
        land_refs, sems = refs[nb + NG + 1:nb + 2 * NG + 1], refs[-1]
        for name, (off, rows, col0, width, tr) in LAYOUT.items():
            v = ins[name][...]
            if tr:
                v = v.T
            out[pl.ds(off, rows), pl.ds(col0, width)] = v.astype(BF)
        out[pl.ds(2944, 128), pl.ds(768, 256)] = jnp.zeros((128, 256), BF)
        mine = [pltpu.make_async_copy(src, dst, sems.at[i]) for i, (src, dst) in enumerate(_piece_refs(out, land_refs, me_ref[0]))]
        for cp in mine:
            cp.start()
        for cp in mine:
            cp.wait()

    grid_spec = pltpu.PrefetchScalarGridSpec(
        num_scalar_prefetch=1, grid=(1,),
        in_specs=[pl.BlockSpec((None,) + w[n].shape[1:], lambda i, me: (l, 0, 0)) for n in BIG] + [ANY] * NG,
        out_specs=[pl.BlockSpec((ROWS, D), lambda i, me: (0, 0))] + [ANY] * NG,
        scratch_shapes=[pltpu.SemaphoreType.DMA((npc,))])
    outs = pl.pallas_call(
        body, name="prep", grid_spec=grid_spec,
        out_shape=[jax.ShapeDtypeStruct((ROWS, D), BF)] + [jax.ShapeDtypeStruct(a.shape, a.dtype) for a in lands],
        input_output_aliases={1 + nb + i: 1 + i for i in range(NG)},
        compiler_params=_cparams(("arbitrary",)),
    )(me, *[w[n] for n in BIG], *lands)
    return outs[0], list(outs[1:])


def _place():
    x, y, c = lax.axis_index("x"), lax.axis_index("y"), lax.axis_index("c")
    chips = [(1 - x, y), (x, 1 - y), (1 - x, 1 - y)]
    return x, y, c, chips


def _allgather(name, src, dsts, pieces):
    npc = len(pieces)

    def body(src_ref, *rest):
        outs, (send_sems, recv_sems, local_sems) = rest[:len(dsts)], rest[len(dsts):]
        x, y, c, chips = _place()
        me, sibling = (x, y, c), (x, y, 1 - c)

        def shard_of(dev):
            return 4 * dev[0] + 2 * dev[1] + dev[2]

        def copies(k, block, to, from_src):
            res = []
            for di, dst_fn, src_sl in pieces:
                dst = outs[di].at[dst_fn(shard_of(block))]
                s = src_ref.at[src_sl] if from_src else dst
                res.append(pltpu.make_async_remote_copy(src_ref=s, dst_ref=dst, send_sem=send_sems.at[k],
                                                        recv_sem=recv_sems.at[k], device_id=to, device_id_type=MESH))
            return res

        def whole(k):
            return pltpu.make_async_remote_copy(src_ref=src_ref, dst_ref=src_ref, send_sem=send_sems.at[k],
                                                recv_sem=recv_sems.at[k], device_id=me, device_id_type=MESH)

        mine = [pltpu.make_async_copy(src_ref.at[src_sl], outs[di].at[dst_fn(shard_of(me))], local_sems.at[i])
                for i, (di, dst_fn, src_sl) in enumerate(pieces)]
        for cp in mine:
            cp.start()
        for cp in copies(0, me, sibling, True):
            cp.start()
        for j, chip in enumerate(chips):
            for cp in copies(1 + j, me, (*chip, c), True):
                cp.start()
        for j, chip in enumerate(chips):
            whole(1 + j).wait_recv()
            for cp in copies(4 + j, (*chip, c), sibling, False):
                cp.start()
        for k in (0, 4, 5, 6):
            whole(k).wait_recv()
        for k in range(7):
            whole(k).wait_send()
        for cp in mine:
            cp.wait()

    any_spec = pl.BlockSpec(memory_space=pl.ANY)
    return pl.pallas_call(
        body, name=name, in_specs=[any_spec], out_specs=[any_spec] * len(dsts), out_shape=dsts,
        scratch_shapes=[pltpu.SemaphoreType.DMA((7,)), pltpu.SemaphoreType.DMA((7,)), pltpu.SemaphoreType.DMA((npc,))],
    )(src)


HBM = pl.BlockSpec(memory_space=pltpu.HBM)
SEM = pl.BlockSpec(memory_space=pltpu.SEMAPHORE)
ANY = pl.BlockSpec(memory_space=pl.ANY)
EFFECT = pltpu.SideEffectType.DATAFLOW_SIDE_EFFECTING
GNAMES = tuple(GATHERED)
NG = len(GNAMES)


def _in_hbm(a):
    return pltpu.with_memory_space_constraint(a, pltpu.HBM)


def _piece_refs(packed_ref, land_refs, shard):
    out = []
    for gname, row0, poff, prow, pcol, width in PIECES:
        land = land_refs[GNAMES.index(gname)]
        out.append((packed_ref.at[pl.ds(poff, prow), pl.ds(pcol, width)], land.at[pl.ds(row0 + shard * prow, prow), :]))
    return out


def _gather_start(l, packed, lands, after):
    def body(packed_ref, *rest):
        land_refs, send_sems, recv_sems = rest[:NG], rest[NG + 1], rest[NG + 2]
        token = rest[-1]
        x, y, c, chips = _place()
        targets = [(x, y, 1 - c)] + [(*chip, c) for chip in chips]
        for k, to in enumerate(targets):
            for src, dst in _piece_refs(packed_ref, land_refs, 4 * x + 2 * y + c):
                pltpu.make_async_remote_copy(src_ref=src, dst_ref=dst, send_sem=send_sems.at[k], recv_sem=recv_sems.at[k],
                                             device_id=to, device_id_type=MESH).start()
        token[...] = jnp.zeros_like(token)

    hbm = lambda a: pltpu.HBM(a.shape, a.dtype)
    outs = pl.pallas_call(
        body, name=f"gather_start_{l}",
        out_shape=(pltpu.SemaphoreType.DMA((4,)), pltpu.SemaphoreType.DMA((4,)), hbm(packed), *[hbm(a) for a in lands],
                   jax.ShapeDtypeStruct((8, 128), F32)),
        in_specs=(HBM,) * (1 + NG) + (ANY,),
        out_specs=(SEM, SEM) + (HBM,) * (1 + NG) + (pl.BlockSpec(memory_space=pltpu.VMEM),),
        input_output_aliases={i: 2 + i for i in range(1 + NG)},
        compiler_params=pltpu.CompilerParams(has_side_effects=EFFECT),
    )(_in_hbm(packed), *[_in_hbm(a) for a in lands], after)
    return outs[0], outs[1], outs[2], list(outs[3:3 + NG]), outs[-1]


def _gather_wait(l, send_sems, recv_sems, packed, lands, after):
    def body(packed_ref, *rest):
        send_sems, recv_sems = rest[NG], rest[NG + 1]
        x, y, c, _ = _place()
        for k in range(4):
            cp = pltpu.make_async_remote_copy(src_ref=packed_ref, dst_ref=packed_ref, send_sem=send_sems.at[k],
                                              recv_sem=recv_sems.at[k], device_id=(x, y, c), device_id_type=MESH)
            cp.wait_send()
            cp.wait_recv()

    hbm = lambda a: pltpu.HBM(a.shape, a.dtype)
    outs = pl.pallas_call(
        body, name=f"gather_wait_{l}",
        out_shape=(hbm(packed), *[hbm(a) for a in lands]),
        in_specs=(HBM,) * (1 + NG) + (SEM, SEM, ANY), out_specs=(HBM,) * (1 + NG),
        input_output_aliases={i: i for i in range(1 + NG)},
        compiler_params=pltpu.CompilerParams(has_side_effects=EFFECT),
    )(packed, *lands, send_sems, recv_sems, after)
    return outs[0], list(outs[1:])


def _gather_forward(l, packed, lands):
    def body(packed_ref, *rest):
        land_refs, (send_sems, recv_sems) = rest[NG:2 * NG], rest[2 * NG:]
        x, y, c, chips = _place()
        for j, (cx, cy) in enumerate(chips):
            for _, rows in _piece_refs(packed_ref, land_refs, 4 * cx + 2 * cy + c):
                pltpu.make_async_remote_copy(src_ref=rows, dst_ref=rows, send_sem=send_sems.at[j], recv_sem=recv_sems.at[j],
                                             device_id=(x, y, 1 - c), device_id_type=MESH).start()
        for j in range(3):
            cp = pltpu.make_async_remote_copy(src_ref=packed_ref, dst_ref=packed_ref, send_sem=send_sems.at[j],
                                              recv_sem=recv_sems.at[j], device_id=(x, y, c), device_id_type=MESH)
            cp.wait_recv()
            cp.wait_send()

    outs = pl.pallas_call(
        body, name=f"gather_forward_{l}", in_specs=[ANY] * (1 + NG), out_specs=[ANY] * NG,
        out_shape=[jax.ShapeDtypeStruct(a.shape, a.dtype) for a in lands],
        input_output_aliases={1 + i: i for i in range(NG)},
        scratch_shapes=[pltpu.SemaphoreType.DMA((3,)), pltpu.SemaphoreType.DMA((3,))],
    )(packed, *lands)
    return dict(zip(GNAMES, outs))


RELATIONS = ((0, 0, 1), (1, 0, 0), (0, 1, 0), (1, 1, 0), (1, 0, 1), (0, 1, 1), (1, 1, 1))


def _peers():
    x, y, c = lax.axis_index("x"), lax.axis_index("y"), lax.axis_index("c")
    return [((1 - x) if fx else x, (1 - y) if fy else y, (1 - c) if fc else c) for fx, fy, fc in RELATIONS]


def _scatter_start(tag, l, slab, recv, parts):
    def body(slab_ref, recv_ref, send_sems, recv_sems, slab_out, recv_out, token):
        for k, to in enumerate(_peers()):
            for r0, nr in parts:
                pltpu.make_async_remote_copy(src_ref=slab_ref.at[4 * to[0] + 2 * to[1] + to[2], l, pl.ds(r0, nr)],
                                             dst_ref=recv_ref.at[l, k, pl.ds(r0, nr)],
                                             send_sem=send_sems.at[k], recv_sem=recv_sems.at[k],
                                             device_id=to, device_id_type=MESH).start()
        token[...] = jnp.zeros_like(token)

    return pl.pallas_call(
        body, name=f"scatter_start_{tag}",
        out_shape=(pltpu.SemaphoreType.DMA((7,)), pltpu.SemaphoreType.DMA((7,)), pltpu.HBM(slab.shape, slab.dtype),
                   pltpu.HBM(recv.shape, recv.dtype), jax.ShapeDtypeStruct((8, 128), F32)),
        in_specs=(HBM, HBM), out_specs=(SEM, SEM, HBM, HBM, pl.BlockSpec(memory_space=pltpu.VMEM)),
        input_output_aliases={0: 2, 1: 3},
        compiler_params=pltpu.CompilerParams(has_side_effects=EFFECT),
    )(_in_hbm(slab), _in_hbm(recv))


def _scatter_wait(tag, l, send_sems, recv_sems, slab, recv, parts):
    total = sum(nr for _, nr in parts)

    def body(slab_ref, recv_ref, send_sems, recv_sems, slab_out, recv_out):
        for k, to in enumerate(_peers()):
            cp = pltpu.make_async_remote_copy(src_ref=slab_ref.at[0, l, pl.ds(0, total)], dst_ref=recv_ref.at[l, k, pl.ds(0, total)],
                                              send_sem=send_sems.at[k], recv_sem=recv_sems.at[k], device_id=to, device_id_type=MESH)
            cp.wait_send()
            cp.wait_recv()

    return pl.pallas_call(
        body, name=f"scatter_wait_{tag}",
        out_shape=(pltpu.HBM(slab.shape, slab.dtype), pltpu.HBM(recv.shape, recv.dtype)),
        in_specs=(HBM, HBM, SEM, SEM), out_specs=(HBM, HBM), input_output_aliases={0: 0, 1: 1},
        compiler_params=pltpu.CompilerParams(has_side_effects=EFFECT),
    )(slab, recv, send_sems, recv_sems)


def _gather_rows(name, src):
    dst = jax.ShapeDtypeStruct((NDEV,) + src.shape, src.dtype)
    full = (slice(None), slice(None), slice(None))
    pieces = [(0, lambda shard: (pl.ds(shard, 1), slice(None), slice(None)), full)]
    return _allgather(name, src.reshape((1,) + src.shape), [dst], pieces)[0]


FC = 1408


def _ffn_fwd(h, l, pre_g, post_g, wgu, wd):
    T = h.shape[0]
    R = min(512, T)

    def body(h_ref, pg_ref, qg_ref, wgu_ref, wd_ref, out_ref, n_ref, ab_ref, f_ref):
        hh = h_ref[...]
        xh, _ = _rms(hh)
        n = (xh * pg_ref[...]).astype(BF)
        n_ref[...] = n
        f = jnp.zeros((R, D), F32)
        for ci in range(F // FC):
            a = _dot_nt(n, wgu_ref[pl.ds(ci * FC, FC), :])
            b = _dot_nt(n, wgu_ref[pl.ds(F + ci * FC, FC), :])
            ab_ref[:, pl.ds(ci * FC, FC)] = a.astype(BF)
            ab_ref[:, pl.ds(F + ci * FC, FC)] = b.astype(BF)
            s = (a * _sigmoid(a) * b).astype(BF)
            f = f + _dot(s, wd_ref[pl.ds(ci * FC, FC), :])
        f_ref[...] = f
        fh, _ = _rms(f)
        out_ref[...] = hh + 0.5 * (fh * qg_ref[...])

    tile = lambda n: pl.BlockSpec((R, n), lambda i: (i, 0))
    return pl.pallas_call(
        body, name="ffn_fwd", grid=(T // R,),
        in_specs=[tile(D), _row(l, D), _row(l, D), _full((2 * F, D)), _full((F, D))],
        out_specs=[tile(D), tile(D), tile(2 * F), tile(D)],
        out_shape=[jax.ShapeDtypeStruct((T, D), F32), jax.ShapeDtypeStruct((T, D), BF),
                   jax.ShapeDtypeStruct((T, 2 * F), BF), jax.ShapeDtypeStruct((T, D), F32)],
        compiler_params=_cparams(("parallel",)),
    )(h, pre_g, post_g, wgu, wd)


def _ffn_bwd(dout, h, ab, f, l, pre_g, post_g, wgu, wd):
    T = h.shape[0]
    R = min(256, T)

    def body(do_ref, h_ref, ab_ref, f_ref, pg_ref, qg_ref, wgu_ref, wd_ref,
             dh_ref, dab_ref, s_ref, df_ref, dpg_ref, dqg_ref):
        i = pl.program_id(0)

        @pl.when(i == 0)
        def _():
            dpg_ref[...] = jnp.zeros_like(dpg_ref)
            dqg_ref[...] = jnp.zeros_like(dqg_ref)

        do = do_ref[...]
        fh, fr = _rms(f_ref[...])
        df, dq = _rms_bwd(fh, fr, qg_ref[...], 0.5 * do)
        dqg_ref[...] += dq
        df = df.astype(BF)
        df_ref[...] = df
        dn = jnp.zeros((R, D), F32)
        for ci in range(F // FC):
            ga, gb = pl.ds(ci * FC, FC), pl.ds(F + ci * FC, FC)
            ds = _dot_nt(df, wd_ref[ga, :])
            a = ab_ref[:, ga].astype(F32)
            b = ab_ref[:, gb].astype(F32)
            sg = _sigmoid(a)
            sil = a * sg
            s_ref[:, ga] = (sil * b).astype(BF)
            da = (ds * b * (sg * (1.0 + a * (1.0 - sg)))).astype(BF)
            db = (ds * sil).astype(BF)
            dab_ref[:, ga] = da
            dab_ref[:, gb] = db
            dn = dn + _dot(da, wgu_ref[ga, :]) + _dot(db, wgu_ref[gb, :])
        xh, xr = _rms(h_ref[...])
        dx, dp = _rms_bwd(xh, xr, pg_ref[...], dn)
        dpg_ref[...] += dp
        dh_ref[...] = do + dx

    tile = lambda n: pl.BlockSpec((R, n), lambda i: (i, 0))
    acc = pl.BlockSpec((1, D), lambda i: (0, 0))
    return pl.pallas_call(
        body, name="ffn_bwd", grid=(T // R,),
        in_specs=[tile(D), tile(D), tile(2 * F), tile(D), _row(l, D), _row(l, D), _full((2 * F, D)), _full((F, D))],
        out_specs=[tile(D), tile(2 * F), tile(F), tile(D), acc, acc],
        out_shape=[jax.ShapeDtypeStruct((T, D), F32), jax.ShapeDtypeStruct((T, 2 * F), BF),
                   jax.ShapeDtypeStruct((T, F), BF), jax.ShapeDtypeStruct((T, D), BF),
                   jax.ShapeDtypeStruct((1, D), F32), jax.ShapeDtypeStruct((1, D), F32)],
        compiler_params=_cparams(("arbitrary",)),
    )(dout, h, ab, f, pre_g, post_g, wgu, wd)


def _mix_in_fwd(h, l, pre_g, win):
    T = h.shape[0]
    R = min(256, T)

    def body(h_ref, pg_ref, w_ref, n_ref, z_ref):
        xh, _ = _rms(h_ref[...])
        n = (xh * pg_ref[...]).astype(BF)
        n_ref[...] = n
        for ci in range(C // FC):
            z_ref[:, pl.ds(ci * FC, FC)] = _dot_nt(n, w_ref[pl.ds(ci * FC, FC), :])

    tile = lambda n: pl.BlockSpec((R, n), lambda i: (i, 0))
    return pl.pallas_call(
        body, name="mix_in_fwd", grid=(T // R,),
        in_specs=[tile(D), _row(l, D), _full((C, D))],
        out_specs=[tile(D), tile(C)],
        out_shape=[jax.ShapeDtypeStruct((T, D), BF), jax.ShapeDtypeStruct((T, C), F32)],
        compiler_params=_cparams(("parallel",)),
    )(h, pre_g, win)


def _mix_in_bwd(dout, dz, h, l, pre_g, win):
    T = h.shape[0]
    R = min(512, T)

    def body(do_ref, dz_ref, h_ref, pg_ref, w_ref, dh_ref, dpg_ref):
        @pl.when(pl.program_id(0) == 0)
        def _():
            dpg_ref[...] = jnp.zeros_like(dpg_ref)

        dn = _dot(dz_ref[...], w_ref[...])
        xh, xr = _rms(h_ref[...])
        dx, dp = _rms_bwd(xh, xr, pg_ref[...], dn)
        dpg_ref[...] += dp
        dh_ref[...] = do_ref[...] + dx

    tile = lambda n: pl.BlockSpec((R, n), lambda i: (i, 0))
    return pl.pallas_call(
        body, name="mix_in_bwd", grid=(T // R,),
        in_specs=[tile(D), tile(C), tile(D), _row(l, D), _full((C, D))],
        out_specs=[tile(D), pl.BlockSpec((1, D), lambda i: (0, 0))],
        out_shape=[jax.ShapeDtypeStruct((T, D), F32), jax.ShapeDtypeStruct((1, D), F32)],
        compiler_params=_cparams(("arbitrary",)),
    )(dout, dz, h, pre_g, win)


def _mix_specs(l, R, tile_of):
    def halo(rows, col_block):
        per = R // rows
        return pl.BlockSpec((rows, PW), lambda i: (jnp.maximum(tile_of(i) * per - 1, 0), col_block))
    return [
        pl.BlockSpec((R, C), lambda i: (tile_of(i), 0)),
        halo(POOL_HALO, 0), halo(CONV_HALO, 3), halo(CONV_HALO, 4),
        pl.BlockSpec((R, D), lambda i: (tile_of(i), 0)),
        _whole(l, (4, CHUNK, CHUNK)), _row(l, PW),
        _row(l, PW), _row(l, PW), _whole(l, (4, CHUNK, CHUNK)), _whole(l, (CHUNK, 4)),
        _whole(l, (CONV_HALO, PW)), _row(l, PW), _row(l, PW), _row(l, PW),
        _row(l, D),
        _full((D, PW)), _full((D, PW)), _full((D, PW)), _full((D, D)),
    ]


class _MixFwd:
    def __init__(self, R, tile, refs, scratch):
        (z_ref, zph_ref, zah_ref, zbh_ref, _h, pw_ref, ps_ref, lg_ref, lb_ref, ws_ref, bst_ref,
         ck_ref, cb_ref, cg_ref, cbb_ref, _qg, wpo_ref, wso_ref, wco_ref, wout_ref) = refs
        pbuf, xbuf, sbuf, pm_ref, sg_ref, cv_ref = scratch
        first = tile == 0
        tglob = tile * R + lax.broadcasted_iota(jnp.int32, (R, 1), 0)
        pbuf[pl.ds(0, POOL_HALO), :] = jnp.where(first, 0.0, zph_ref[...])
        pbuf[pl.ds(POOL_HALO, R), :] = z_ref[:, pl.ds(0, PW)]
        self.pooled, self.yg, self.cnt = [], [], []
        for gi, w in enumerate(POOL_WINDOWS):
            cols = pl.ds(gi * CHUNK, CHUNK)
            x = pbuf[pl.ds(POOL_HALO, R), cols]
            acc = x
            for j in range(1, w):
                acc = acc + pbuf[pl.ds(POOL_HALO - j, R), cols]
            cnt = jnp.minimum(tglob + 1, w).astype(F32)
            pooled = (acc / cnt - x).astype(BF)
            yg = _dot(pooled, pw_ref[gi].astype(BF))
            pm_ref[:, cols] = (yg * ps_ref[:, cols]).astype(BF)
            self.pooled.append(pooled)
            self.yg.append(yg)
            self.cnt.append(cnt)
        zu, zv = z_ref[:, pl.ds(OFF_U, PW)], z_ref[:, pl.ds(OFF_V, PW)]
        self.u = _gelu(zu)
        self.vh, self.vr = _ln(_gelu(zv))
        self.vln = (self.vh * lg_ref[...] + lb_ref[...]).astype(BF)
        tt = lax.broadcasted_iota(jnp.int32, (CHUNK, CHUNK), 0)
        ss = lax.broadcasted_iota(jnp.int32, (CHUNK, CHUNK), 1)
        self.causal = tt >= ss
        self.wc = [jnp.where(self.causal, ws_ref[hd], 0.0).astype(BF) for hd in range(4)]
        for ck in range(R // CHUNK):
            for hd in range(4):
                rows, cols = pl.ds(ck * CHUNK, CHUNK), pl.ds(hd * CHUNK, CHUNK)
                blk = self.vln[ck * CHUNK:(ck + 1) * CHUNK, hd * CHUNK:(hd + 1) * CHUNK]
                sbuf[rows, cols] = _dot(self.wc[hd], blk) + bst_ref[:, pl.ds(hd, 1)]
        self.s = sbuf[...]
        sg_ref[...] = (self.u * self.s).astype(BF)
        self.za = z_ref[:, pl.ds(OFF_A, PW)]
        self.sgb = _sigmoid(z_ref[:, pl.ds(OFF_B, PW)])
        xbuf[pl.ds(0, CONV_HALO), :] = jnp.where(first, 0.0, zah_ref[...] * _sigmoid(zbh_ref[...]))
        xbuf[pl.ds(CONV_HALO, R), :] = self.za * self.sgb
        y = jnp.zeros((R, PW), F32) + cb_ref[...]
        for k in range(CONV_K):
            y = y + xbuf[pl.ds(CONV_HALO - (CONV_K - 1) + k, R), :] * ck_ref[pl.ds(k, 1), :]
        self.yh, self.yr = _ln(y)
        self.yl = self.yh * cg_ref[...] + cbb_ref[...]
        self.sy = _sigmoid(self.yl)
        cv_ref[...] = (self.yl * self.sy).astype(BF)
        self.g = [_sigmoid(z_ref[:, pl.ds(OFF_G + j * D, D)]) for j in range(3)]
        self.y = [_dot_nt(pm_ref[...], wpo_ref[...]), _dot_nt(sg_ref[...], wso_ref[...]), _dot_nt(cv_ref[...], wco_ref[...])]
        self.merged = (self.g[0] * self.y[0] + self.g[1] * self.y[1] + self.g[2] * self.y[2]).astype(BF)
        self.o = _dot(self.merged, wout_ref[...])


def _mix_scratch(R):
    return [pltpu.VMEM((R + POOL_HALO, PW), F32), pltpu.VMEM((R + CONV_HALO, PW), F32), pltpu.VMEM((R, PW), F32)]


def _mix_core_fwd(z, h, l, sm, gw):
    T = h.shape[0]
    R = min(256, T)

    def body(*refs):
        ins, out_ref, scratch = refs[:20], refs[20], refs[21:]
        fw = _MixFwd(R, pl.program_id(0), ins, scratch)
        oh, _ = _rms(fw.o)
        out_ref[...] = ins[4][...] + oh * ins[15][...]

    act = pltpu.VMEM((R, PW), BF)
    return pl.pallas_call(
        body, name="mix_core_fwd", grid=(T // R,),
        in_specs=_mix_specs(l, R, lambda i: i),
        out_specs=pl.BlockSpec((R, D), lambda i: (i, 0)),
        out_shape=jax.ShapeDtypeStruct((T, D), F32),
        scratch_shapes=_mix_scratch(R) + [act, act, act],
        compiler_params=_cparams(("arbitrary",)),
    )(z, z, z, z, h, sm["pool_w"], sm["pool_scale"], sm["sgu_ln_g"], sm["sgu_ln_b"], sm["sgu_w_s"], sm["sgu_b_sT"],
      sm["conv_k"], sm["conv_dw_b"], sm["conv_ln_g"], sm["conv_ln_b"], sm["mix_post_g"],
      gw["po"], gw["so"], gw["co"], gw["wo"])


MIX_SMALL_GRADS = (("pool_w", (4, CHUNK, CHUNK)), ("pool_scale", (1, PW)), ("sgu_ln_g", (1, PW)), ("sgu_ln_b", (1, PW)),
                   ("sgu_w_s", (4, CHUNK, CHUNK)), ("sgu_b_sT", (CHUNK, 4)), ("conv_k", (CONV_HALO, PW)),
                   ("conv_dw_b", (1, PW)), ("conv_ln_g", (1, PW)), ("conv_ln_b", (1, PW)), ("mix_post_g", (1, D)))


def _mix_core_bwd(dout, z, h, l, sm, gw):
    T = h.shape[0]
    R = min(128, T)
    nt = T // R
    tile_of = lambda i: nt - 1 - i

    def body(*refs):
        do_ref, ins = refs[0], refs[1:21]
        (dz_ref, mg_ref, dob_ref, dy0_ref, dy1_ref, dy2_ref, pm_ref, sg_ref, cv_ref,
         dpw_ref, dps_ref, dlg_ref, dlb_ref, dws_ref, dbs_ref, dck_ref, dcb_ref, dcg_ref, dcbb_ref, dqg_ref) = refs[21:41]
        pbuf, xbuf, sbuf, qbuf, dybuf, dvbuf = refs[41:]
        (_z, _zp, _za, _zb, h_ref, pw_ref, ps_ref, lg_ref, lb_ref, ws_ref, bst_ref,
         ck_ref, cb_ref, cg_ref, cbb_ref, qg_ref, wpo_ref, wso_ref, wco_ref, wout_ref) = ins
        i = pl.program_id(0)
        small = (dpw_ref, dps_ref, dlg_ref, dlb_ref, dws_ref, dbs_ref, dck_ref, dcb_ref, dcg_ref, dcbb_ref, dqg_ref)

        @pl.when(i == 0)
        def _():
            for r in small:
                r[...] = jnp.zeros_like(r)
            qbuf[pl.ds(R, POOL_HALO), :] = jnp.zeros((POOL_HALO, PW), F32)
            dybuf[pl.ds(R, CONV_HALO), :] = jnp.zeros((CONV_HALO, PW), F32)

        fw = _MixFwd(R, tile_of(i), ins, (pbuf, xbuf, sbuf, pm_ref, sg_ref, cv_ref))
        mg_ref[...] = fw.merged
        oh, orr = _rms(fw.o)
        do, dq = _rms_bwd(oh, orr, qg_ref[...], do_ref[...])
        dqg_ref[...] += dq
        do = do.astype(BF)
        dob_ref[...] = do
        dm = _dot_nt(do, wout_ref[...])
        dys = []
        for j, dyj_ref in enumerate((dy0_ref, dy1_ref, dy2_ref)):
            g = fw.g[j]
            dz_ref[:, pl.ds(OFF_G + j * D, D)] = (dm * fw.y[j] * g * (1.0 - g)).astype(BF)
            dyj = (dm * g).astype(BF)
            dyj_ref[...] = dyj
            dys.append(dyj)
        dpm = _dot(dys[0], wpo_ref[...])
        dsg = _dot(dys[1], wso_ref[...])
        dcv = _dot(dys[2], wco_ref[...])
        for gi, w in enumerate(POOL_WINDOWS):
            cols = pl.ds(gi * CHUNK, CHUNK)
            dpm_g = dpm[:, gi * CHUNK:(gi + 1) * CHUNK]
            dps_ref[:, cols] += _colsum(dpm_g * fw.yg[gi])
            dyg = (dpm_g * ps_ref[:, cols]).astype(BF)
            dpw_ref[gi] += _dot_tn(fw.pooled[gi], dyg)
            dpooled = _dot_nt(dyg, pw_ref[gi].astype(BF))
            qbuf[pl.ds(0, R), cols] = dpooled / fw.cnt[gi]
            acc = -dpooled
            for j in range(w):
                acc = acc + qbuf[pl.ds(j, R), cols]
            dz_ref[:, cols] = acc.astype(BF)
        qbuf[pl.ds(R, POOL_HALO), :] = qbuf[pl.ds(0, POOL_HALO), :]
        ds = dsg * fw.u
        du = dsg * fw.s
        for ck in range(R // CHUNK):
            for hd in range(4):
                rows, cols = pl.ds(ck * CHUNK, CHUNK), pl.ds(hd * CHUNK, CHUNK)
                ds_f = ds[ck * CHUNK:(ck + 1) * CHUNK, hd * CHUNK:(hd + 1) * CHUNK]
                ds_blk = ds_f.astype(BF)
                v_blk = fw.vln[ck * CHUNK:(ck + 1) * CHUNK, hd * CHUNK:(hd + 1) * CHUNK]
                dbs_ref[:, pl.ds(hd, 1)] += jnp.sum(ds_f, axis=1, keepdims=True)
                dws_ref[hd] += jnp.where(fw.causal, _dot_nt(ds_blk, v_blk), 0.0)
                dvbuf[rows, cols] = _dot_tn(fw.wc[hd], ds_blk)
        dgv, dg, db = _ln_bwd(fw.vh, fw.vr, lg_ref[...], dvbuf[...])
        dlg_ref[...] += dg
        dlb_ref[...] += db
        dz_ref[:, pl.ds(OFF_V, PW)] = (dgv * _gelu_grad(_z[:, pl.ds(OFF_V, PW)])).astype(BF)
        dz_ref[:, pl.ds(OFF_U, PW)] = (du * _gelu_grad(_z[:, pl.ds(OFF_U, PW)])).astype(BF)
        dyl = dcv * (fw.sy * (1.0 + fw.yl * (1.0 - fw.sy)))
        dy, dg, db = _ln_bwd(fw.yh, fw.yr, cg_ref[...], dyl)
        dcg_ref[...] += dg
        dcbb_ref[...] += db
        dcb_ref[...] += _colsum(dy)
        dybuf[pl.ds(0, R), :] = dy
        dxg = jnp.zeros((R, PW), F32)
        for k in range(CONV_K):
            dck_ref[pl.ds(k, 1), :] += _colsum(dy * xbuf[pl.ds(CONV_HALO - (CONV_K - 1) + k, R), :])
            dxg = dxg + dybuf[pl.ds(CONV_K - 1 - k, R), :] * ck_ref[pl.ds(k, 1), :]
        dybuf[pl.ds(R, CONV_HALO), :] = dybuf[pl.ds(0, CONV_HALO), :]
        dz_ref[:, pl.ds(OFF_A, PW)] = (dxg * fw.sgb).astype(BF)
        dz_ref[:, pl.ds(OFF_B, PW)] = (dxg * fw.za * fw.sgb * (1.0 - fw.sgb)).astype(BF)

    tile = lambda n: pl.BlockSpec((R, n), lambda i: (tile_of(i), 0))
    small_specs = [pl.BlockSpec(shape, lambda i, nd=len(shape): (0,) * nd) for _, shape in MIX_SMALL_GRADS]
    outs = pl.pallas_call(
        body, name="mix_core_bwd", grid=(nt,),
        in_specs=[tile(D)] + _mix_specs(l, R, tile_of),
        out_specs=[tile(C), tile(D), tile(D), tile(D), tile(D), tile(D), tile(PW), tile(PW), tile(PW)] + small_specs,
        out_shape=[jax.ShapeDtypeStruct((T, C), BF)] + [jax.ShapeDtypeStruct((T, D), BF)] * 5
        + [jax.ShapeDtypeStruct((T, PW), BF)] * 3 + [jax.ShapeDtypeStruct(shape, F32) for _, shape in MIX_SMALL_GRADS],
        scratch_shapes=_mix_scratch(R) + [pltpu.VMEM((R + POOL_HALO, PW), F32), pltpu.VMEM((R + CONV_HALO, PW), F32),
                                          pltpu.VMEM((R, PW), F32)],
        compiler_params=_cparams(("arbitrary",)),
    )(dout, z, z, z, z, h, sm["pool_w"], sm["pool_scale"], sm["sgu_ln_g"], sm["sgu_ln_b"], sm["sgu_w_s"], sm["sgu_b_sT"],
      sm["conv_k"], sm["conv_dw_b"], sm["conv_ln_g"], sm["conv_ln_b"], sm["mix_post_g"],
      gw["po"], gw["so"], gw["co"], gw["wo"])
    return outs[:9], dict(zip([n for n, _ in MIX_SMALL_GRADS], outs[9:]))


def _ple_fwd(h, p, l, pre_g, post_g, wpr, wpg):
    T = h.shape[0]
    R = min(512, T)

    def body(h_ref, p_ref, pg_ref, qg_ref, wpr_ref, wpg_ref, out_ref, n_ref, pb_ref, gp_ref, e_ref):
        hh = h_ref[...]
        xh, _ = _rms(hh)
        n = (xh * pg_ref[...]).astype(BF)
        n_ref[...] = n
        pb = p_ref[...].astype(BF)
        pb_ref[:, pl.ds(0, PLE)] = pb
        pb_ref[:, pl.ds(PLE, PW - PLE)] = jnp.zeros((R, PW - PLE), BF)
        e = _dot_nt(pb, wpr_ref[:, pl.ds(0, PLE)])
        gp = _dot(n, wpg_ref[...])
        gp_ref[...] = gp
        e_ref[...] = e
        qh, _ = _rms(_sigmoid(gp) * e)
        out_ref[...] = hh + qh * qg_ref[...]

    tile = lambda n: pl.BlockSpec((R, n), lambda i: (i, 0))
    return pl.pallas_call(
        body, name="ple_fwd", grid=(T // R,),
        in_specs=[tile(D), pl.BlockSpec((None, None, R, PLE), lambda i: (l, 0, i, 0)), _row(l, D), _row(l, D),
                  _full((D, PW)), _full((D, D))],
        out_specs=[tile(D), tile(D), tile(PW), tile(D), tile(D)],
        out_shape=[jax.ShapeDtypeStruct((T, D), F32), jax.ShapeDtypeStruct((T, D), BF), jax.ShapeDtypeStruct((T, PW), BF),
                   jax.ShapeDtypeStruct((T, D), F32), jax.ShapeDtypeStruct((T, D), F32)],
        compiler_params=_cparams(("parallel",)),
    )(h, p, pre_g, post_g, wpr, wpg)


def _ple_bwd(dout, h, gp, e, l, pre_g, post_g, wpg):
    T = h.shape[0]
    R = min(512, T)

    def body(do_ref, h_ref, gp_ref, e_ref, pg_ref, qg_ref, wpg_ref, dh_ref, de_ref, dgp_ref, dpg_ref, dqg_ref):
        @pl.when(pl.program_id(0) == 0)
        def _():
            dpg_ref[...] = jnp.zeros_like(dpg_ref)
            dqg_ref[...] = jnp.zeros_like(dqg_ref)

        do = do_ref[...]
        g = _sigmoid(gp_ref[...])
        e = e_ref[...]
        qh, qr = _rms(g * e)
        dq, dqg = _rms_bwd(qh, qr, qg_ref[...], do)
        dqg_ref[...] += dqg
        de_ref[...] = (dq * g).astype(BF)
        dgp = (dq * e * g * (1.0 - g)).astype(BF)
        dgp_ref[...] = dgp
        dn = _dot_nt(dgp, wpg_ref[...])
        xh, xr = _rms(h_ref[...])
        dx, dp = _rms_bwd(xh, xr, pg_ref[...], dn)
        dpg_ref[...] += dp
        dh_ref[...] = do + dx

    tile = lambda n: pl.BlockSpec((R, n), lambda i: (i, 0))
    acc = pl.BlockSpec((1, D), lambda i: (0, 0))
    return pl.pallas_call(
        body, name="ple_bwd", grid=(T // R,),
        in_specs=[tile(D), tile(D), tile(D), tile(D), _row(l, D), _row(l, D), _full((D, D))],
        out_specs=[tile(D), tile(D), tile(D), acc, acc],
        out_shape=[jax.ShapeDtypeStruct((T, D), F32), jax.ShapeDtypeStruct((T, D), BF), jax.ShapeDtypeStruct((T, D), BF),
                   jax.ShapeDtypeStruct((1, D), F32), jax.ShapeDtypeStruct((1, D), F32)],
        compiler_params=_cparams(("arbitrary",)),
    )(dout, h, gp, e, pre_g, post_g, wpg)


def _loss_head(y, target):
    T = y.shape[0]
    R = min(512, T)

    def body(y_ref, t_ref, loss_ref, dy_ref):
        @pl.when(pl.program_id(0) == 0)
        def _():
            loss_ref[...] = jnp.zeros_like(loss_ref)

        err = y_ref[...] - t_ref[0]
        dy_ref[...] = err * (1.0 / D)
        loss_ref[...] += 0.5 * jnp.sum(_mean(err * err), axis=0, keepdims=True)

    tile = pl.BlockSpec((R, D), lambda i: (i, 0))
    return pl.pallas_call(
        body, name="loss_head", grid=(T // R,),
        in_specs=[tile, pl.BlockSpec((1, R, D), lambda i: (0, i, 0))],
        out_specs=[pl.BlockSpec((1, 1), lambda i: (0, 0)), tile],
        out_shape=[jax.ShapeDtypeStruct((1, 1), F32), jax.ShapeDtypeStruct((T, D), F32)],
        compiler_params=_cparams(("arbitrary",)),
    )(y, target)


def _wgrad_f(slab, a, b, l, name, a_col0=0):
    off, r = LAYOUT[name][0], LAYOUT[name][1]
    per = FC // r
    nblk = NDEV // per
    a0 = a_col0 // FC

    T = a.shape[0]
    n = b.shape[1]
    kt = min(1024, T)

    def body(a_ref, b_ref, slab_ref, out_ref, acc_ref):
        k = pl.program_id(1)

        @pl.when(k == 0)
        def _():
            acc_ref[...] = jnp.zeros_like(acc_ref)

        acc_ref[...] += _dot_tn(a_ref[...], b_ref[...])

        @pl.when(k == pl.num_programs(1) - 1)
        def _():
            out_ref[...] = acc_ref[...].reshape(per, r, n).astype(out_ref.dtype)

    return pl.pallas_call(
        body, name="wgrad_" + name, grid=(nblk, T // kt),
        in_specs=[pl.BlockSpec((kt, FC), lambda i, k: (k, i + a0)), pl.BlockSpec((kt, n), lambda i, k: (k, 0)),
                  pl.BlockSpec(memory_space=pl.ANY)],
        out_specs=pl.BlockSpec((per, None, r, n), lambda i, k: (i, l, off // r, 0)),
        out_shape=jax.ShapeDtypeStruct(slab.shape, slab.dtype),
        scratch_shapes=[pltpu.VMEM((FC, n), F32)],
        input_output_aliases={2: 0},
        compiler_params=_cparams(("parallel", "arbitrary")),
    )(a, b, slab)


def _wgrad_d(slab, a, b, l, name):
    off, r, col0, width = LAYOUT[name][:4]
    T = a.shape[0]
    n = b.shape[1]
    kt = min(1024, T)

    def body(a_ref, b_ref, slab_ref, out_ref, acc_ref):
        k = pl.program_id(0)

        @pl.when(k == 0)
        def _():
            acc_ref[...] = jnp.zeros_like(acc_ref)

        acc_ref[...] += _dot_tn(a_ref[...], b_ref[...])

        @pl.when(k == pl.num_programs(0) - 1)
        def _():
            out_ref[...] = acc_ref[...].reshape(NDEV, r, n).astype(out_ref.dtype)

    return pl.pallas_call(
        body, name="wgrad_" + name, grid=(T // kt,),
        in_specs=[pl.BlockSpec((kt, D), lambda k: (k, 0)), pl.BlockSpec((kt, n), lambda k: (k, 0)),
                  pl.BlockSpec(memory_space=pl.ANY)],
        out_specs=pl.BlockSpec((NDEV, None, r, n), lambda k: (0, l, off // r, col0 // n)),
        out_shape=jax.ShapeDtypeStruct(slab.shape, slab.dtype),
        scratch_shapes=[pltpu.VMEM((D, n), F32)],
        input_output_aliases={2: 0},
        compiler_params=_cparams(("arbitrary",)),
    )(a, b, slab)


def _adamw(w, g, m, v):
    m = ADAM_B1 * m + (1.0 - ADAM_B1) * g
    v = ADAM_B2 * v + (1.0 - ADAM_B2) * (g * g)
    m_hat = m / (1.0 - ADAM_B1 ** ADAM_STEP)
    v_hat = v / (1.0 - ADAM_B2 ** ADAM_STEP)
    delta = -ADAM_LR * (m_hat / (jnp.sqrt(v_hat) + ADAM_EPS) + ADAM_WD * w)
    return delta, m, v


def _adam_big(name, me, slab, recv, w, m, v):
    off, rows, col0, width, tr = LAYOUT[name]
    L = w.shape[0]
    wshape = w.shape[1:]
    nt = 4 if name == "w_in" else 1
    if tr:
        pblk, wblk = (rows, width // nt), (wshape[0] // nt, wshape[1])
        pmap = lambda t: (off // rows, col0 // (width // nt) + t)
    else:
        pblk, wblk = (rows // nt, width), (wshape[0] // nt, wshape[1])
        pmap = lambda t: (off // (rows // nt) + t, 0)

    def body(me_ref, s_ref, r_ref, w_ref, m_ref, v_ref, g_out, d_out, m_out, v_out):
        g = s_ref[...].astype(F32)
        for k in range(NDEV - 1):
            g = g + r_ref[k].astype(F32)
        if tr:
            g = g.T
        d, mm, vv = _adamw(w_ref[...], g, m_ref[...], v_ref[...])
        g_out[...] = g
        d_out[...] = d
        m_out[...] = mm
        v_out[...] = vv

    wspec = pl.BlockSpec((None,) + wblk, lambda l, t, me: (l, t, 0))
    grid_spec = pltpu.PrefetchScalarGridSpec(
        num_scalar_prefetch=1, grid=(L, nt),
        in_specs=[pl.BlockSpec((None, None) + pblk, lambda l, t, me: (me[0], l) + pmap(t)),
                  pl.BlockSpec((None, NDEV - 1) + pblk, lambda l, t, me: (l, 0) + pmap(t)), wspec, wspec, wspec],
        out_specs=[wspec] * 4)
    return pl.pallas_call(
        body, name="adam_" + name, grid_spec=grid_spec, out_shape=[jax.ShapeDtypeStruct(w.shape, F32)] * 4,
        compiler_params=_cparams(("parallel", "parallel")),
    )(me, slab, recv, w, m, v)


def _adam_small(gall, w, m, v):
    rows = w.shape[0]
    tr = 32

    def body(g_ref, w_ref, m_ref, v_ref, g_out, d_out, m_out, v_out):
        g = g_ref[0]
        for k in range(1, NDEV):
            g = g + g_ref[k]
        d, mm, vv = _adamw(w_ref[...], g, m_ref[...], v_ref[...])
        g_out[...] = g
        d_out[...] = d
        m_out[...] = mm
        v_out[...] = vv

    spec = pl.BlockSpec((tr, D), lambda i: (i, 0))
    return pl.pallas_call(
        body, name="adam_small", grid=(rows // tr,),
        in_specs=[pl.BlockSpec((NDEV, tr, D), lambda i: (0, i, 0)), spec, spec, spec],
        out_specs=[spec] * 4, out_shape=[jax.ShapeDtypeStruct(w.shape, F32)] * 4,
        compiler_params=_cparams(("parallel",)),
    )(gall, w, m, v)


def _adam_plain(g, w, m, v):
    def body(g_ref, w_ref, m_ref, v_ref, d_out, m_out, v_out):
        d, mm, vv = _adamw(w_ref[...], g_ref[...], m_ref[...], v_ref[...])
        d_out[...] = d
        m_out[...] = mm
        v_out[...] = vv

    vm = pl.BlockSpec(memory_space=pltpu.VMEM)
    return pl.pallas_call(
        body, name="adam_conv_k", in_specs=[vm] * 4, out_specs=[vm] * 3,
        out_shape=[jax.ShapeDtypeStruct(w.shape, F32)] * 3,
    )(g, w, m, v)


def _pad_cols(a, n):
    return jnp.pad(a, [(0, 0)] * (a.ndim - 1) + [(0, n - a.shape[-1])])


def _pack_small(d, conv_k_full):
    L = d["pool_w"].shape[0]
    gains = jnp.stack([d[n].reshape(L, D) for n in GAINS], axis=1)
    halves = [_pad_cols(d[n].reshape(L, PW), D) for n in HALVES] + [jnp.zeros((L, D), F32)]
    halves = jnp.stack(halves, axis=1)
    ck = jnp.zeros((L, 16, D), F32) if conv_k_full is None else conv_k_full.reshape(L, 16, D)
    out = jnp.concatenate([gains, halves, d["pool_w"].reshape(L, 64, D), d["sgu_w_s"].reshape(L, 64, D), ck], axis=1)
    return out.reshape(L * SMALL_ROWS, D)


def _unpack_small(a, like):
    L = a.shape[0] // SMALL_ROWS
    a = a.reshape(L, SMALL_ROWS, D)
    out = {}
    for i, n in enumerate(GAINS):
        out[n] = a[:, i, :].reshape(like[n].shape)
    for i, n in enumerate(HALVES):
        out[n] = a[:, 8 + i, :PW].reshape(like[n].shape)
    out["pool_w"] = a[:, 16:80, :].reshape(like["pool_w"].shape)
    out["sgu_w_s"] = a[:, 80:144, :].reshape(like["sgu_w_s"].shape)
    out["conv_k_full"] = a[:, 144:160, :].reshape(L, CONV_HALO, PW)
    return out


WEIGHTS = ("ffn1_pre_g", "ffn1_w_gate", "ffn1_w_up", "ffn1_w_down", "ffn1_post_g", "mix_pre_g", "w_in", "pool_w", "pool_scale",
           "w_pool_out", "sgu_ln_g", "sgu_ln_b", "sgu_w_s", "sgu_b_s", "w_sgu_out", "conv_dw_k", "conv_dw_b", "conv_ln_g",
           "conv_ln_b", "w_conv_out", "w_out", "mix_post_g", "ffn2_pre_g", "ffn2_w_gate", "ffn2_w_up", "ffn2_w_down",
           "ffn2_post_g", "ple_w_proj", "ple_pre_g", "ple_w_gate", "ple_post_g")
SMALL = GAINS + HALVES + ("pool_w", "sgu_w_s")
FFN1_ROWS = ((704, 1056),)
FFN1_ROWS_OUT = ((0, 704), (1760, ROWS - 1760))


class _Comm:
    def __init__(self, w, me):
        self.w, self.me = w, me

    def gather_start(self, l, after):
        packed, lands = _prep(self.w, l, self.me, [lax.empty(GATHERED[n], BF) for n in GNAMES])
        send_sems, recv_sems, packed, lands, token = _gather_start(l, packed, lands, packed if after is None else after)
        return (send_sems, recv_sems, packed, lands), token[0, 0]

    def gather_finish(self, l, state, after):
        send_sems, recv_sems, packed, lands = state
        packed, lands = _gather_wait(l, send_sems, recv_sems, packed, lands, packed if after is None else after)
        return _gather_forward(l, packed, lands)

    def scatter_start(self, tag, l, slab, recv, parts):
        send_sems, recv_sems, slab, recv, token = _scatter_start(tag, l, slab, recv, parts)
        return (tag, l, send_sems, recv_sems, parts), slab, recv, token[0, 0]

    def scatter_finish(self, state, slab, recv):
        tag, l, send_sems, recv_sems, parts = state
        return _scatter_wait(tag, l, send_sems, recv_sems, slab, recv, parts)


def _fwd_bwd(x, p, target, w, conv_k, comm):
    L = w["w_in"].shape[0]
    T = x.shape[1]
    row = lambda a: a.reshape(L, 1, a.shape[-1])
    sm = {n: row(w[n]) for n in GAINS + ("pool_scale", "sgu_ln_g", "sgu_ln_b", "conv_dw_b", "conv_ln_g", "conv_ln_b")}
    sm.update(pool_w=w["pool_w"], sgu_w_s=w["sgu_w_s"], sgu_b_sT=w["sgu_b_s"].transpose(0, 2, 1), conv_k=conv_k)

    h = x[0]
    saved, gws = [], [None] * L
    state, _ = comm.gather_start(0, conv_k)
    gws[0] = comm.gather_finish(0, state, None)
    for l in range(L):
        gw = gws[l]
        pre1 = sm["ffn1_pre_g"]
        if l + 1 < L:
            state, token = comm.gather_start(l + 1, gw["pg"])
            pre1 = pre1 + token
        s = {"h0": h}
        h, s["n1"], s["ab1"], s["f1"] = _ffn_fwd(h, l, pre1, sm["ffn1_post_g"], gw["gu1"], gw["d1"])
        s["h1"] = h
        s["nm"], s["z"] = _mix_in_fwd(h, l, sm["mix_pre_g"], gw["win"])
        h = _mix_core_fwd(s["z"], h, l, sm, gw)
        s["h2"] = h
        h, s["n2"], s["ab2"], s["f2"] = _ffn_fwd(h, l, sm["ffn2_pre_g"], sm["ffn2_post_g"], gw["gu2"], gw["d2"])
        s["h3"] = h
        h, s["np"], s["pb"], s["gp"], s["e"] = _ple_fwd(h, p, l, sm["ple_pre_g"], sm["ple_post_g"], gw["pr"], gw["pg"])
        saved.append(s)
        if l + 1 < L:
            gws[l + 1] = comm.gather_finish(l + 1, state, h)

    loss_part, dh = _loss_head(h, target)

    slab = lax.empty((NDEV, L, ROWS, D), BF)
    recv = lax.empty((L, NDEV - 1, ROWS, D), BF)
    sg = {n: [None] * L for n in SMALL + ("conv_k", "sgu_b_sT")}
    pending, token = [], None
    for l in reversed(range(L)):
        s, gw = saved[l], gws[l]
        post = sm["ple_post_g"] if token is None else sm["ple_post_g"] + token
        dh, de, dgp, sg["ple_pre_g"][l], sg["ple_post_g"][l] = _ple_bwd(
            dh, s["h3"], s["gp"], s["e"], l, sm["ple_pre_g"], post, gw["pg"])
        slab = _wgrad_d(slab, s["np"], dgp, l, "ple_w_gate")
        slab = _wgrad_d(slab, de, s["pb"], l, "ple_w_proj")

        dh, dab, ss, df, sg["ffn2_pre_g"][l], sg["ffn2_post_g"][l] = _ffn_bwd(
            dh, s["h2"], s["ab2"], s["f2"], l, sm["ffn2_pre_g"], sm["ffn2_post_g"], gw["gu2"], gw["d2"])
        slab = _wgrad_f(slab, dab, s["n2"], l, "ffn2_w_gate")
        slab = _wgrad_f(slab, dab, s["n2"], l, "ffn2_w_up", a_col0=F)
        slab = _wgrad_f(slab, ss, df, l, "ffn2_w_down")

        (dz, mg, dob, dy0, dy1, dy2, pm, sgv, cv), g_mix = _mix_core_bwd(dh, s["z"], s["h1"], l, sm, gw)
        for n in g_mix:
            sg[n][l] = g_mix[n]
        dh, sg["mix_pre_g"][l] = _mix_in_bwd(dh, dz, s["h1"], l, sm["mix_pre_g"], gw["win"])
        slab = _wgrad_f(slab, dz, s["nm"], l, "w_in")
        slab = _wgrad_d(slab, mg, dob, l, "w_out")
        slab = _wgrad_d(slab, dy0, pm, l, "w_pool_out")
        slab = _wgrad_d(slab, dy1, sgv, l, "w_sgu_out")
        slab = _wgrad_d(slab, dy2, cv, l, "w_conv_out")

        pre1, parts, tag = sm["ffn1_pre_g"], ((0, ROWS),), str(l)
        if l == 0:
            for st in pending:
                slab, recv = comm.scatter_finish(st, slab, recv)
            st, slab, recv, token = comm.scatter_start("0a", 0, slab, recv, FFN1_ROWS_OUT)
            pending, pre1, parts, tag = [st], pre1 + token, FFN1_ROWS, "0b"
        dh, dab, ss, df, sg["ffn1_pre_g"][l], sg["ffn1_post_g"][l] = _ffn_bwd(
            dh, s["h0"], s["ab1"], s["f1"], l, pre1, sm["ffn1_post_g"], gw["gu1"], gw["d1"])
        slab = _wgrad_f(slab, dab, s["n1"], l, "ffn1_w_gate")
        slab = _wgrad_f(slab, dab, s["n1"], l, "ffn1_w_up", a_col0=F)
        slab = _wgrad_f(slab, ss, df, l, "ffn1_w_down")
        if l > 0:
            for st in pending:
                slab, recv = comm.scatter_finish(st, slab, recv)
            pending = []
        st, slab, recv, token = comm.scatter_start(tag, l, slab, recv, parts)
        pending.append(st)
    for st in pending:
        slab, recv = comm.scatter_finish(st, slab, recv)
    return loss_part, dh.reshape(1, T, D), slab, recv, sg


def _step(x, p, target, w, m, v):
    L = w["w_in"].shape[0]
    ix, iy, ic = lax.axis_index("x"), lax.axis_index("y"), lax.axis_index("c")
    me = 4 * ix + 2 * iy + ic

    ck_local = jnp.pad(w["conv_dw_k"].reshape(L, CONV_K, 64), ((0, 0), (0, 1), (0, 0))).reshape(L * 2, D)
    ck_all = _gather_rows("allgather_conv_k", _pad_rows8(ck_local))[:, :L * 2]
    conv_k = ck_all.reshape(NDEV, L, CONV_HALO, 64).transpose(1, 2, 0, 3).reshape(L, CONV_HALO, PW)

    me_arr = me.astype(jnp.int32).reshape(1)
    loss_part, grad_x, slab, recv, sg = _fwd_bwd(x, p, target, w, conv_k, _Comm(w, me_arr))
    loss = lax.psum(loss_part[0, 0], ("x", "y", "c"))

    res = {n: _adam_big(n, me_arr, slab, recv, w[n], m[n], v[n]) for n in BIG}

    sgrads = {n: jnp.stack(sg[n]) for n in SMALL if n != "sgu_b_s"}
    sgrads["sgu_b_s"] = jnp.stack(sg["sgu_b_sT"]).transpose(0, 2, 1)
    gall = _gather_rows("allgather_small_grads", _pack_small(sgrads, jnp.stack(sg["conv_k"])))
    outs = _adam_small(gall, _pack_small(w, None), _pack_small(m, None), _pack_small(v, None))
    unpacked = [_unpack_small(o, w) for o in outs]
    for n in SMALL:
        res[n] = tuple(u[n] for u in unpacked)

    gk = lax.dynamic_slice_in_dim(unpacked[0]["conv_k_full"][:, :CONV_K, :], me * 64, 64, axis=2)
    shp = w["conv_dw_k"].shape
    flat = lambda a: a.reshape(L * CONV_K, 64)
    dk, mk, vk = _adam_plain(flat(gk), flat(w["conv_dw_k"]), flat(m["conv_dw_k"]), flat(v["conv_dw_k"]))
    res["conv_dw_k"] = (gk.reshape(shp), dk.reshape(shp), mk.reshape(shp), vk.reshape(shp))

    return (loss, grad_x, *[res[n][0] for n in WEIGHTS], *[res[n][1] for n in WEIGHTS],
            *[res[n][2] for n in WEIGHTS], *[res[n][3] for n in WEIGHTS])


def _pad_rows8(a):
    return jnp.pad(a, ((0, (-a.shape[0]) % 8), (0, 0)))


def kernel(x, p, ffn1_pre_g, ffn1_w_gate, ffn1_w_up, ffn1_w_down, ffn1_post_g, mix_pre_g, w_in, pool_w, pool_scale, w_pool_out, sgu_ln_g, sgu_ln_b, sgu_w_s, sgu_b_s, w_sgu_out, conv_dw_k, conv_dw_b, conv_ln_g, conv_ln_b, w_conv_out, w_out, mix_post_g, ffn2_pre_g, ffn2_w_gate, ffn2_w_up, ffn2_w_down, ffn2_post_g, ple_w_proj, ple_pre_g, ple_w_gate, ple_post_g, loss_target, m_ffn1_pre_g, m_ffn1_w_gate, m_ffn1_w_up, m_ffn1_w_down, m_ffn1_post_g, m_mix_pre_g, m_w_in, m_pool_w, m_pool_scale, m_w_pool_out, m_sgu_ln_g, m_sgu_ln_b, m_sgu_w_s, m_sgu_b_s, m_w_sgu_out, m_conv_dw_k, m_conv_dw_b, m_conv_ln_g, m_conv_ln_b, m_w_conv_out, m_w_out, m_mix_post_g, m_ffn2_pre_g, m_ffn2_w_gate, m_ffn2_w_up, m_ffn2_w_down, m_ffn2_post_g, m_ple_w_proj, m_ple_pre_g, m_ple_w_gate, m_ple_post_g, v_ffn1_pre_g, v_ffn1_w_gate, v_ffn1_w_up, v_ffn1_w_down, v_ffn1_post_g, v_mix_pre_g, v_w_in, v_pool_w, v_pool_scale, v_w_pool_out, v_sgu_ln_g, v_sgu_ln_b, v_sgu_w_s, v_sgu_b_s, v_w_sgu_out, v_conv_dw_k, v_conv_dw_b, v_conv_ln_g, v_conv_ln_b, v_w_conv_out, v_w_out, v_mix_post_g, v_ffn2_pre_g, v_ffn2_w_gate, v_ffn2_w_up, v_ffn2_w_down, v_ffn2_post_g, v_ple_w_proj, v_ple_pre_g, v_ple_w_gate, v_ple_post_g):
    args = dict(locals())
    w = {n: args[n] for n in WEIGHTS}
    m = {n: args["m_" + n] for n in WEIGHTS}
    v = {n: args["v_" + n] for n in WEIGHTS}
    return _step(x, p, loss_target, w, m, v)
```

```python
import functools

import jax
import jax.numpy as jnp
from jax import lax
from jax.experimental import pallas as pl
from jax.experimental.pallas import tpu as pltpu

D = 1024
F = 2816
C = 5632
PW = 512
PLE = 256
NDEV = 8
CHUNK = 128
POOL_WINDOWS = (2, 4, 8, 16)
CONV_K = 31
POOL_HALO = 16
CONV_HALO = 32
EPS = 1e-6
OFF_U, OFF_V, OFF_A, OFF_B, OFF_G = 512, 1024, 1536, 2048, 2560

ADAM_LR, ADAM_B1, ADAM_B2, ADAM_EPS, ADAM_WD, ADAM_STEP = 0.001, 0.9, 0.999, 1e-08, 0.01, 10

BF = jnp.bfloat16
F32 = jnp.float32
VMEM_LIMIT = 56 * 1024 * 1024
MESH = pl.DeviceIdType.MESH
INV_SQRT2 = 0.7071067811865476
INV_SQRT_2PI = 0.3989422804014327

ROWS = 3328
LAYOUT = {
    "w_in": (0, 704, 0, 1024, True),
    "ffn1_w_gate": (704, 352, 0, 1024, True),
    "ffn1_w_up": (1056, 352, 0, 1024, True),
    "ffn1_w_down": (1408, 352, 0, 1024, False),
    "ffn2_w_gate": (1760, 352, 0, 1024, True),
    "ffn2_w_up": (2112, 352, 0, 1024, True),
    "ffn2_w_down": (2464, 352, 0, 1024, False),
    "w_pool_out": (2816, 128, 0, 512, True),
    "w_sgu_out": (2816, 128, 512, 512, True),
    "w_conv_out": (2944, 128, 0, 512, True),
    "ple_w_proj": (2944, 128, 512, 256, True),
    "w_out": (3072, 128, 0, 1024, False),
    "ple_w_gate": (3200, 128, 0, 1024, False),
}
BIG = tuple(LAYOUT)
GATHERED = {
    "win": (C, D), "gu1": (2 * F, D), "d1": (F, D), "gu2": (2 * F, D), "d2": (F, D),
    "po": (D, PW), "so": (D, PW), "co": (D, PW), "pr": (D, PW), "wo": (D, D), "pg": (D, D),
}
PIECES = (
    ("win", 0, 0, 704, 0, 1024), ("gu1", 0, 704, 352, 0, 1024), ("gu1", F, 1056, 352, 0, 1024),
    ("d1", 0, 1408, 352, 0, 1024), ("gu2", 0, 1760, 352, 0, 1024), ("gu2", F, 2112, 352, 0, 1024),
    ("d2", 0, 2464, 352, 0, 1024), ("po", 0, 2816, 128, 0, 512), ("so", 0, 2816, 128, 512, 512),
    ("co", 0, 2944, 128, 0, 512), ("pr", 0, 2944, 128, 512, 512), ("wo", 0, 3072, 128, 0, 1024),
    ("pg", 0, 3200, 128, 0, 1024),
)
GAINS = ("ffn1_pre_g", "ffn1_post_g", "mix_pre_g", "mix_post_g", "ffn2_pre_g", "ffn2_post_g", "ple_pre_g", "ple_post_g")
HALVES = ("pool_scale", "sgu_ln_g", "sgu_ln_b", "sgu_b_s", "conv_dw_b", "conv_ln_g", "conv_ln_b")
SMALL_ROWS = 160


def _cparams(sem=None, **kw):
    if sem is not None:
        kw["dimension_semantics"] = sem
    return pltpu.CompilerParams(vmem_limit_bytes=VMEM_LIMIT, **kw)


def _whole(l, shape):
    nd = len(shape)
    return pl.BlockSpec((None,) + tuple(shape), lambda *_: (l,) + (0,) * nd, pipeline_mode=pl.Buffered(1))


def _full(shape):
    nd = len(shape)
    return pl.BlockSpec(tuple(shape), lambda *_: (0,) * nd, pipeline_mode=pl.Buffered(1))


def _row(l, n):
    return pl.BlockSpec((None, 1, n), lambda *_: (l, 0, 0))


def _dot(a, b):
    return jnp.dot(a, b, preferred_element_type=F32)


def _dot_nt(a, b):
    return lax.dot_general(a, b, (((1,), (1,)), ((), ())), preferred_element_type=F32)


def _dot_tn(a, b):
    return lax.dot_general(a, b, (((0,), (0,)), ((), ())), preferred_element_type=F32)


def _mean(x):
    return jnp.mean(x, axis=-1, keepdims=True)


def _colsum(x):
    return jnp.sum(x, axis=0, keepdims=True)


def _rms(x):
    r = lax.rsqrt(_mean(x * x) + EPS)
    return x * r, r


def _rms_bwd(xh, r, g, dy):
    dxh = dy * g
    return r * (dxh - xh * _mean(dxh * xh)), _colsum(dy * xh)


def _ln(x):
    xc = x - _mean(x)
    r = lax.rsqrt(_mean(xc * xc) + EPS)
    return xc * r, r


def _ln_bwd(xh, r, g, dy):
    dxh = dy * g
    return r * (dxh - _mean(dxh) - xh * _mean(dxh * xh)), _colsum(dy * xh), _colsum(dy)


def _sigmoid(x):
    return jax.nn.sigmoid(x)


def _prep(w, l, me, lands):
    nb, npc = len(BIG), len(PIECES)

    def body(me_ref, *refs):
        ins, out = dict(zip(BIG, refs[:nb])), refs[nb + NG]
        land_refs, sems = refs[nb + NG + 1:nb + 2 * NG + 1], refs[-1]
        for name, (off, rows, col0, width, tr) in LAYOUT.items():
            v = ins[name][...]
            if tr:
                v = v.T
            out[pl.ds(off, rows), pl.ds(col0, width)] = v.astype(BF)
        out[pl.ds(2944, 128), pl.ds(768, 256)] = jnp.zeros((128, 256), BF)
        mine = [pltpu.make_async_copy(src, dst, sems.at[i]) for i, (src, dst) in enumerate(_piece_refs(out, land_refs, me_ref[0]))]
        for cp in mine:
            cp.start()
        for cp in mine:
            cp.wait()

    grid_spec = pltpu.PrefetchScalarGridSpec(
        num_scalar_prefetch=1, grid=(1,),
        in_specs=[pl.BlockSpec((None,) + w[n].shape[1:], lambda i, me: (l, 0, 0)) for n in BIG] + [ANY] * NG,
        out_specs=[pl.BlockSpec((ROWS, D), lambda i, me: (0, 0))] + [ANY] * NG,
        scratch_shapes=[pltpu.SemaphoreType.DMA((npc,))])
    outs = pl.pallas_call(
        body, name="prep", grid_spec=grid_spec,
        out_shape=[jax.ShapeDtypeStruct((ROWS, D), BF)] + [jax.ShapeDtypeStruct(a.shape, a.dtype) for a in lands],
        input_output_aliases={1 + nb + i: 1 + i for i in range(NG)},
        compiler_params=_cparams(("arbitrary",)),
    )(me, *[w[n] for n in BIG], *lands)
    return outs[0], list(outs[1:])


def _place():
    x, y, c = lax.axis_index("x"), lax.axis_index("y"), lax.axis_index("c")
    chips = [(1 - x, y), (x, 1 - y), (1 - x, 1 - y)]
    return x, y, c, chips


def _allgather(name, src, dsts, pieces):
    npc = len(pieces)

    def body(src_ref, *rest):
        outs, (send_sems, recv_sems, local_sems) = rest[:len(dsts)], rest[len(dsts):]
        x, y, c, chips = _place()
        me, sibling = (x, y, c), (x, y, 1 - c)

        def shard_of(dev):
            return 4 * dev[0] + 2 * dev[1] + dev[2]

        def copies(k, block, to, from_src):
            res = []
            for di, dst_fn, src_sl in pieces:
                dst = outs[di].at[dst_fn(shard_of(block))]
                s = src_ref.at[src_sl] if from_src else dst
                res.append(pltpu.make_async_remote_copy(src_ref=s, dst_ref=dst, send_sem=send_sems.at[k],
                                                        recv_sem=recv_sems.at[k], device_id=to, device_id_type=MESH))
            return res

        def whole(k):
            return pltpu.make_async_remote_copy(src_ref=src_ref, dst_ref=src_ref, send_sem=send_sems.at[k],
                                                recv_sem=recv_sems.at[k], device_id=me, device_id_type=MESH)

        mine = [pltpu.make_async_copy(src_ref.at[src_sl], outs[di].at[dst_fn(shard_of(me))], local_sems.at[i])
                for i, (di, dst_fn, src_sl) in enumerate(pieces)]
        for cp in mine:
            cp.start()
        for cp in copies(0, me, sibling, True):
            cp.start()
        for j, chip in enumerate(chips):
            for cp in copies(1 + j, me, (*chip, c), True):
                cp.start()
        for j, chip in enumerate(chips):
            whole(1 + j).wait_recv()
            for cp in copies(4 + j, (*chip, c), sibling, False):
                cp.start()
        for k in (0, 4, 5, 6):
            whole(k).wait_recv()
        for k in range(7):
            whole(k).wait_send()
        for cp in mine:
            cp.wait()

    any_spec = pl.BlockSpec(memory_space=pl.ANY)
    return pl.pallas_call(
        body, name=name, in_specs=[any_spec], out_specs=[any_spec] * len(dsts), out_shape=dsts,
        scratch_shapes=[pltpu.SemaphoreType.DMA((7,)), pltpu.SemaphoreType.DMA((7,)), pltpu.SemaphoreType.DMA((npc,))],
    )(src)


HBM = pl.BlockSpec(memory_space=pltpu.HBM)
SEM = pl.BlockSpec(memory_space=pltpu.SEMAPHORE)
ANY = pl.BlockSpec(memory_space=pl.ANY)
EFFECT = pltpu.SideEffectType.DATAFLOW_SIDE_EFFECTING
GNAMES = tuple(GATHERED)
NG = len(GNAMES)


def _in_hbm(a):
    return pltpu.with_memory_space_constraint(a, pltpu.HBM)


def _piece_refs(packed_ref, land_refs, shard):
    out = []
    for gname, row0, poff, prow, pcol, width in PIECES:
        land = land_refs[GNAMES.index(gname)]
        out.append((packed_ref.at[pl.ds(poff, prow), pl.ds(pcol, width)], land.at[pl.ds(row0 + shard * prow, prow), :]))
    return out


def _gather_start(l, packed, lands, after):
    def body(packed_ref, *rest):
        land_refs, send_sems, recv_sems = rest[:NG], rest[NG + 1], rest[NG + 2]
        token = rest[-1]
        x, y, c, chips = _place()
        targets = [(x, y, 1 - c)] + [(*chip, c) for chip in chips]
        for k, to in enumerate(targets):
            for src, dst in _piece_refs(packed_ref, land_refs, 4 * x + 2 * y + c):
                pltpu.make_async_remote_copy(src_ref=src, dst_ref=dst, send_sem=send_sems.at[k], recv_sem=recv_sems.at[k],
                                             device_id=to, device_id_type=MESH).start()
        token[...] = jnp.zeros_like(token)

    hbm = lambda a: pltpu.HBM(a.shape, a.dtype)
    outs = pl.pallas_call(
        body, name=f"gather_start_{l}",
        out_shape=(pltpu.SemaphoreType.DMA((4,)), pltpu.SemaphoreType.DMA((4,)), hbm(packed), *[hbm(a) for a in lands],
                   jax.ShapeDtypeStruct((8, 128), F32)),
        in_specs=(HBM,) * (1 + NG) + (ANY,),
        out_specs=(SEM, SEM) + (HBM,) * (1 + NG) + (pl.BlockSpec(memory_space=pltpu.VMEM),),
        input_output_aliases={i: 2 + i for i in range(1 + NG)},
        compiler_params=pltpu.CompilerParams(has_side_effects=EFFECT),
    )(_in_hbm(packed), *[_in_hbm(a) for a in lands], after)
    return outs[0], outs[1], outs[2], list(outs[3:3 + NG]), outs[-1]


def _gather_wait(l, send_sems, recv_sems, packed, lands, after):
    def body(packed_ref, *rest):
        send_sems, recv_sems = rest[NG], rest[NG + 1]
        x, y, c, _ = _place()
        for k in range(4):
            cp = pltpu.make_async_remote_copy(src_ref=packed_ref, dst_ref=packed_ref, send_sem=send_sems.at[k],
                                              recv_sem=recv_sems.at[k], device_id=(x, y, c), device_id_type=MESH)
            cp.wait_send()
            cp.wait_recv()

    hbm = lambda a: pltpu.HBM(a.shape, a.dtype)
    outs = pl.pallas_call(
        body, name=f"gather_wait_{l}",
        out_shape=(hbm(packed), *[hbm(a) for a in lands]),
        in_specs=(HBM,) * (1 + NG) + (SEM, SEM, ANY), out_specs=(HBM,) * (1 + NG),
        input_output_aliases={i: i for i in range(1 + NG)},
        compiler_params=pltpu.CompilerParams(has_side_effects=EFFECT),
    )(packed, *lands, send_sems, recv_sems, after)
    return outs[0], list(outs[1:])


def _gather_forward(l, packed, lands):
    def body(packed_ref, *rest):
        land_refs, (send_sems, recv_sems) = rest[NG:2 * NG], rest[2 * NG:]
        x, y, c, chips = _place()
        for j, (cx, cy) in enumerate(chips):
            for _, rows in _piece_refs(packed_ref, land_refs, 4 * cx + 2 * cy + c):
                pltpu.make_async_remote_copy(src_ref=rows, dst_ref=rows, send_sem=send_sems.at[j], recv_sem=recv_sems.at[j],
                                             device_id=(x, y, 1 - c), device_id_type=MESH).start()
        for j in range(3):
            cp = pltpu.make_async_remote_copy(src_ref=packed_ref, dst_ref=packed_ref, send_sem=send_sems.at[j],
                                              recv_sem=recv_sems.at[j], device_id=(x, y, c), device_id_type=MESH)
            cp.wait_recv()
            cp.wait_send()

    outs = pl.pallas_call(
        body, name=f"gather_forward_{l}", in_specs=[ANY] * (1 + NG), out_specs=[ANY] * NG,
        out_shape=[jax.ShapeDtypeStruct(a.shape, a.dtype) for a in lands],
        input_output_aliases={1 + i: i for i in range(NG)},
        scratch_shapes=[pltpu.SemaphoreType.DMA((3,)), pltpu.SemaphoreType.DMA((3,))],
    )(packed, *lands)
    return dict(zip(GNAMES, outs))


RELATIONS = ((0, 0, 1), (1, 0, 0), (0, 1, 0), (1, 1, 0), (1, 0, 1), (0, 1, 1), (1, 1, 1))


def _peers():
    x, y, c = lax.axis_index("x"), lax.axis_index("y"), lax.axis_index("c")
    return [((1 - x) if fx else x, (1 - y) if fy else y, (1 - c) if fc else c) for fx, fy, fc in RELATIONS]


def _scatter_start(tag, l, slab, recv, parts, after):
    def body(slab_ref, recv_ref, after_ref, send_sems, recv_sems, slab_out, recv_out, token):
        for k, to in enumerate(_peers()):
            for r0, nr in parts:
                pltpu.make_async_remote_copy(src_ref=slab_ref.at[4 * to[0] + 2 * to[1] + to[2], l, pl.ds(r0, nr)],
                                             dst_ref=recv_ref.at[l, k, pl.ds(r0, nr)],
                                             send_sem=send_sems.at[k], recv_sem=recv_sems.at[k],
                                             device_id=to, device_id_type=MESH).start()
        token[...] = jnp.zeros_like(token)

    return pl.pallas_call(
        body, name=f"scatter_start_{tag}",
        out_shape=(pltpu.SemaphoreType.DMA((7,)), pltpu.SemaphoreType.DMA((7,)), pltpu.HBM(slab.shape, slab.dtype),
                   pltpu.HBM(recv.shape, recv.dtype), jax.ShapeDtypeStruct((8, 128), F32)),
        in_specs=(HBM, HBM, ANY), out_specs=(SEM, SEM, HBM, HBM, pl.BlockSpec(memory_space=pltpu.VMEM)),
        input_output_aliases={0: 2, 1: 3},
        compiler_params=pltpu.CompilerParams(has_side_effects=EFFECT),
    )(_in_hbm(slab), _in_hbm(recv), after)


def _scatter_wait(tag, l, send_sems, recv_sems, slab, recv, parts):
    total = sum(nr for _, nr in parts)

    def body(slab_ref, recv_ref, send_sems, recv_sems, slab_out, recv_out):
        for k, to in enumerate(_peers()):
            cp = pltpu.make_async_remote_copy(src_ref=slab_ref.at[0, l, pl.ds(0, total)], dst_ref=recv_ref.at[l, k, pl.ds(0, total)],
                                              send_sem=send_sems.at[k], recv_sem=recv_sems.at[k], device_id=to, device_id_type=MESH)
            cp.wait_send()
            cp.wait_recv()

    return pl.pallas_call(
        body, name=f"scatter_wait_{tag}",
        out_shape=(pltpu.HBM(slab.shape, slab.dtype), pltpu.HBM(recv.shape, recv.dtype)),
        in_specs=(HBM, HBM, SEM, SEM), out_specs=(HBM, HBM), input_output_aliases={0: 0, 1: 1},
        compiler_params=pltpu.CompilerParams(has_side_effects=EFFECT),
    )(slab, recv, send_sems, recv_sems)


def _gather_rows(name, src):
    dst = jax.ShapeDtypeStruct((NDEV,) + src.shape, src.dtype)
    full = (slice(None), slice(None), slice(None))
    pieces = [(0, lambda shard: (pl.ds(shard, 1), slice(None), slice(None)), full)]
    return _allgather(name, src.reshape((1,) + src.shape), [dst], pieces)[0]


FC = 1408


def _ffn_fwd(h, l, pre_g, post_g, wgu, wd):
    T = h.shape[0]
    R = min(512, T)

    def body(h_ref, pg_ref, qg_ref, wgu_ref, wd_ref, out_ref, n_ref, ab_ref, f_ref):
        hh = h_ref[...]
        xh, _ = _rms(hh)
        n = (xh * pg_ref[...]).astype(BF)
        n_ref[...] = n
        f = jnp.zeros((R, D), F32)
        for ci in range(F // FC):
            a = _dot_nt(n, wgu_ref[pl.ds(ci * FC, FC), :])
            b = _dot_nt(n, wgu_ref[pl.ds(F + ci * FC, FC), :])
            ab_ref[:, pl.ds(ci * FC, FC)] = a.astype(BF)
            ab_ref[:, pl.ds(F + ci * FC, FC)] = b.astype(BF)
            s = (a * _sigmoid(a) * b).astype(BF)
            f = f + _dot(s, wd_ref[pl.ds(ci * FC, FC), :])
        f_ref[...] = f
        fh, _ = _rms(f)
        out_ref[...] = hh + 0.5 * (fh * qg_ref[...])

    tile = lambda n: pl.BlockSpec((R, n), lambda i: (i, 0))
    return pl.pallas_call(
        body, name="ffn_fwd", grid=(T // R,),
        in_specs=[tile(D), _row(l, D), _row(l, D), _full((2 * F, D)), _full((F, D))],
        out_specs=[tile(D), tile(D), tile(2 * F), tile(D)],
        out_shape=[jax.ShapeDtypeStruct((T, D), F32), jax.ShapeDtypeStruct((T, D), BF),
                   jax.ShapeDtypeStruct((T, 2 * F), BF), jax.ShapeDtypeStruct((T, D), F32)],
        compiler_params=_cparams(("parallel",)),
    )(h, pre_g, post_g, wgu, wd)


def _ffn_bwd(dout, h, ab, f, l, pre_g, post_g, wgu, wd):
    T = h.shape[0]
    R = min(256, T)

    def body(do_ref, h_ref, ab_ref, f_ref, pg_ref, qg_ref, wgu_ref, wd_ref,
             dh_ref, dab_ref, s_ref, df_ref, dpg_ref, dqg_ref):
        i = pl.program_id(0)

        @pl.when(i == 0)
        def _():
            dpg_ref[...] = jnp.zeros_like(dpg_ref)
            dqg_ref[...] = jnp.zeros_like(dqg_ref)

        do = do_ref[...]
        fh, fr = _rms(f_ref[...])
        df, dq = _rms_bwd(fh, fr, qg_ref[...], 0.5 * do)
        dqg_ref[...] += dq
        df = df.astype(BF)
        df_ref[...] = df
        dn = jnp.zeros((R, D), F32)
        for ci in range(F // FC):
            ga, gb = pl.ds(ci * FC, FC), pl.ds(F + ci * FC, FC)
            ds = _dot_nt(df, wd_ref[ga, :])
            a = ab_ref[:, ga].astype(F32)
            b = ab_ref[:, gb].astype(F32)
            sg = _sigmoid(a)
            sil = a * sg
            s_ref[:, ga] = (sil * b).astype(BF)
            da = (ds * b * (sg * (1.0 + a * (1.0 - sg)))).astype(BF)
            db = (ds * sil).astype(BF)
            dab_ref[:, ga] = da
            dab_ref[:, gb] = db
            dn = dn + _dot(da, wgu_ref[ga, :]) + _dot(db, wgu_ref[gb, :])
        xh, xr = _rms(h_ref[...])
        dx, dp = _rms_bwd(xh, xr, pg_ref[...], dn)
        dpg_ref[...] += dp
        dh_ref[...] = do + dx

    tile = lambda n: pl.BlockSpec((R, n), lambda i: (i, 0))
    acc = pl.BlockSpec((1, D), lambda i: (0, 0))
    return pl.pallas_call(
        body, name="ffn_bwd", grid=(T // R,),
        in_specs=[tile(D), tile(D), tile(2 * F), tile(D), _row(l, D), _row(l, D), _full((2 * F, D)), _full((F, D))],
        out_specs=[tile(D), tile(2 * F), tile(F), tile(D), acc, acc],
        out_shape=[jax.ShapeDtypeStruct((T, D), F32), jax.ShapeDtypeStruct((T, 2 * F), BF),
                   jax.ShapeDtypeStruct((T, F), BF), jax.ShapeDtypeStruct((T, D), BF),
                   jax.ShapeDtypeStruct((1, D), F32), jax.ShapeDtypeStruct((1, D), F32)],
        compiler_params=_cparams(("arbitrary",)),
    )(dout, h, ab, f, pre_g, post_g, wgu, wd)


def _mix_in_fwd(h, l, pre_g, win):
    T = h.shape[0]
    R = min(256, T)

    def body(h_ref, pg_ref, w_ref, n_ref, z_ref):
        xh, _ = _rms(h_ref[...])
        n = (xh * pg_ref[...]).astype(BF)
        n_ref[...] = n
        for ci in range(C // FC):
            z_ref[:, pl.ds(ci * FC, FC)] = _dot_nt(n, w_ref[pl.ds(ci * FC, FC), :])

    tile = lambda n: pl.BlockSpec((R, n), lambda i: (i, 0))
    return pl.pallas_call(
        body, name="mix_in_fwd", grid=(T // R,),
        in_specs=[tile(D), _row(l, D), _full((C, D))],
        out_specs=[tile(D), tile(C)],
        out_shape=[jax.ShapeDtypeStruct((T, D), BF), jax.ShapeDtypeStruct((T, C), F32)],
        compiler_params=_cparams(("parallel",)),
    )(h, pre_g, win)


def _mix_in_bwd(dout, dz, h, l, pre_g, win):
    T = h.shape[0]
    R = min(512, T)

    def body(do_ref, dz_ref, h_ref, pg_ref, w_ref, dh_ref, dpg_ref):
        @pl.when(pl.program_id(0) == 0)
        def _():
            dpg_ref[...] = jnp.zeros_like(dpg_ref)

        dn = _dot(dz_ref[...], w_ref[...])
        xh, xr = _rms(h_ref[...])
        dx, dp = _rms_bwd(xh, xr, pg_ref[...], dn)
        dpg_ref[...] += dp
        dh_ref[...] = do_ref[...] + dx

    tile = lambda n: pl.BlockSpec((R, n), lambda i: (i, 0))
    return pl.pallas_call(
        body, name="mix_in_bwd", grid=(T // R,),
        in_specs=[tile(D), tile(C), tile(D), _row(l, D), _full((C, D))],
        out_specs=[tile(D), pl.BlockSpec((1, D), lambda i: (0, 0))],
        out_shape=[jax.ShapeDtypeStruct((T, D), F32), jax.ShapeDtypeStruct((1, D), F32)],
        compiler_params=_cparams(("arbitrary",)),
    )(dout, dz, h, pre_g, win)


def _mix_specs(l, R, tile_of):
    def halo(rows, col_block):
        per = R // rows
        return pl.BlockSpec((rows, PW), lambda i: (jnp.maximum(tile_of(i) * per - 1, 0), col_block))
    return [
        pl.BlockSpec((R, C), lambda i: (tile_of(i), 0)),
        halo(POOL_HALO, 0), halo(CONV_HALO, 3), halo(CONV_HALO, 4),
        pl.BlockSpec((R, D), lambda i: (tile_of(i), 0)),
        _whole(l, (4, CHUNK, CHUNK)), _row(l, PW),
        _row(l, PW), _row(l, PW), _whole(l, (4, CHUNK, CHUNK)), _whole(l, (CHUNK, 4)),
        _whole(l, (CONV_HALO, PW)), _row(l, PW), _row(l, PW), _row(l, PW),
        _row(l, D),
        _full((D, PW)), _full((D, PW)), _full((D, PW)), _full((D, D)),
    ]


SUBLANES = 8


def _phase_copies(buf, shifted, rows):
    for b in range(1, SUBLANES):
        shifted[b - 1] = buf[pl.ds(b, rows), :]


def _window(buf, shifted, off, R):
    a, b = divmod(off, SUBLANES)
    return buf[pl.ds(off, R), :] if b == 0 else shifted[b - 1, pl.ds(SUBLANES * a, R), :]


class _MixFwd:
    def __init__(self, R, tile, refs, scratch):
        (z_ref, zph_ref, zah_ref, zbh_ref, _h, pw_ref, ps_ref, lg_ref, lb_ref, ws_ref, bst_ref,
         ck_ref, cb_ref, cg_ref, cbb_ref, _qg, wpo_ref, wso_ref, wco_ref, wout_ref) = refs
        pbuf, xbuf, sbuf, xsh, pm_ref, sg_ref, cv_ref = scratch
        first = tile == 0
        tglob = tile * R + lax.broadcasted_iota(jnp.int32, (R, 1), 0)
        pbuf[pl.ds(0, POOL_HALO), :] = jnp.where(first, 0.0, zph_ref[...])
        pbuf[pl.ds(POOL_HALO, R), :] = z_ref[:, pl.ds(0, PW)]
        self.pooled, self.yg, self.cnt = [], [], []
        for gi, w in enumerate(POOL_WINDOWS):
            cols = pl.ds(gi * CHUNK, CHUNK)
            x = pbuf[pl.ds(POOL_HALO, R), cols]
            acc = x
            for j in range(1, w):
                acc = acc + pbuf[pl.ds(POOL_HALO - j, R), cols]
            cnt = jnp.minimum(tglob + 1, w).astype(F32)
            pooled = (acc / cnt - x).astype(BF)
            yg = _dot(pooled, pw_ref[gi].astype(BF))
            pm_ref[:, cols] = (yg * ps_ref[:, cols]).astype(BF)
            self.pooled.append(pooled)
            self.yg.append(yg)
            self.cnt.append(cnt)
        self.zu, self.zv = z_ref[:, pl.ds(OFF_U, PW)], z_ref[:, pl.ds(OFF_V, PW)]
        self.eu, self.ev = lax.erf(self.zu * INV_SQRT2), lax.erf(self.zv * INV_SQRT2)
        self.u = 0.5 * self.zu * (1.0 + self.eu)
        self.vh, self.vr = _ln(0.5 * self.zv * (1.0 + self.ev))
        self.vln = (self.vh * lg_ref[...] + lb_ref[...]).astype(BF)
        tt = lax.broadcasted_iota(jnp.int32, (CHUNK, CHUNK), 0)
        ss = lax.broadcasted_iota(jnp.int32, (CHUNK, CHUNK), 1)
        self.causal = tt >= ss
        self.wc = [jnp.where(self.causal, ws_ref[hd], 0.0).astype(BF) for hd in range(4)]
        for ck in range(R // CHUNK):
            for hd in range(4):
                rows, cols = pl.ds(ck * CHUNK, CHUNK), pl.ds(hd * CHUNK, CHUNK)
                blk = self.vln[ck * CHUNK:(ck + 1) * CHUNK, hd * CHUNK:(hd + 1) * CHUNK]
                sbuf[rows, cols] = _dot(self.wc[hd], blk) + bst_ref[:, pl.ds(hd, 1)]
        self.s = sbuf[...]
        sg_ref[...] = (self.u * self.s).astype(BF)
        self.za = z_ref[:, pl.ds(OFF_A, PW)]
        self.sgb = _sigmoid(z_ref[:, pl.ds(OFF_B, PW)])
        xbuf[pl.ds(0, CONV_HALO), :] = jnp.where(first, 0.0, zah_ref[...] * _sigmoid(zbh_ref[...]))
        xbuf[pl.ds(CONV_HALO, R), :] = self.za * self.sgb
        _phase_copies(xbuf, xsh, R + CONV_HALO - SUBLANES)
        y = jnp.zeros((R, PW), F32) + cb_ref[...]
        for k in range(CONV_K):
            y = y + _window(xbuf, xsh, CONV_HALO - (CONV_K - 1) + k, R) * ck_ref[pl.ds(k, 1), :]
        self.yh, self.yr = _ln(y)
        self.yl = self.yh * cg_ref[...] + cbb_ref[...]
        self.sy = _sigmoid(self.yl)
        cv_ref[...] = (self.yl * self.sy).astype(BF)
        self.g = [_sigmoid(z_ref[:, pl.ds(OFF_G + j * D, D)]) for j in range(3)]
        self.y = [_dot_nt(pm_ref[...], wpo_ref[...]), _dot_nt(sg_ref[...], wso_ref[...]), _dot_nt(cv_ref[...], wco_ref[...])]
        self.merged = (self.g[0] * self.y[0] + self.g[1] * self.y[1] + self.g[2] * self.y[2]).astype(BF)
        self.o = _dot(self.merged, wout_ref[...])


def _phase_scratch(R):
    return pltpu.VMEM((SUBLANES - 1, R + CONV_HALO - SUBLANES, PW), F32)


def _mix_scratch(R):
    return [pltpu.VMEM((R + POOL_HALO, PW), F32), pltpu.VMEM((R + CONV_HALO, PW), F32), pltpu.VMEM((R, PW), F32),
            _phase_scratch(R)]


def _mix_core_fwd(z, h, l, sm, gw):
    T = h.shape[0]
    R = min(256, T)

    def body(*refs):
        ins, out_ref, scratch = refs[:20], refs[20], refs[21:]
        fw = _MixFwd(R, pl.program_id(0), ins, scratch)
        oh, _ = _rms(fw.o)
        out_ref[...] = ins[4][...] + oh * ins[15][...]

    act = pltpu.VMEM((R, PW), BF)
    return pl.pallas_call(
        body, name="mix_core_fwd", grid=(T // R,),
        in_specs=_mix_specs(l, R, lambda i: i),
        out_specs=pl.BlockSpec((R, D), lambda i: (i, 0)),
        out_shape=jax.ShapeDtypeStruct((T, D), F32),
        scratch_shapes=_mix_scratch(R) + [act, act, act],
        compiler_params=_cparams(("arbitrary",)),
    )(z, z, z, z, h, sm["pool_w"], sm["pool_scale"], sm["sgu_ln_g"], sm["sgu_ln_b"], sm["sgu_w_s"], sm["sgu_b_sT"],
      sm["conv_k"], sm["conv_dw_b"], sm["conv_ln_g"], sm["conv_ln_b"], sm["mix_post_g"],
      gw["po"], gw["so"], gw["co"], gw["wo"])


MIX_SMALL_GRADS = (("pool_w", (4, CHUNK, CHUNK)), ("pool_scale", (1, PW)), ("sgu_ln_g", (1, PW)), ("sgu_ln_b", (1, PW)),
                   ("sgu_w_s", (4, CHUNK, CHUNK)), ("sgu_b_sT", (CHUNK, 4)), ("conv_k", (CONV_HALO, PW)),
                   ("conv_dw_b", (1, PW)), ("conv_ln_g", (1, PW)), ("conv_ln_b", (1, PW)), ("mix_post_g", (1, D)))


def _mix_core_bwd(dout, z, h, l, sm, gw):
    T = h.shape[0]
    R = min(128, T)
    nt = T // R
    tile_of = lambda i: nt - 1 - i

    def body(*refs):
        do_ref, ins = refs[0], refs[1:21]
        (dz_ref, mg_ref, dob_ref, dy0_ref, dy1_ref, dy2_ref, pm_ref, sg_ref, cv_ref,
         dpw_ref, dps_ref, dlg_ref, dlb_ref, dws_ref, dbs_ref, dck_ref, dcb_ref, dcg_ref, dcbb_ref, dqg_ref) = refs[21:41]
        pbuf, xbuf, sbuf, xsh, qbuf, dybuf, dvbuf, dysh = refs[41:]
        (_z, _zp, _za, _zb, h_ref, pw_ref, ps_ref, lg_ref, lb_ref, ws_ref, bst_ref,
         ck_ref, cb_ref, cg_ref, cbb_ref, qg_ref, wpo_ref, wso_ref, wco_ref, wout_ref) = ins
        i = pl.program_id(0)
        small = (dpw_ref, dps_ref, dlg_ref, dlb_ref, dws_ref, dbs_ref, dck_ref, dcb_ref, dcg_ref, dcbb_ref, dqg_ref)

        @pl.when(i == 0)
        def _():
            for r in small:
                r[...] = jnp.zeros_like(r)
            qbuf[pl.ds(R, POOL_HALO), :] = jnp.zeros((POOL_HALO, PW), F32)
            dybuf[pl.ds(R, CONV_HALO), :] = jnp.zeros((CONV_HALO, PW), F32)

        fw = _MixFwd(R, tile_of(i), ins, (pbuf, xbuf, sbuf, xsh, pm_ref, sg_ref, cv_ref))
        mg_ref[...] = fw.merged
        oh, orr = _rms(fw.o)
        do, dq = _rms_bwd(oh, orr, qg_ref[...], do_ref[...])
        dqg_ref[...] += dq
        do = do.astype(BF)
        dob_ref[...] = do
        dm = _dot_nt(do, wout_ref[...])
        dys = []
        for j, dyj_ref in enumerate((dy0_ref, dy1_ref, dy2_ref)):
            g = fw.g[j]
            dz_ref[:, pl.ds(OFF_G + j * D, D)] = (dm * fw.y[j] * g * (1.0 - g)).astype(BF)
            dyj = (dm * g).astype(BF)
            dyj_ref[...] = dyj
            dys.append(dyj)
        dpm = _dot(dys[0], wpo_ref[...])
        dsg = _dot(dys[1], wso_ref[...])
        dcv = _dot(dys[2], wco_ref[...])
        for gi, w in enumerate(POOL_WINDOWS):
            cols = pl.ds(gi * CHUNK, CHUNK)
            dpm_g = dpm[:, gi * CHUNK:(gi + 1) * CHUNK]
            dps_ref[:, cols] += _colsum(dpm_g * fw.yg[gi])
            dyg = (dpm_g * ps_ref[:, cols]).astype(BF)
            dpw_ref[gi] += _dot_tn(fw.pooled[gi], dyg)
            dpooled = _dot_nt(dyg, pw_ref[gi].astype(BF))
            qbuf[pl.ds(0, R), cols] = dpooled / fw.cnt[gi]
            acc = -dpooled
            for j in range(w):
                acc = acc + qbuf[pl.ds(j, R), cols]
            dz_ref[:, cols] = acc.astype(BF)
        qbuf[pl.ds(R, POOL_HALO), :] = qbuf[pl.ds(0, POOL_HALO), :]
        ds = dsg * fw.u
        du = dsg * fw.s
        for ck in range(R // CHUNK):
            for hd in range(4):
                rows, cols = pl.ds(ck * CHUNK, CHUNK), pl.ds(hd * CHUNK, CHUNK)
                ds_f = ds[ck * CHUNK:(ck + 1) * CHUNK, hd * CHUNK:(hd + 1) * CHUNK]
                ds_blk = ds_f.astype(BF)
                v_blk = fw.vln[ck * CHUNK:(ck + 1) * CHUNK, hd * CHUNK:(hd + 1) * CHUNK]
                dbs_ref[:, pl.ds(hd, 1)] += jnp.sum(ds_f, axis=1, keepdims=True)
                dws_ref[hd] += jnp.where(fw.causal, _dot_nt(ds_blk, v_blk), 0.0)
                dvbuf[rows, cols] = _dot_tn(fw.wc[hd], ds_blk)
        dgv, dg, db = _ln_bwd(fw.vh, fw.vr, lg_ref[...], dvbuf[...])
        dlg_ref[...] += dg
        dlb_ref[...] += db
        gelu_grad = lambda x, e: 0.5 * (1.0 + e) + x * jnp.exp(-0.5 * x * x) * INV_SQRT_2PI
        dz_ref[:, pl.ds(OFF_V, PW)] = (dgv * gelu_grad(fw.zv, fw.ev)).astype(BF)
        dz_ref[:, pl.ds(OFF_U, PW)] = (du * gelu_grad(fw.zu, fw.eu)).astype(BF)
        dyl = dcv * (fw.sy * (1.0 + fw.yl * (1.0 - fw.sy)))
        dy, dg, db = _ln_bwd(fw.yh, fw.yr, cg_ref[...], dyl)
        dcg_ref[...] += dg
        dcbb_ref[...] += db
        dcb_ref[...] += _colsum(dy)
        dybuf[pl.ds(0, R), :] = dy
        _phase_copies(dybuf, dysh, R + CONV_HALO - SUBLANES)
        dxg = jnp.zeros((R, PW), F32)
        for k in range(CONV_K):
            dck_ref[pl.ds(k, 1), :] += _colsum(dy * _window(xbuf, xsh, CONV_HALO - (CONV_K - 1) + k, R))
            dxg = dxg + _window(dybuf, dysh, CONV_K - 1 - k, R) * ck_ref[pl.ds(k, 1), :]
        dybuf[pl.ds(R, CONV_HALO), :] = dybuf[pl.ds(0, CONV_HALO), :]
        dz_ref[:, pl.ds(OFF_A, PW)] = (dxg * fw.sgb).astype(BF)
        dz_ref[:, pl.ds(OFF_B, PW)] = (dxg * fw.za * fw.sgb * (1.0 - fw.sgb)).astype(BF)

    tile = lambda n: pl.BlockSpec((R, n), lambda i: (tile_of(i), 0))
    small_specs = [pl.BlockSpec(shape, lambda i, nd=len(shape): (0,) * nd) for _, shape in MIX_SMALL_GRADS]
    outs = pl.pallas_call(
        body, name="mix_core_bwd", grid=(nt,),
        in_specs=[tile(D)] + _mix_specs(l, R, tile_of),
        out_specs=[tile(C), tile(D), tile(D), tile(D), tile(D), tile(D), tile(PW), tile(PW), tile(PW)] + small_specs,
        out_shape=[jax.ShapeDtypeStruct((T, C), BF)] + [jax.ShapeDtypeStruct((T, D), BF)] * 5
        + [jax.ShapeDtypeStruct((T, PW), BF)] * 3 + [jax.ShapeDtypeStruct(shape, F32) for _, shape in MIX_SMALL_GRADS],
        scratch_shapes=_mix_scratch(R) + [pltpu.VMEM((R + POOL_HALO, PW), F32), pltpu.VMEM((R + CONV_HALO, PW), F32),
                                          pltpu.VMEM((R, PW), F32), _phase_scratch(R)],
        compiler_params=_cparams(("arbitrary",)),
    )(dout, z, z, z, z, h, sm["pool_w"], sm["pool_scale"], sm["sgu_ln_g"], sm["sgu_ln_b"], sm["sgu_w_s"], sm["sgu_b_sT"],
      sm["conv_k"], sm["conv_dw_b"], sm["conv_ln_g"], sm["conv_ln_b"], sm["mix_post_g"],
      gw["po"], gw["so"], gw["co"], gw["wo"])
    return outs[:9], dict(zip([n for n, _ in MIX_SMALL_GRADS], outs[9:]))


def _ple_fwd(h, p, l, pre_g, post_g, wpr, wpg):
    T = h.shape[0]
    R = min(512, T)

    def body(h_ref, p_ref, pg_ref, qg_ref, wpr_ref, wpg_ref, out_ref, n_ref, pb_ref, gp_ref, e_ref):
        hh = h_ref[...]
        xh, _ = _rms(hh)
        n = (xh * pg_ref[...]).astype(BF)
        n_ref[...] = n
        pb = p_ref[...].astype(BF)
        pb_ref[:, pl.ds(0, PLE)] = pb
        pb_ref[:, pl.ds(PLE, PW - PLE)] = jnp.zeros((R, PW - PLE), BF)
        e = _dot_nt(pb, wpr_ref[:, pl.ds(0, PLE)])
        gp = _dot(n, wpg_ref[...])
        gp_ref[...] = gp
        e_ref[...] = e
        qh, _ = _rms(_sigmoid(gp) * e)
        out_ref[...] = hh + qh * qg_ref[...]

    tile = lambda n: pl.BlockSpec((R, n), lambda i: (i, 0))
    return pl.pallas_call(
        body, name="ple_fwd", grid=(T // R,),
        in_specs=[tile(D), pl.BlockSpec((None, None, R, PLE), lambda i: (l, 0, i, 0)), _row(l, D), _row(l, D),
                  _full((D, PW)), _full((D, D))],
        out_specs=[tile(D), tile(D), tile(PW), tile(D), tile(D)],
        out_shape=[jax.ShapeDtypeStruct((T, D), F32), jax.ShapeDtypeStruct((T, D), BF), jax.ShapeDtypeStruct((T, PW), BF),
                   jax.ShapeDtypeStruct((T, D), F32), jax.ShapeDtypeStruct((T, D), F32)],
        compiler_params=_cparams(("parallel",)),
    )(h, p, pre_g, post_g, wpr, wpg)


def _ple_bwd(dout, h, gp, e, l, pre_g, post_g, wpg):
    T = h.shape[0]
    R = min(512, T)

    def body(do_ref, h_ref, gp_ref, e_ref, pg_ref, qg_ref, wpg_ref, dh_ref, de_ref, dgp_ref, dpg_ref, dqg_ref):
        @pl.when(pl.program_id(0) == 0)
        def _():
            dpg_ref[...] = jnp.zeros_like(dpg_ref)
            dqg_ref[...] = jnp.zeros_like(dqg_ref)

        do = do_ref[...]
        g = _sigmoid(gp_ref[...])
        e = e_ref[...]
        qh, qr = _rms(g * e)
        dq, dqg = _rms_bwd(qh, qr, qg_ref[...], do)
        dqg_ref[...] += dqg
        de_ref[...] = (dq * g).astype(BF)
        dgp = (dq * e * g * (1.0 - g)).astype(BF)
        dgp_ref[...] = dgp
        dn = _dot_nt(dgp, wpg_ref[...])
        xh, xr = _rms(h_ref[...])
        dx, dp = _rms_bwd(xh, xr, pg_ref[...], dn)
        dpg_ref[...] += dp
        dh_ref[...] = do + dx

    tile = lambda n: pl.BlockSpec((R, n), lambda i: (i, 0))
    acc = pl.BlockSpec((1, D), lambda i: (0, 0))
    return pl.pallas_call(
        body, name="ple_bwd", grid=(T // R,),
        in_specs=[tile(D), tile(D), tile(D), tile(D), _row(l, D), _row(l, D), _full((D, D))],
        out_specs=[tile(D), tile(D), tile(D), acc, acc],
        out_shape=[jax.ShapeDtypeStruct((T, D), F32), jax.ShapeDtypeStruct((T, D), BF), jax.ShapeDtypeStruct((T, D), BF),
                   jax.ShapeDtypeStruct((1, D), F32), jax.ShapeDtypeStruct((1, D), F32)],
        compiler_params=_cparams(("arbitrary",)),
    )(dout, h, gp, e, pre_g, post_g, wpg)


def _loss_head(y, target):
    T = y.shape[0]
    R = min(512, T)

    def body(y_ref, t_ref, loss_ref, dy_ref):
        @pl.when(pl.program_id(0) == 0)
        def _():
            loss_ref[...] = jnp.zeros_like(loss_ref)

        err = y_ref[...] - t_ref[0]
        dy_ref[...] = err * (1.0 / D)
        loss_ref[...] += 0.5 * jnp.sum(_mean(err * err), axis=0, keepdims=True)

    tile = pl.BlockSpec((R, D), lambda i: (i, 0))
    return pl.pallas_call(
        body, name="loss_head", grid=(T // R,),
        in_specs=[tile, pl.BlockSpec((1, R, D), lambda i: (0, i, 0))],
        out_specs=[pl.BlockSpec((1, 1), lambda i: (0, 0)), tile],
        out_shape=[jax.ShapeDtypeStruct((1, 1), F32), jax.ShapeDtypeStruct((T, D), F32)],
        compiler_params=_cparams(("arbitrary",)),
    )(y, target)


def _wgrad_f(slab, a, b, l, name, a_col0=0):
    off, r = LAYOUT[name][0], LAYOUT[name][1]
    per = FC // r
    nblk = NDEV // per
    a0 = a_col0 // FC

    T = a.shape[0]
    n = b.shape[1]
    kt = min(1024, T)

    def body(a_ref, b_ref, slab_ref, out_ref, acc_ref):
        k = pl.program_id(1)

        @pl.when(k == 0)
        def _():
            acc_ref[...] = jnp.zeros_like(acc_ref)

        acc_ref[...] += _dot_tn(a_ref[...], b_ref[...])

        @pl.when(k == pl.num_programs(1) - 1)
        def _():
            out_ref[...] = acc_ref[...].reshape(per, r, n).astype(out_ref.dtype)

    return pl.pallas_call(
        body, name="wgrad_" + name, grid=(nblk, T // kt),
        in_specs=[pl.BlockSpec((kt, FC), lambda i, k: (k, i + a0)), pl.BlockSpec((kt, n), lambda i, k: (k, 0)),
                  pl.BlockSpec(memory_space=pl.ANY)],
        out_specs=pl.BlockSpec((per, None, r, n), lambda i, k: (i, l, off // r, 0)),
        out_shape=jax.ShapeDtypeStruct(slab.shape, slab.dtype),
        scratch_shapes=[pltpu.VMEM((FC, n), F32)],
        input_output_aliases={2: 0},
        compiler_params=_cparams(("parallel", "arbitrary")),
    )(a, b, slab)


def _wgrad_d(slab, a, b, l, name):
    off, r, col0, width = LAYOUT[name][:4]
    T = a.shape[0]
    n = b.shape[1]
    kt = min(1024, T)

    def body(a_ref, b_ref, slab_ref, out_ref, acc_ref):
        k = pl.program_id(0)

        @pl.when(k == 0)
        def _():
            acc_ref[...] = jnp.zeros_like(acc_ref)

        acc_ref[...] += _dot_tn(a_ref[...], b_ref[...])

        @pl.when(k == pl.num_programs(0) - 1)
        def _():
            out_ref[...] = acc_ref[...].reshape(NDEV, r, n).astype(out_ref.dtype)

    return pl.pallas_call(
        body, name="wgrad_" + name, grid=(T // kt,),
        in_specs=[pl.BlockSpec((kt, D), lambda k: (k, 0)), pl.BlockSpec((kt, n), lambda k: (k, 0)),
                  pl.BlockSpec(memory_space=pl.ANY)],
        out_specs=pl.BlockSpec((NDEV, None, r, n), lambda k: (0, l, off // r, col0 // n)),
        out_shape=jax.ShapeDtypeStruct(slab.shape, slab.dtype),
        scratch_shapes=[pltpu.VMEM((D, n), F32)],
        input_output_aliases={2: 0},
        compiler_params=_cparams(("arbitrary",)),
    )(a, b, slab)


def _adamw(w, g, m, v):
    m = ADAM_B1 * m + (1.0 - ADAM_B1) * g
    v = ADAM_B2 * v + (1.0 - ADAM_B2) * (g * g)
    m_hat = m / (1.0 - ADAM_B1 ** ADAM_STEP)
    v_hat = v / (1.0 - ADAM_B2 ** ADAM_STEP)
    delta = -ADAM_LR * (m_hat / (jnp.sqrt(v_hat) + ADAM_EPS) + ADAM_WD * w)
    return delta, m, v


def _adam_big(name, me, slab, recv, w, m, v):
    off, rows, col0, width, tr = LAYOUT[name]
    L = w.shape[0]
    wshape = w.shape[1:]
    nt = 4 if name == "w_in" else 1
    if tr:
        pblk, wblk = (rows, width // nt), (wshape[0] // nt, wshape[1])
        pmap = lambda t: (off // rows, col0 // (width // nt) + t)
    else:
        pblk, wblk = (rows // nt, width), (wshape[0] // nt, wshape[1])
        pmap = lambda t: (off // (rows // nt) + t, 0)

    def body(me_ref, s_ref, r_ref, w_ref, m_ref, v_ref, g_out, d_out, m_out, v_out):
        g = s_ref[...].astype(F32)
        for k in range(NDEV - 1):
            g = g + r_ref[k].astype(F32)
        if tr:
            g = g.T
        d, mm, vv = _adamw(w_ref[...], g, m_ref[...], v_ref[...])
        g_out[...] = g
        d_out[...] = d
        m_out[...] = mm
        v_out[...] = vv

    wspec = pl.BlockSpec((None,) + wblk, lambda l, t, me: (l, t, 0))
    grid_spec = pltpu.PrefetchScalarGridSpec(
        num_scalar_prefetch=1, grid=(L, nt),
        in_specs=[pl.BlockSpec((None, None) + pblk, lambda l, t, me: (me[0], l) + pmap(t)),
                  pl.BlockSpec((None, NDEV - 1) + pblk, lambda l, t, me: (l, 0) + pmap(t)), wspec, wspec, wspec],
        out_specs=[wspec] * 4)
    return pl.pallas_call(
        body, name="adam_" + name, grid_spec=grid_spec, out_shape=[jax.ShapeDtypeStruct(w.shape, F32)] * 4,
        compiler_params=_cparams(("parallel", "parallel")),
    )(me, slab, recv, w, m, v)


def _adam_small(gall, w, m, v):
    rows = w.shape[0]
    tr = 32

    def body(g_ref, w_ref, m_ref, v_ref, g_out, d_out, m_out, v_out):
        g = g_ref[0]
        for k in range(1, NDEV):
            g = g + g_ref[k]
        d, mm, vv = _adamw(w_ref[...], g, m_ref[...], v_ref[...])
        g_out[...] = g
        d_out[...] = d
        m_out[...] = mm
        v_out[...] = vv

    spec = pl.BlockSpec((tr, D), lambda i: (i, 0))
    return pl.pallas_call(
        body, name="adam_small", grid=(rows // tr,),
        in_specs=[pl.BlockSpec((NDEV, tr, D), lambda i: (0, i, 0)), spec, spec, spec],
        out_specs=[spec] * 4, out_shape=[jax.ShapeDtypeStruct(w.shape, F32)] * 4,
        compiler_params=_cparams(("parallel",)),
    )(gall, w, m, v)


def _adam_plain(g, w, m, v):
    def body(g_ref, w_ref, m_ref, v_ref, d_out, m_out, v_out):
        d, mm, vv = _adamw(w_ref[...], g_ref[...], m_ref[...], v_ref[...])
        d_out[...] = d
        m_out[...] = mm
        v_out[...] = vv

    vm = pl.BlockSpec(memory_space=pltpu.VMEM)
    return pl.pallas_call(
        body, name="adam_conv_k", in_specs=[vm] * 4, out_specs=[vm] * 3,
        out_shape=[jax.ShapeDtypeStruct(w.shape, F32)] * 3,
    )(g, w, m, v)


def _pad_cols(a, n):
    return jnp.pad(a, [(0, 0)] * (a.ndim - 1) + [(0, n - a.shape[-1])])


def _pack_small(d, conv_k_full):
    L = d["pool_w"].shape[0]
    gains = jnp.stack([d[n].reshape(L, D) for n in GAINS], axis=1)
    halves = [_pad_cols(d[n].reshape(L, PW), D) for n in HALVES] + [jnp.zeros((L, D), F32)]
    halves = jnp.stack(halves, axis=1)
    ck = jnp.zeros((L, 16, D), F32) if conv_k_full is None else conv_k_full.reshape(L, 16, D)
    out = jnp.concatenate([gains, halves, d["pool_w"].reshape(L, 64, D), d["sgu_w_s"].reshape(L, 64, D), ck], axis=1)
    return out.reshape(L * SMALL_ROWS, D)


def _unpack_small(a, like):
    L = a.shape[0] // SMALL_ROWS
    a = a.reshape(L, SMALL_ROWS, D)
    out = {}
    for i, n in enumerate(GAINS):
        out[n] = a[:, i, :].reshape(like[n].shape)
    for i, n in enumerate(HALVES):
        out[n] = a[:, 8 + i, :PW].reshape(like[n].shape)
    out["pool_w"] = a[:, 16:80, :].reshape(like["pool_w"].shape)
    out["sgu_w_s"] = a[:, 80:144, :].reshape(like["sgu_w_s"].shape)
    out["conv_k_full"] = a[:, 144:160, :].reshape(L, CONV_HALO, PW)
    return out


WEIGHTS = ("ffn1_pre_g", "ffn1_w_gate", "ffn1_w_up", "ffn1_w_down", "ffn1_post_g", "mix_pre_g", "w_in", "pool_w", "pool_scale",
           "w_pool_out", "sgu_ln_g", "sgu_ln_b", "sgu_w_s", "sgu_b_s", "w_sgu_out", "conv_dw_k", "conv_dw_b", "conv_ln_g",
           "conv_ln_b", "w_conv_out", "w_out", "mix_post_g", "ffn2_pre_g", "ffn2_w_gate", "ffn2_w_up", "ffn2_w_down",
           "ffn2_post_g", "ple_w_proj", "ple_pre_g", "ple_w_gate", "ple_post_g")
SMALL = GAINS + HALVES + ("pool_w", "sgu_w_s")
FFN1_ROWS = ((704, 1056),)
FFN1_ROWS_OUT = ((0, 704), (1760, ROWS - 1760))


class _Comm:
    def __init__(self, w, me):
        self.w, self.me = w, me

    def gather_start(self, l, after):
        packed, lands = _prep(self.w, l, self.me, [lax.empty(GATHERED[n], BF) for n in GNAMES])
        send_sems, recv_sems, packed, lands, token = _gather_start(l, packed, lands, packed if after is None else after)
        return (send_sems, recv_sems, packed, lands), token[0, 0]

    def gather_finish(self, l, state, after):
        send_sems, recv_sems, packed, lands = state
        packed, lands = _gather_wait(l, send_sems, recv_sems, packed, lands, packed if after is None else after)
        return _gather_forward(l, packed, lands)

    def scatter_start(self, tag, l, slab, recv, parts, after):
        after = jnp.zeros((8, 128), F32) if after is None else after
        send_sems, recv_sems, slab, recv, token = _scatter_start(tag, l, slab, recv, parts, after)
        return (tag, l, send_sems, recv_sems, parts), slab, recv, token[0, 0]

    def gather_small(self, name, rows):
        return _gather_rows("allgather_" + name, rows)

    def scatter_finish(self, state, slab, recv):
        tag, l, send_sems, recv_sems, parts = state
        return _scatter_wait(tag, l, send_sems, recv_sems, slab, recv, parts)


def _fwd_bwd(x, p, target, w, conv_k, comm):
    L = w["w_in"].shape[0]
    T = x.shape[1]
    row = lambda a: a.reshape(L, 1, a.shape[-1])
    sm = {n: row(w[n]) for n in GAINS + ("pool_scale", "sgu_ln_g", "sgu_ln_b", "conv_dw_b", "conv_ln_g", "conv_ln_b")}
    sm.update(pool_w=w["pool_w"], sgu_w_s=w["sgu_w_s"], sgu_b_sT=w["sgu_b_s"].transpose(0, 2, 1), conv_k=conv_k)

    h = x[0]
    saved, gws = [], [None] * L
    state, _ = comm.gather_start(0, conv_k)
    gws[0] = comm.gather_finish(0, state, None)
    for l in range(L):
        gw = gws[l]
        pre1 = sm["ffn1_pre_g"]
        if l + 1 < L:
            state, token = comm.gather_start(l + 1, gw["pg"])
            pre1 = pre1 + token
        s = {"h0": h}
        h, s["n1"], s["ab1"], s["f1"] = _ffn_fwd(h, l, pre1, sm["ffn1_post_g"], gw["gu1"], gw["d1"])
        s["h1"] = h
        s["nm"], s["z"] = _mix_in_fwd(h, l, sm["mix_pre_g"], gw["win"])
        h = _mix_core_fwd(s["z"], h, l, sm, gw)
        s["h2"] = h
        h, s["n2"], s["ab2"], s["f2"] = _ffn_fwd(h, l, sm["ffn2_pre_g"], sm["ffn2_post_g"], gw["gu2"], gw["d2"])
        s["h3"] = h
        h, s["np"], s["pb"], s["gp"], s["e"] = _ple_fwd(h, p, l, sm["ple_pre_g"], sm["ple_post_g"], gw["pr"], gw["pg"])
        saved.append(s)
        if l + 1 < L:
            gws[l + 1] = comm.gather_finish(l + 1, state, h)

    loss_part, dh = _loss_head(h, target)

    slab = lax.empty((NDEV, L, ROWS, D), BF)
    recv = lax.empty((L, NDEV - 1, ROWS, D), BF)
    sg = {n: [None] * L for n in SMALL + ("conv_k", "sgu_b_sT")}
    pending, token = [], None
    gsmall = [None] * L

    def small_rows(l, with_ffn1):
        d = {n: sg[n][l][None] for n in SMALL if n != "sgu_b_s" and (with_ffn1 or not n.startswith("ffn1"))}
        d["sgu_b_s"] = sg["sgu_b_sT"][l].T[None]
        if not with_ffn1:
            d["ffn1_pre_g"] = d["ffn1_post_g"] = jnp.zeros((1, D), F32)
        return _pack_small(d, sg["conv_k"][l][None])
    for l in reversed(range(L)):
        s, gw = saved[l], gws[l]
        post = sm["ple_post_g"] if token is None else sm["ple_post_g"] + token
        dh, de, dgp, sg["ple_pre_g"][l], sg["ple_post_g"][l] = _ple_bwd(
            dh, s["h3"], s["gp"], s["e"], l, sm["ple_pre_g"], post, gw["pg"])
        slab = _wgrad_d(slab, s["np"], dgp, l, "ple_w_gate")
        slab = _wgrad_d(slab, de, s["pb"], l, "ple_w_proj")

        dh, dab, ss, df, sg["ffn2_pre_g"][l], sg["ffn2_post_g"][l] = _ffn_bwd(
            dh, s["h2"], s["ab2"], s["f2"], l, sm["ffn2_pre_g"], sm["ffn2_post_g"], gw["gu2"], gw["d2"])
        slab = _wgrad_f(slab, dab, s["n2"], l, "ffn2_w_gate")
        slab = _wgrad_f(slab, dab, s["n2"], l, "ffn2_w_up", a_col0=F)
        slab = _wgrad_f(slab, ss, df, l, "ffn2_w_down")

        (dz, mg, dob, dy0, dy1, dy2, pm, sgv, cv), g_mix = _mix_core_bwd(dh, s["z"], s["h1"], l, sm, gw)
        for n in g_mix:
            sg[n][l] = g_mix[n]
        dh, sg["mix_pre_g"][l] = _mix_in_bwd(dh, dz, s["h1"], l, sm["mix_pre_g"], gw["win"])
        slab = _wgrad_f(slab, dz, s["nm"], l, "w_in")
        slab = _wgrad_d(slab, mg, dob, l, "w_out")
        slab = _wgrad_d(slab, dy0, pm, l, "w_pool_out")
        slab = _wgrad_d(slab, dy1, sgv, l, "w_sgu_out")
        slab = _wgrad_d(slab, dy2, cv, l, "w_conv_out")

        pre1, parts, tag = sm["ffn1_pre_g"], ((0, ROWS),), str(l)
        if l == 0:
            for st in pending:
                slab, recv = comm.scatter_finish(st, slab, recv)
            gsmall[0] = comm.gather_small("small_grads_0", small_rows(0, False))
            st, slab, recv, token = comm.scatter_start("0a", 0, slab, recv, FFN1_ROWS_OUT, gsmall[0])
            pending, pre1, parts, tag = [st], pre1 + token, FFN1_ROWS, "0b"
        dh, dab, ss, df, sg["ffn1_pre_g"][l], sg["ffn1_post_g"][l] = _ffn_bwd(
            dh, s["h0"], s["ab1"], s["f1"], l, pre1, sm["ffn1_post_g"], gw["gu1"], gw["d1"])
        slab = _wgrad_f(slab, dab, s["n1"], l, "ffn1_w_gate")
        slab = _wgrad_f(slab, dab, s["n1"], l, "ffn1_w_up", a_col0=F)
        slab = _wgrad_f(slab, ss, df, l, "ffn1_w_down")
        after = None
        if l > 0:
            for st in pending:
                slab, recv = comm.scatter_finish(st, slab, recv)
            pending = []
            after = gsmall[l] = comm.gather_small(f"small_grads_{l}", small_rows(l, True))
        st, slab, recv, token = comm.scatter_start(tag, l, slab, recv, parts, after)
        pending.append(st)
    last = jnp.pad(sg["ffn1_pre_g"][0], ((0, 7), (0, 0))) + jnp.pad(sg["ffn1_post_g"][0], ((1, 6), (0, 0))) + token
    last = comm.gather_small("small_grads_last", last)
    for st in pending:
        slab, recv = comm.scatter_finish(st, slab, recv)
    gall = jnp.concatenate(gsmall, axis=1).at[:, 0:2, :].set(last[:, 0:2, :])
    return loss_part, dh.reshape(1, T, D), slab, recv, sg, gall


def _step(x, p, target, w, m, v):
    L = w["w_in"].shape[0]
    ix, iy, ic = lax.axis_index("x"), lax.axis_index("y"), lax.axis_index("c")
    me = 4 * ix + 2 * iy + ic

    ck_local = jnp.pad(w["conv_dw_k"].reshape(L, CONV_K, 64), ((0, 0), (0, 1), (0, 0))).reshape(L * 2, D)
    ck_all = _gather_rows("allgather_conv_k", _pad_rows8(ck_local))[:, :L * 2]
    conv_k = ck_all.reshape(NDEV, L, CONV_HALO, 64).transpose(1, 2, 0, 3).reshape(L, CONV_HALO, PW)

    me_arr = me.astype(jnp.int32).reshape(1)
    loss_part, grad_x, slab, recv, _, gall = _fwd_bwd(x, p, target, w, conv_k, _Comm(w, me_arr))
    loss = lax.psum(loss_part[0, 0], ("x", "y", "c"))

    res = {n: _adam_big(n, me_arr, slab, recv, w[n], m[n], v[n]) for n in BIG}

    outs = _adam_small(gall, _pack_small(w, None), _pack_small(m, None), _pack_small(v, None))
    unpacked = [_unpack_small(o, w) for o in outs]
    for n in SMALL:
        res[n] = tuple(u[n] for u in unpacked)

    gk = lax.dynamic_slice_in_dim(unpacked[0]["conv_k_full"][:, :CONV_K, :], me * 64, 64, axis=2)
    shp = w["conv_dw_k"].shape
    flat = lambda a: a.reshape(L * CONV_K, 64)
    dk, mk, vk = _adam_plain(flat(gk), flat(w["conv_dw_k"]), flat(m["conv_dw_k"]), flat(v["conv_dw_k"]))
    res["conv_dw_k"] = (gk.reshape(shp), dk.reshape(shp), mk.reshape(shp), vk.reshape(shp))

    return (loss, grad_x, *[res[n][0] for n in WEIGHTS], *[res[n][1] for n in WEIGHTS],
            *[res[n][2] for n in WEIGHTS], *[res[n][3] for n in WEIGHTS])


def _pad_rows8(a):
    return jnp.pad(a, ((0, (-a.shape[0]) % 8), (0, 0)))


def kernel(x, p, ffn1_pre_g, ffn1_w_gate, ffn1_w_up, ffn1_w_down, ffn1_post_g, mix_pre_g, w_in, pool_w, pool_scale, w_pool_out, sgu_ln_g, sgu_ln_b, sgu_w_s, sgu_b_s, w_sgu_out, conv_dw_k, conv_dw_b, conv_ln_g, conv_ln_b, w_conv_out, w_out, mix_post_g, ffn2_pre_g, ffn2_w_gate, ffn2_w_up, ffn2_w_down, ffn2_post_g, ple_w_proj, ple_pre_g, ple_w_gate, ple_post_g, loss_target, m_ffn1_pre_g, m_ffn1_w_gate, m_ffn1_w_up, m_ffn1_w_down, m_ffn1_post_g, m_mix_pre_g, m_w_in, m_pool_w, m_pool_scale, m_w_pool_out, m_sgu_ln_g, m_sgu_ln_b, m_sgu_w_s, m_sgu_b_s, m_w_sgu_out, m_conv_dw_k, m_conv_dw_b, m_conv_ln_g, m_conv_ln_b, m_w_conv_out, m_w_out, m_mix_post_g, m_ffn2_pre_g, m_ffn2_w_gate, m_ffn2_w_up, m_ffn2_w_down, m_ffn2_post_g, m_ple_w_proj, m_ple_pre_g, m_ple_w_gate, m_ple_post_g, v_ffn1_pre_g, v_ffn1_w_gate, v_ffn1_w_up, v_ffn1_w_down, v_ffn1_post_g, v_mix_pre_g, v_w_in, v_pool_w, v_pool_scale, v_w_pool_out, v_sgu_ln_g, v_sgu_ln_b, v_sgu_w_s, v_sgu_b_s, v_w_sgu_out, v_conv_dw_k, v_conv_dw_b, v_conv_ln_g, v_conv_ln_b, v_w_conv_out, v_w_out, v_mix_post_g, v_ffn2_pre_g, v_ffn2_w_gate, v_ffn2_w_up, v_ffn2_w_down, v_ffn2_post_g, v_ple_w_proj, v_ple_pre_g, v_ple_w_gate, v_ple_post_g):
    args = dict(locals())
    w = {n: args[n] for n in WEIGHTS}
    m = {n: args["m_" + n] for n in WEIGHTS}
    v = {n: args["v_" + n] for n in WEIGHTS}
    return _step(x, p, loss_target, w, m, v)
```

```python
import functools

import jax
import jax.numpy as jnp
from jax import lax
from jax.experimental import pallas as pl
from jax.experimental.pallas import tpu as pltpu

D = 1024
F = 2816
C = 5632
PW = 512
PLE = 256
NDEV = 8
CHUNK = 128
POOL_WINDOWS = (2, 4, 8, 16)
CONV_K = 31
POOL_HALO = 16
CONV_HALO = 32
EPS = 1e-6
OFF_U, OFF_V, OFF_A, OFF_B, OFF_G = 512, 1024, 1536, 2048, 2560

ADAM_LR, ADAM_B1, ADAM_B2, ADAM_EPS, ADAM_WD, ADAM_STEP = 0.001, 0.9, 0.999, 1e-08, 0.01, 10

BF = jnp.bfloat16
F32 = jnp.float32
VMEM_LIMIT = 56 * 1024 * 1024
MESH = pl.DeviceIdType.MESH
INV_SQRT2 = 0.7071067811865476
INV_SQRT_2PI = 0.3989422804014327

ROWS = 3328
LAYOUT = {
    "w_in": (0, 704, 0, 1024, True),
    "ffn1_w_gate": (704, 352, 0, 1024, True),
    "ffn1_w_up": (1056, 352, 0, 1024, True),
    "ffn1_w_down": (1408, 352, 0, 1024, False),
    "ffn2_w_gate": (1760, 352, 0, 1024, True),
    "ffn2_w_up": (2112, 352, 0, 1024, True),
    "ffn2_w_down": (2464, 352, 0, 1024, False),
    "w_pool_out": (2816, 128, 0, 512, True),
    "w_sgu_out": (2816, 128, 512, 512, True),
    "w_conv_out": (2944, 128, 0, 512, True),
    "ple_w_proj": (2944, 128, 512, 256, True),
    "w_out": (3072, 128, 0, 1024, False),
    "ple_w_gate": (3200, 128, 0, 1024, False),
}
BIG = tuple(LAYOUT)
SWAPPED = ("w_in", "ffn1_w_gate", "ffn1_w_up", "ffn2_w_gate", "ffn2_w_up")
GATHERED = {
    "win": (C, D), "gu1": (2 * F, D), "d1": (F, D), "gu2": (2 * F, D), "d2": (F, D),
    "po": (D, PW), "so": (D, PW), "co": (D, PW), "pr": (D, PW), "wo": (D, D), "pg": (D, D),
}
PIECES = (
    ("win", 0, 0, 704, 0, 1024), ("gu1", 0, 704, 352, 0, 1024), ("gu1", F, 1056, 352, 0, 1024),
    ("d1", 0, 1408, 352, 0, 1024), ("gu2", 0, 1760, 352, 0, 1024), ("gu2", F, 2112, 352, 0, 1024),
    ("d2", 0, 2464, 352, 0, 1024), ("po", 0, 2816, 128, 0, 512), ("so", 0, 2816, 128, 512, 512),
    ("co", 0, 2944, 128, 0, 512), ("pr", 0, 2944, 128, 512, 512), ("wo", 0, 3072, 128, 0, 1024),
    ("pg", 0, 3200, 128, 0, 1024),
)
GAINS = ("ffn1_pre_g", "ffn1_post_g", "mix_pre_g", "mix_post_g", "ffn2_pre_g", "ffn2_post_g", "ple_pre_g", "ple_post_g")
HALVES = ("pool_scale", "sgu_ln_g", "sgu_ln_b", "sgu_b_s", "conv_dw_b", "conv_ln_g", "conv_ln_b")
SMALL_ROWS = 160


def _cparams(sem=None, **kw):
    if sem is not None:
        kw["dimension_semantics"] = sem
    return pltpu.CompilerParams(vmem_limit_bytes=VMEM_LIMIT, **kw)


def _whole(l, shape):
    nd = len(shape)
    return pl.BlockSpec((None,) + tuple(shape), lambda *_: (l,) + (0,) * nd, pipeline_mode=pl.Buffered(1))


def _full(shape):
    nd = len(shape)
    return pl.BlockSpec(tuple(shape), lambda *_: (0,) * nd, pipeline_mode=pl.Buffered(1))


def _row(l, n):
    return pl.BlockSpec((None, 1, n), lambda *_: (l, 0, 0))


def _dot(a, b):
    return jnp.dot(a, b, preferred_element_type=F32)


def _dot_nt(a, b):
    return lax.dot_general(a, b, (((1,), (1,)), ((), ())), preferred_element_type=F32)


def _dot_tn(a, b):
    return lax.dot_general(a, b, (((0,), (0,)), ((), ())), preferred_element_type=F32)


def _mean(x):
    return jnp.mean(x, axis=-1, keepdims=True)


def _colsum(x):
    return jnp.sum(x, axis=0, keepdims=True)


def _rms(x):
    r = lax.rsqrt(_mean(x * x) + EPS)
    return x * r, r


def _rms_bwd(xh, r, g, dy):
    dxh = dy * g
    return r * (dxh - xh * _mean(dxh * xh)), _colsum(dy * xh)


def _ln(x):
    xc = x - _mean(x)
    r = lax.rsqrt(_mean(xc * xc) + EPS)
    return xc * r, r


def _ln_bwd(xh, r, g, dy):
    dxh = dy * g
    return r * (dxh - _mean(dxh) - xh * _mean(dxh * xh)), _colsum(dy * xh), _colsum(dy)


def _sigmoid(x):
    return jax.nn.sigmoid(x)


def _prep(w, l, me, lands):
    nb, npc = len(BIG), len(PIECES)

    def body(me_ref, *refs):
        ins, out = dict(zip(BIG, refs[:nb])), refs[nb + NG]
        land_refs, sems = refs[nb + NG + 1:nb + 2 * NG + 1], refs[-1]
        for name, (off, rows, col0, width, tr) in LAYOUT.items():
            v = ins[name][...]
            if tr and name not in SWAPPED:
                v = v.T
            out[pl.ds(off, rows), pl.ds(col0, width)] = v.astype(BF)
        out[pl.ds(2944, 128), pl.ds(768, 256)] = jnp.zeros((128, 256), BF)
        mine = [pltpu.make_async_copy(src, dst, sems.at[i]) for i, (src, dst) in enumerate(_piece_refs(out, land_refs, me_ref[0]))]
        for cp in mine:
            cp.start()
        for cp in mine:
            cp.wait()

    grid_spec = pltpu.PrefetchScalarGridSpec(
        num_scalar_prefetch=1, grid=(1,),
        in_specs=[pl.BlockSpec((None,) + w[n].shape[1:], lambda i, me: (l, 0, 0)) for n in BIG] + [ANY] * NG,
        out_specs=[pl.BlockSpec((ROWS, D), lambda i, me: (0, 0))] + [ANY] * NG,
        scratch_shapes=[pltpu.SemaphoreType.DMA((npc,))])
    outs = pl.pallas_call(
        body, name="prep", grid_spec=grid_spec,
        out_shape=[jax.ShapeDtypeStruct((ROWS, D), BF)] + [jax.ShapeDtypeStruct(a.shape, a.dtype) for a in lands],
        input_output_aliases={1 + nb + i: 1 + i for i in range(NG)},
        compiler_params=_cparams(("arbitrary",)),
    )(me, *[w[n] for n in BIG], *lands)
    return outs[0], list(outs[1:])


def _place():
    x, y, c = lax.axis_index("x"), lax.axis_index("y"), lax.axis_index("c")
    chips = [(1 - x, y), (x, 1 - y), (1 - x, 1 - y)]
    return x, y, c, chips


def _allgather(name, src, dsts, pieces):
    npc = len(pieces)

    def body(src_ref, *rest):
        outs, (send_sems, recv_sems, local_sems) = rest[:len(dsts)], rest[len(dsts):]
        x, y, c, chips = _place()
        me, sibling = (x, y, c), (x, y, 1 - c)

        def shard_of(dev):
            return 4 * dev[0] + 2 * dev[1] + dev[2]

        def copies(k, block, to, from_src):
            res = []
            for di, dst_fn, src_sl in pieces:
                dst = outs[di].at[dst_fn(shard_of(block))]
                s = src_ref.at[src_sl] if from_src else dst
                res.append(pltpu.make_async_remote_copy(src_ref=s, dst_ref=dst, send_sem=send_sems.at[k],
                                                        recv_sem=recv_sems.at[k], device_id=to, device_id_type=MESH))
            return res

        def whole(k):
            return pltpu.make_async_remote_copy(src_ref=src_ref, dst_ref=src_ref, send_sem=send_sems.at[k],
                                                recv_sem=recv_sems.at[k], device_id=me, device_id_type=MESH)

        mine = [pltpu.make_async_copy(src_ref.at[src_sl], outs[di].at[dst_fn(shard_of(me))], local_sems.at[i])
                for i, (di, dst_fn, src_sl) in enumerate(pieces)]
        for cp in mine:
            cp.start()
        for cp in copies(0, me, sibling, True):
            cp.start()
        for j, chip in enumerate(chips):
            for cp in copies(1 + j, me, (*chip, c), True):
                cp.start()
        for j, chip in enumerate(chips):
            whole(1 + j).wait_recv()
            for cp in copies(4 + j, (*chip, c), sibling, False):
                cp.start()
        for k in (0, 4, 5, 6):
            whole(k).wait_recv()
        for k in range(7):
            whole(k).wait_send()
        for cp in mine:
            cp.wait()

    any_spec = pl.BlockSpec(memory_space=pl.ANY)
    return pl.pallas_call(
        body, name=name, in_specs=[any_spec], out_specs=[any_spec] * len(dsts), out_shape=dsts,
        scratch_shapes=[pltpu.SemaphoreType.DMA((7,)), pltpu.SemaphoreType.DMA((7,)), pltpu.SemaphoreType.DMA((npc,))],
    )(src)


HBM = pl.BlockSpec(memory_space=pltpu.HBM)
SEM = pl.BlockSpec(memory_space=pltpu.SEMAPHORE)
ANY = pl.BlockSpec(memory_space=pl.ANY)
EFFECT = pltpu.SideEffectType.DATAFLOW_SIDE_EFFECTING
GNAMES = tuple(GATHERED)
NG = len(GNAMES)


def _in_hbm(a):
    return pltpu.with_memory_space_constraint(a, pltpu.HBM)


def _piece_refs(packed_ref, land_refs, shard):
    out = []
    for gname, row0, poff, prow, pcol, width in PIECES:
        land = land_refs[GNAMES.index(gname)]
        out.append((packed_ref.at[pl.ds(poff, prow), pl.ds(pcol, width)], land.at[pl.ds(row0 + shard * prow, prow), :]))
    return out


def _gather_start(l, packed, lands, after):
    def body(packed_ref, *rest):
        land_refs, send_sems, recv_sems = rest[:NG], rest[NG + 1], rest[NG + 2]
        token = rest[-1]
        x, y, c, chips = _place()
        targets = [(x, y, 1 - c)] + [(*chip, c) for chip in chips]
        for k, to in enumerate(targets):
            for src, dst in _piece_refs(packed_ref, land_refs, 4 * x + 2 * y + c):
                pltpu.make_async_remote_copy(src_ref=src, dst_ref=dst, send_sem=send_sems.at[k], recv_sem=recv_sems.at[k],
                                             device_id=to, device_id_type=MESH).start()
        token[...] = jnp.zeros_like(token)

    hbm = lambda a: pltpu.HBM(a.shape, a.dtype)
    outs = pl.pallas_call(
        body, name=f"gather_start_{l}",
        out_shape=(pltpu.SemaphoreType.DMA((4,)), pltpu.SemaphoreType.DMA((4,)), hbm(packed), *[hbm(a) for a in lands],
                   jax.ShapeDtypeStruct((8, 128), F32)),
        in_specs=(HBM,) * (1 + NG) + (ANY,),
        out_specs=(SEM, SEM) + (HBM,) * (1 + NG) + (pl.BlockSpec(memory_space=pltpu.VMEM),),
        input_output_aliases={i: 2 + i for i in range(1 + NG)},
        compiler_params=pltpu.CompilerParams(has_side_effects=EFFECT),
    )(_in_hbm(packed), *[_in_hbm(a) for a in lands], after)
    return outs[0], outs[1], outs[2], list(outs[3:3 + NG]), outs[-1]


def _gather_wait(l, send_sems, recv_sems, packed, lands, after):
    def body(packed_ref, *rest):
        send_sems, recv_sems = rest[NG], rest[NG + 1]
        x, y, c, _ = _place()
        for k in range(4):
            cp = pltpu.make_async_remote_copy(src_ref=packed_ref, dst_ref=packed_ref, send_sem=send_sems.at[k],
                                              recv_sem=recv_sems.at[k], device_id=(x, y, c), device_id_type=MESH)
            cp.wait_send()
            cp.wait_recv()

    hbm = lambda a: pltpu.HBM(a.shape, a.dtype)
    outs = pl.pallas_call(
        body, name=f"gather_wait_{l}",
        out_shape=(hbm(packed), *[hbm(a) for a in lands]),
        in_specs=(HBM,) * (1 + NG) + (SEM, SEM, ANY), out_specs=(HBM,) * (1 + NG),
        input_output_aliases={i: i for i in range(1 + NG)},
        compiler_params=pltpu.CompilerParams(has_side_effects=EFFECT),
    )(packed, *lands, send_sems, recv_sems, after)
    return outs[0], list(outs[1:])


def _gather_forward(l, packed, lands):
    def body(packed_ref, *rest):
        land_refs, (send_sems, recv_sems) = rest[NG:2 * NG], rest[2 * NG:]
        x, y, c, chips = _place()
        for j, (cx, cy) in enumerate(chips):
            for _, rows in _piece_refs(packed_ref, land_refs, 4 * cx + 2 * cy + c):
                pltpu.make_async_remote_copy(src_ref=rows, dst_ref=rows, send_sem=send_sems.at[j], recv_sem=recv_sems.at[j],
                                             device_id=(x, y, 1 - c), device_id_type=MESH).start()
        for j in range(3):
            cp = pltpu.make_async_remote_copy(src_ref=packed_ref, dst_ref=packed_ref, send_sem=send_sems.at[j],
                                              recv_sem=recv_sems.at[j], device_id=(x, y, c), device_id_type=MESH)
            cp.wait_recv()
            cp.wait_send()

    outs = pl.pallas_call(
        body, name=f"gather_forward_{l}", in_specs=[ANY] * (1 + NG), out_specs=[ANY] * NG,
        out_shape=[jax.ShapeDtypeStruct(a.shape, a.dtype) for a in lands],
        input_output_aliases={1 + i: i for i in range(NG)},
        scratch_shapes=[pltpu.SemaphoreType.DMA((3,)), pltpu.SemaphoreType.DMA((3,))],
    )(packed, *lands)
    return dict(zip(GNAMES, outs))


RELATIONS = ((0, 0, 1), (1, 0, 0), (0, 1, 0), (1, 1, 0), (1, 0, 1), (0, 1, 1), (1, 1, 1))


def _peers():
    x, y, c = lax.axis_index("x"), lax.axis_index("y"), lax.axis_index("c")
    return [((1 - x) if fx else x, (1 - y) if fy else y, (1 - c) if fc else c) for fx, fy, fc in RELATIONS]


def _scatter_start(tag, l, slab, recv, parts, after):
    def body(slab_ref, recv_ref, after_ref, send_sems, recv_sems, slab_out, recv_out, token):
        for k, to in enumerate(_peers()):
            for r0, nr in parts:
                pltpu.make_async_remote_copy(src_ref=slab_ref.at[4 * to[0] + 2 * to[1] + to[2], l, pl.ds(r0, nr)],
                                             dst_ref=recv_ref.at[l, k, pl.ds(r0, nr)],
                                             send_sem=send_sems.at[k], recv_sem=recv_sems.at[k],
                                             device_id=to, device_id_type=MESH).start()
        token[...] = jnp.zeros_like(token)

    return pl.pallas_call(
        body, name=f"scatter_start_{tag}",
        out_shape=(pltpu.SemaphoreType.DMA((7,)), pltpu.SemaphoreType.DMA((7,)), pltpu.HBM(slab.shape, slab.dtype),
                   pltpu.HBM(recv.shape, recv.dtype), jax.ShapeDtypeStruct((8, 128), F32)),
        in_specs=(HBM, HBM, ANY), out_specs=(SEM, SEM, HBM, HBM, pl.BlockSpec(memory_space=pltpu.VMEM)),
        input_output_aliases={0: 2, 1: 3},
        compiler_params=pltpu.CompilerParams(has_side_effects=EFFECT),
    )(_in_hbm(slab), _in_hbm(recv), after)


def _scatter_wait(tag, l, send_sems, recv_sems, slab, recv, parts):
    total = sum(nr for _, nr in parts)

    def body(slab_ref, recv_ref, send_sems, recv_sems, slab_out, recv_out):
        for k, to in enumerate(_peers()):
            cp = pltpu.make_async_remote_copy(src_ref=slab_ref.at[0, l, pl.ds(0, total)], dst_ref=recv_ref.at[l, k, pl.ds(0, total)],
                                              send_sem=send_sems.at[k], recv_sem=recv_sems.at[k], device_id=to, device_id_type=MESH)
            cp.wait_send()
            cp.wait_recv()

    return pl.pallas_call(
        body, name=f"scatter_wait_{tag}",
        out_shape=(pltpu.HBM(slab.shape, slab.dtype), pltpu.HBM(recv.shape, recv.dtype)),
        in_specs=(HBM, HBM, SEM, SEM), out_specs=(HBM, HBM), input_output_aliases={0: 0, 1: 1},
        compiler_params=pltpu.CompilerParams(has_side_effects=EFFECT),
    )(slab, recv, send_sems, recv_sems)


def _gather_rows(name, src):
    dst = jax.ShapeDtypeStruct((NDEV,) + src.shape, src.dtype)
    full = (slice(None), slice(None), slice(None))
    pieces = [(0, lambda shard: (pl.ds(shard, 1), slice(None), slice(None)), full)]
    return _allgather(name, src.reshape((1,) + src.shape), [dst], pieces)[0]


FC = 1408


def _ffn_fwd(h, l, pre_g, post_g, wgu, wd):
    T = h.shape[0]
    R = min(512, T)

    def body(h_ref, pg_ref, qg_ref, wgu_ref, wd_ref, out_ref, n_ref, ab_ref, f_ref):
        hh = h_ref[...]
        xh, _ = _rms(hh)
        n = (xh * pg_ref[...]).astype(BF)
        n_ref[...] = n
        f = jnp.zeros((R, D), F32)
        for ci in range(F // FC):
            a = _dot_nt(n, wgu_ref[pl.ds(ci * FC, FC), :])
            b = _dot_nt(n, wgu_ref[pl.ds(F + ci * FC, FC), :])
            ab_ref[:, pl.ds(ci * FC, FC)] = a.astype(BF)
            ab_ref[:, pl.ds(F + ci * FC, FC)] = b.astype(BF)
            s = (a * _sigmoid(a) * b).astype(BF)
            f = f + _dot(s, wd_ref[pl.ds(ci * FC, FC), :])
        f_ref[...] = f
        fh, _ = _rms(f)
        out_ref[...] = hh + 0.5 * (fh * qg_ref[...])

    tile = lambda n: pl.BlockSpec((R, n), lambda i: (i, 0))
    return pl.pallas_call(
        body, name="ffn_fwd", grid=(T // R,),
        in_specs=[tile(D), _row(l, D), _row(l, D), _full((2 * F, D)), _full((F, D))],
        out_specs=[tile(D), tile(D), tile(2 * F), tile(D)],
        out_shape=[jax.ShapeDtypeStruct((T, D), F32), jax.ShapeDtypeStruct((T, D), BF),
                   jax.ShapeDtypeStruct((T, 2 * F), BF), jax.ShapeDtypeStruct((T, D), F32)],
        compiler_params=_cparams(("parallel",)),
    )(h, pre_g, post_g, wgu, wd)


def _ffn_bwd(dout, h, ab, f, l, pre_g, post_g, wgu, wd):
    T = h.shape[0]
    R = min(256, T)

    def body(do_ref, h_ref, ab_ref, f_ref, pg_ref, qg_ref, wgu_ref, wd_ref,
             dh_ref, dab_ref, s_ref, df_ref, dpg_ref, dqg_ref):
        i = pl.program_id(0)

        @pl.when(i == 0)
        def _():
            dpg_ref[...] = jnp.zeros_like(dpg_ref)
            dqg_ref[...] = jnp.zeros_like(dqg_ref)

        do = do_ref[...]
        fh, fr = _rms(f_ref[...])
        df, dq = _rms_bwd(fh, fr, qg_ref[...], 0.5 * do)
        dqg_ref[...] += dq
        df = df.astype(BF)
        df_ref[...] = df
        dn = jnp.zeros((R, D), F32)
        for ci in range(F // FC):
            ga, gb = pl.ds(ci * FC, FC), pl.ds(F + ci * FC, FC)
            ds = _dot_nt(df, wd_ref[ga, :])
            a = ab_ref[:, ga].astype(F32)
            b = ab_ref[:, gb].astype(F32)
            sg = _sigmoid(a)
            sil = a * sg
            s_ref[:, ga] = (sil * b).astype(BF)
            da = (ds * b * (sg * (1.0 + a * (1.0 - sg)))).astype(BF)
            db = (ds * sil).astype(BF)
            dab_ref[:, ga] = da
            dab_ref[:, gb] = db
            dn = dn + _dot(da, wgu_ref[ga, :]) + _dot(db, wgu_ref[gb, :])
        xh, xr = _rms(h_ref[...])
        dx, dp = _rms_bwd(xh, xr, pg_ref[...], dn)
        dpg_ref[...] += dp
        dh_ref[...] = do + dx

    tile = lambda n: pl.BlockSpec((R, n), lambda i: (i, 0))
    acc = pl.BlockSpec((1, D), lambda i: (0, 0))
    return pl.pallas_call(
        body, name="ffn_bwd", grid=(T // R,),
        in_specs=[tile(D), tile(D), tile(2 * F), tile(D), _row(l, D), _row(l, D), _full((2 * F, D)), _full((F, D))],
        out_specs=[tile(D), tile(2 * F), tile(F), tile(D), acc, acc],
        out_shape=[jax.ShapeDtypeStruct((T, D), F32), jax.ShapeDtypeStruct((T, 2 * F), BF),
                   jax.ShapeDtypeStruct((T, F), BF), jax.ShapeDtypeStruct((T, D), BF),
                   jax.ShapeDtypeStruct((1, D), F32), jax.ShapeDtypeStruct((1, D), F32)],
        compiler_params=_cparams(("arbitrary",)),
    )(dout, h, ab, f, pre_g, post_g, wgu, wd)


def _mix_in_fwd(h, l, pre_g, win):
    T = h.shape[0]
    R = min(256, T)

    def body(h_ref, pg_ref, w_ref, n_ref, z_ref):
        xh, _ = _rms(h_ref[...])
        n = (xh * pg_ref[...]).astype(BF)
        n_ref[...] = n
        for ci in range(C // FC):
            z_ref[:, pl.ds(ci * FC, FC)] = _dot_nt(n, w_ref[pl.ds(ci * FC, FC), :])

    tile = lambda n: pl.BlockSpec((R, n), lambda i: (i, 0))
    return pl.pallas_call(
        body, name="mix_in_fwd", grid=(T // R,),
        in_specs=[tile(D), _row(l, D), _full((C, D))],
        out_specs=[tile(D), tile(C)],
        out_shape=[jax.ShapeDtypeStruct((T, D), BF), jax.ShapeDtypeStruct((T, C), F32)],
        compiler_params=_cparams(("parallel",)),
    )(h, pre_g, win)


def _mix_in_bwd(dout, dz, h, l, pre_g, win):
    T = h.shape[0]
    R = min(512, T)

    def body(do_ref, dz_ref, h_ref, pg_ref, w_ref, dh_ref, dpg_ref):
        @pl.when(pl.program_id(0) == 0)
        def _():
            dpg_ref[...] = jnp.zeros_like(dpg_ref)

        dn = _dot(dz_ref[...], w_ref[...])
        xh, xr = _rms(h_ref[...])
        dx, dp = _rms_bwd(xh, xr, pg_ref[...], dn)
        dpg_ref[...] += dp
        dh_ref[...] = do_ref[...] + dx

    tile = lambda n: pl.BlockSpec((R, n), lambda i: (i, 0))
    return pl.pallas_call(
        body, name="mix_in_bwd", grid=(T // R,),
        in_specs=[tile(D), tile(C), tile(D), _row(l, D), _full((C, D))],
        out_specs=[tile(D), pl.BlockSpec((1, D), lambda i: (0, 0))],
        out_shape=[jax.ShapeDtypeStruct((T, D), F32), jax.ShapeDtypeStruct((1, D), F32)],
        compiler_params=_cparams(("arbitrary",)),
    )(dout, dz, h, pre_g, win)


def _mix_specs(l, R, tile_of):
    def halo(rows, col_block):
        per = R // rows
        return pl.BlockSpec((rows, PW), lambda i: (jnp.maximum(tile_of(i) * per - 1, 0), col_block))
    return [
        pl.BlockSpec((R, C), lambda i: (tile_of(i), 0)),
        halo(POOL_HALO, 0), halo(CONV_HALO, 3), halo(CONV_HALO, 4),
        pl.BlockSpec((R, D), lambda i: (tile_of(i), 0)),
        _whole(l, (4, CHUNK, CHUNK)), _row(l, PW),
        _row(l, PW), _row(l, PW), _whole(l, (4, CHUNK, CHUNK)), _whole(l, (CHUNK, 4)),
        _whole(l, (CONV_HALO, PW)), _row(l, PW), _row(l, PW), _row(l, PW),
        _row(l, D),
        _full((D, PW)), _full((D, PW)), _full((D, PW)), _full((D, D)),
    ]


SUBLANES = 8


def _phase_copies(buf, shifted, rows):
    for b in range(1, SUBLANES):
        shifted[b - 1] = buf[pl.ds(b, rows), :]


def _window(buf, shifted, off, R):
    a, b = divmod(off, SUBLANES)
    return buf[pl.ds(off, R), :] if b == 0 else shifted[b - 1, pl.ds(SUBLANES * a, R), :]


class _MixFwd:
    def __init__(self, R, tile, refs, scratch):
        (z_ref, zph_ref, zah_ref, zbh_ref, _h, pw_ref, ps_ref, lg_ref, lb_ref, ws_ref, bst_ref,
         ck_ref, cb_ref, cg_ref, cbb_ref, _qg, wpo_ref, wso_ref, wco_ref, wout_ref) = refs
        pbuf, xbuf, sbuf, xsh, pm_ref, sg_ref, cv_ref = scratch
        first = tile == 0
        tglob = tile * R + lax.broadcasted_iota(jnp.int32, (R, 1), 0)
        pbuf[pl.ds(0, POOL_HALO), :] = jnp.where(first, 0.0, zph_ref[...])
        pbuf[pl.ds(POOL_HALO, R), :] = z_ref[:, pl.ds(0, PW)]
        self.pooled, self.yg, self.cnt = [], [], []
        for gi, w in enumerate(POOL_WINDOWS):
            cols = pl.ds(gi * CHUNK, CHUNK)
            x = pbuf[pl.ds(POOL_HALO, R), cols]
            acc = x
            for j in range(1, w):
                acc = acc + pbuf[pl.ds(POOL_HALO - j, R), cols]
            cnt = jnp.minimum(tglob + 1, w).astype(F32)
            pooled = (acc / cnt - x).astype(BF)
            yg = _dot(pooled, pw_ref[gi].astype(BF))
            pm_ref[:, cols] = (yg * ps_ref[:, cols]).astype(BF)
            self.pooled.append(pooled)
            self.yg.append(yg)
            self.cnt.append(cnt)
        self.zu, self.zv = z_ref[:, pl.ds(OFF_U, PW)], z_ref[:, pl.ds(OFF_V, PW)]
        self.eu, self.ev = lax.erf(self.zu * INV_SQRT2), lax.erf(self.zv * INV_SQRT2)
        self.u = 0.5 * self.zu * (1.0 + self.eu)
        self.vh, self.vr = _ln(0.5 * self.zv * (1.0 + self.ev))
        self.vln = (self.vh * lg_ref[...] + lb_ref[...]).astype(BF)
        tt = lax.broadcasted_iota(jnp.int32, (CHUNK, CHUNK), 0)
        ss = lax.broadcasted_iota(jnp.int32, (CHUNK, CHUNK), 1)
        self.causal = tt >= ss
        self.wc = [jnp.where(self.causal, ws_ref[hd], 0.0).astype(BF) for hd in range(4)]
        for ck in range(R // CHUNK):
            for hd in range(4):
                rows, cols = pl.ds(ck * CHUNK, CHUNK), pl.ds(hd * CHUNK, CHUNK)
                blk = self.vln[ck * CHUNK:(ck + 1) * CHUNK, hd * CHUNK:(hd + 1) * CHUNK]
                sbuf[rows, cols] = _dot(self.wc[hd], blk) + bst_ref[:, pl.ds(hd, 1)]
        self.s = sbuf[...]
        sg_ref[...] = (self.u * self.s).astype(BF)
        self.za = z_ref[:, pl.ds(OFF_A, PW)]
        self.sgb = _sigmoid(z_ref[:, pl.ds(OFF_B, PW)])
        xbuf[pl.ds(0, CONV_HALO), :] = jnp.where(first, 0.0, zah_ref[...] * _sigmoid(zbh_ref[...]))
        xbuf[pl.ds(CONV_HALO, R), :] = self.za * self.sgb
        _phase_copies(xbuf, xsh, R + CONV_HALO - SUBLANES)
        y = jnp.zeros((R, PW), F32) + cb_ref[...]
        for k in range(CONV_K):
            y = y + _window(xbuf, xsh, CONV_HALO - (CONV_K - 1) + k, R) * ck_ref[pl.ds(k, 1), :]
        self.yh, self.yr = _ln(y)
        self.yl = self.yh * cg_ref[...] + cbb_ref[...]
        self.sy = _sigmoid(self.yl)
        cv_ref[...] = (self.yl * self.sy).astype(BF)
        self.g = [_sigmoid(z_ref[:, pl.ds(OFF_G + j * D, D)]) for j in range(3)]
        self.y = [_dot_nt(pm_ref[...], wpo_ref[...]), _dot_nt(sg_ref[...], wso_ref[...]), _dot_nt(cv_ref[...], wco_ref[...])]
        self.merged = (self.g[0] * self.y[0] + self.g[1] * self.y[1] + self.g[2] * self.y[2]).astype(BF)
        self.o = _dot(self.merged, wout_ref[...])


def _phase_scratch(R):
    return pltpu.VMEM((SUBLANES - 1, R + CONV_HALO - SUBLANES, PW), F32)


def _mix_scratch(R):
    return [pltpu.VMEM((R + POOL_HALO, PW), F32), pltpu.VMEM((R + CONV_HALO, PW), F32), pltpu.VMEM((R, PW), F32),
            _phase_scratch(R)]


def _mix_core_fwd(z, h, l, sm, gw):
    T = h.shape[0]
    R = min(256, T)

    def body(*refs):
        ins, out_ref, scratch = refs[:20], refs[20], refs[21:]
        fw = _MixFwd(R, pl.program_id(0), ins, scratch)
        oh, _ = _rms(fw.o)
        out_ref[...] = ins[4][...] + oh * ins[15][...]

    act = pltpu.VMEM((R, PW), BF)
    return pl.pallas_call(
        body, name="mix_core_fwd", grid=(T // R,),
        in_specs=_mix_specs(l, R, lambda i: i),
        out_specs=pl.BlockSpec((R, D), lambda i: (i, 0)),
        out_shape=jax.ShapeDtypeStruct((T, D), F32),
        scratch_shapes=_mix_scratch(R) + [act, act, act],
        compiler_params=_cparams(("arbitrary",)),
    )(z, z, z, z, h, sm["pool_w"], sm["pool_scale"], sm["sgu_ln_g"], sm["sgu_ln_b"], sm["sgu_w_s"], sm["sgu_b_sT"],
      sm["conv_k"], sm["conv_dw_b"], sm["conv_ln_g"], sm["conv_ln_b"], sm["mix_post_g"],
      gw["po"], gw["so"], gw["co"], gw["wo"])


MIX_SMALL_GRADS = (("pool_w", (4, CHUNK, CHUNK)), ("pool_scale", (1, PW)), ("sgu_ln_g", (1, PW)), ("sgu_ln_b", (1, PW)),
                   ("sgu_w_s", (4, CHUNK, CHUNK)), ("sgu_b_sT", (CHUNK, 4)), ("conv_k", (CONV_HALO, PW)),
                   ("conv_dw_b", (1, PW)), ("conv_ln_g", (1, PW)), ("conv_ln_b", (1, PW)), ("mix_post_g", (1, D)))


def _mix_core_bwd(dout, z, h, l, sm, gw):
    T = h.shape[0]
    R = min(128, T)
    nt = T // R
    tile_of = lambda i: nt - 1 - i

    def body(*refs):
        do_ref, ins = refs[0], refs[1:21]
        (dz_ref, mg_ref, dob_ref, dy0_ref, dy1_ref, dy2_ref, pm_ref, sg_ref, cv_ref,
         dpw_ref, dps_ref, dlg_ref, dlb_ref, dws_ref, dbs_ref, dck_ref, dcb_ref, dcg_ref, dcbb_ref, dqg_ref) = refs[21:41]
        pbuf, xbuf, sbuf, xsh, qbuf, dybuf, dvbuf, dysh = refs[41:]
        (_z, _zp, _za, _zb, h_ref, pw_ref, ps_ref, lg_ref, lb_ref, ws_ref, bst_ref,
         ck_ref, cb_ref, cg_ref, cbb_ref, qg_ref, wpo_ref, wso_ref, wco_ref, wout_ref) = ins
        i = pl.program_id(0)
        small = (dpw_ref, dps_ref, dlg_ref, dlb_ref, dws_ref, dbs_ref, dck_ref, dcb_ref, dcg_ref, dcbb_ref, dqg_ref)

        @pl.when(i == 0)
        def _():
            for r in small:
                r[...] = jnp.zeros_like(r)
            qbuf[pl.ds(R, POOL_HALO), :] = jnp.zeros((POOL_HALO, PW), F32)
            dybuf[pl.ds(R, CONV_HALO), :] = jnp.zeros((CONV_HALO, PW), F32)

        fw = _MixFwd(R, tile_of(i), ins, (pbuf, xbuf, sbuf, xsh, pm_ref, sg_ref, cv_ref))
        mg_ref[...] = fw.merged
        oh, orr = _rms(fw.o)
        do, dq = _rms_bwd(oh, orr, qg_ref[...], do_ref[...])
        dqg_ref[...] += dq
        do = do.astype(BF)
        dob_ref[...] = do
        dm = _dot_nt(do, wout_ref[...])
        dys = []
        for j, dyj_ref in enumerate((dy0_ref, dy1_ref, dy2_ref)):
            g = fw.g[j]
            dz_ref[:, pl.ds(OFF_G + j * D, D)] = (dm * fw.y[j] * g * (1.0 - g)).astype(BF)
            dyj = (dm * g).astype(BF)
            dyj_ref[...] = dyj
            dys.append(dyj)
        dpm = _dot(dys[0], wpo_ref[...])
        dsg = _dot(dys[1], wso_ref[...])
        dcv = _dot(dys[2], wco_ref[...])
        for gi, w in enumerate(POOL_WINDOWS):
            cols = pl.ds(gi * CHUNK, CHUNK)
            dpm_g = dpm[:, gi * CHUNK:(gi + 1) * CHUNK]
            dps_ref[:, cols] += _colsum(dpm_g * fw.yg[gi])
            dyg = (dpm_g * ps_ref[:, cols]).astype(BF)
            dpw_ref[gi] += _dot_tn(fw.pooled[gi], dyg)
            dpooled = _dot_nt(dyg, pw_ref[gi].astype(BF))
            qbuf[pl.ds(0, R), cols] = dpooled / fw.cnt[gi]
            acc = -dpooled
            for j in range(w):
                acc = acc + qbuf[pl.ds(j, R), cols]
            dz_ref[:, cols] = acc.astype(BF)
        qbuf[pl.ds(R, POOL_HALO), :] = qbuf[pl.ds(0, POOL_HALO), :]
        ds = dsg * fw.u
        du = dsg * fw.s
        for ck in range(R // CHUNK):
            for hd in range(4):
                rows, cols = pl.ds(ck * CHUNK, CHUNK), pl.ds(hd * CHUNK, CHUNK)
                ds_f = ds[ck * CHUNK:(ck + 1) * CHUNK, hd * CHUNK:(hd + 1) * CHUNK]
                ds_blk = ds_f.astype(BF)
                v_blk = fw.vln[ck * CHUNK:(ck + 1) * CHUNK, hd * CHUNK:(hd + 1) * CHUNK]
                dbs_ref[:, pl.ds(hd, 1)] += jnp.sum(ds_f, axis=1, keepdims=True)
                dws_ref[hd] += jnp.where(fw.causal, _dot_nt(ds_blk, v_blk), 0.0)
                dvbuf[rows, cols] = _dot_tn(fw.wc[hd], ds_blk)
        dgv, dg, db = _ln_bwd(fw.vh, fw.vr, lg_ref[...], dvbuf[...])
        dlg_ref[...] += dg
        dlb_ref[...] += db
        gelu_grad = lambda x, e: 0.5 * (1.0 + e) + x * jnp.exp(-0.5 * x * x) * INV_SQRT_2PI
        dz_ref[:, pl.ds(OFF_V, PW)] = (dgv * gelu_grad(fw.zv, fw.ev)).astype(BF)
        dz_ref[:, pl.ds(OFF_U, PW)] = (du * gelu_grad(fw.zu, fw.eu)).astype(BF)
        dyl = dcv * (fw.sy * (1.0 + fw.yl * (1.0 - fw.sy)))
        dy, dg, db = _ln_bwd(fw.yh, fw.yr, cg_ref[...], dyl)
        dcg_ref[...] += dg
        dcbb_ref[...] += db
        dcb_ref[...] += _colsum(dy)
        dybuf[pl.ds(0, R), :] = dy
        _phase_copies(dybuf, dysh, R + CONV_HALO - SUBLANES)
        dxg = jnp.zeros((R, PW), F32)
        for k in range(CONV_K):
            dck_ref[pl.ds(k, 1), :] += _colsum(dy * _window(xbuf, xsh, CONV_HALO - (CONV_K - 1) + k, R))
            dxg = dxg + _window(dybuf, dysh, CONV_K - 1 - k, R) * ck_ref[pl.ds(k, 1), :]
        dybuf[pl.ds(R, CONV_HALO), :] = dybuf[pl.ds(0, CONV_HALO), :]
        dz_ref[:, pl.ds(OFF_A, PW)] = (dxg * fw.sgb).astype(BF)
        dz_ref[:, pl.ds(OFF_B, PW)] = (dxg * fw.za * fw.sgb * (1.0 - fw.sgb)).astype(BF)

    tile = lambda n: pl.BlockSpec((R, n), lambda i: (tile_of(i), 0))
    small_specs = [pl.BlockSpec(shape, lambda i, nd=len(shape): (0,) * nd) for _, shape in MIX_SMALL_GRADS]
    outs = pl.pallas_call(
        body, name="mix_core_bwd", grid=(nt,),
        in_specs=[tile(D)] + _mix_specs(l, R, tile_of),
        out_specs=[tile(C), tile(D), tile(D), tile(D), tile(D), tile(D), tile(PW), tile(PW), tile(PW)] + small_specs,
        out_shape=[jax.ShapeDtypeStruct((T, C), BF)] + [jax.ShapeDtypeStruct((T, D), BF)] * 5
        + [jax.ShapeDtypeStruct((T, PW), BF)] * 3 + [jax.ShapeDtypeStruct(shape, F32) for _, shape in MIX_SMALL_GRADS],
        scratch_shapes=_mix_scratch(R) + [pltpu.VMEM((R + POOL_HALO, PW), F32), pltpu.VMEM((R + CONV_HALO, PW), F32),
                                          pltpu.VMEM((R, PW), F32), _phase_scratch(R)],
        compiler_params=_cparams(("arbitrary",)),
    )(dout, z, z, z, z, h, sm["pool_w"], sm["pool_scale"], sm["sgu_ln_g"], sm["sgu_ln_b"], sm["sgu_w_s"], sm["sgu_b_sT"],
      sm["conv_k"], sm["conv_dw_b"], sm["conv_ln_g"], sm["conv_ln_b"], sm["mix_post_g"],
      gw["po"], gw["so"], gw["co"], gw["wo"])
    return outs[:9], dict(zip([n for n, _ in MIX_SMALL_GRADS], outs[9:]))


def _ple_fwd(h, p, l, pre_g, post_g, wpr, wpg):
    T = h.shape[0]
    R = min(512, T)

    def body(h_ref, p_ref, pg_ref, qg_ref, wpr_ref, wpg_ref, out_ref, n_ref, pb_ref, gp_ref, e_ref):
        hh = h_ref[...]
        xh, _ = _rms(hh)
        n = (xh * pg_ref[...]).astype(BF)
        n_ref[...] = n
        pb = p_ref[...].astype(BF)
        pb_ref[:, pl.ds(0, PLE)] = pb
        pb_ref[:, pl.ds(PLE, PW - PLE)] = jnp.zeros((R, PW - PLE), BF)
        e = _dot_nt(pb, wpr_ref[:, pl.ds(0, PLE)])
        gp = _dot(n, wpg_ref[...])
        gp_ref[...] = gp
        e_ref[...] = e
        qh, _ = _rms(_sigmoid(gp) * e)
        out_ref[...] = hh + qh * qg_ref[...]

    tile = lambda n: pl.BlockSpec((R, n), lambda i: (i, 0))
    return pl.pallas_call(
        body, name="ple_fwd", grid=(T // R,),
        in_specs=[tile(D), pl.BlockSpec((None, None, R, PLE), lambda i: (l, 0, i, 0)), _row(l, D), _row(l, D),
                  _full((D, PW)), _full((D, D))],
        out_specs=[tile(D), tile(D), tile(PW), tile(D), tile(D)],
        out_shape=[jax.ShapeDtypeStruct((T, D), F32), jax.ShapeDtypeStruct((T, D), BF), jax.ShapeDtypeStruct((T, PW), BF),
                   jax.ShapeDtypeStruct((T, D), F32), jax.ShapeDtypeStruct((T, D), F32)],
        compiler_params=_cparams(("parallel",)),
    )(h, p, pre_g, post_g, wpr, wpg)


def _ple_bwd(dout, h, gp, e, l, pre_g, post_g, wpg):
    T = h.shape[0]
    R = min(512, T)

    def body(do_ref, h_ref, gp_ref, e_ref, pg_ref, qg_ref, wpg_ref, dh_ref, de_ref, dgp_ref, dpg_ref, dqg_ref):
        @pl.when(pl.program_id(0) == 0)
        def _():
            dpg_ref[...] = jnp.zeros_like(dpg_ref)
            dqg_ref[...] = jnp.zeros_like(dqg_ref)

        do = do_ref[...]
        g = _sigmoid(gp_ref[...])
        e = e_ref[...]
        qh, qr = _rms(g * e)
        dq, dqg = _rms_bwd(qh, qr, qg_ref[...], do)
        dqg_ref[...] += dqg
        de_ref[...] = (dq * g).astype(BF)
        dgp = (dq * e * g * (1.0 - g)).astype(BF)
        dgp_ref[...] = dgp
        dn = _dot_nt(dgp, wpg_ref[...])
        xh, xr = _rms(h_ref[...])
        dx, dp = _rms_bwd(xh, xr, pg_ref[...], dn)
        dpg_ref[...] += dp
        dh_ref[...] = do + dx

    tile = lambda n: pl.BlockSpec((R, n), lambda i: (i, 0))
    acc = pl.BlockSpec((1, D), lambda i: (0, 0))
    return pl.pallas_call(
        body, name="ple_bwd", grid=(T // R,),
        in_specs=[tile(D), tile(D), tile(D), tile(D), _row(l, D), _row(l, D), _full((D, D))],
        out_specs=[tile(D), tile(D), tile(D), acc, acc],
        out_shape=[jax.ShapeDtypeStruct((T, D), F32), jax.ShapeDtypeStruct((T, D), BF), jax.ShapeDtypeStruct((T, D), BF),
                   jax.ShapeDtypeStruct((1, D), F32), jax.ShapeDtypeStruct((1, D), F32)],
        compiler_params=_cparams(("arbitrary",)),
    )(dout, h, gp, e, pre_g, post_g, wpg)


def _loss_head(y, target):
    T = y.shape[0]
    R = min(512, T)

    def body(y_ref, t_ref, loss_ref, dy_ref):
        @pl.when(pl.program_id(0) == 0)
        def _():
            loss_ref[...] = jnp.zeros_like(loss_ref)

        err = y_ref[...] - t_ref[0]
        dy_ref[...] = err * (1.0 / D)
        loss_ref[...] += 0.5 * jnp.sum(_mean(err * err), axis=0, keepdims=True)

    tile = pl.BlockSpec((R, D), lambda i: (i, 0))
    return pl.pallas_call(
        body, name="loss_head", grid=(T // R,),
        in_specs=[tile, pl.BlockSpec((1, R, D), lambda i: (0, i, 0))],
        out_specs=[pl.BlockSpec((1, 1), lambda i: (0, 0)), tile],
        out_shape=[jax.ShapeDtypeStruct((1, 1), F32), jax.ShapeDtypeStruct((T, D), F32)],
        compiler_params=_cparams(("arbitrary",)),
    )(y, target)


def _wgrad_f(slab, a, b, l, name, a_col0=0):
    off, r = LAYOUT[name][0], LAYOUT[name][1]
    per = FC // r
    nblk = NDEV // per
    a0 = a_col0 // FC

    T = a.shape[0]
    n = b.shape[1]
    kt = min(1024, T)

    def body(a_ref, b_ref, slab_ref, out_ref, acc_ref):
        k = pl.program_id(1)

        @pl.when(k == 0)
        def _():
            acc_ref[...] = jnp.zeros_like(acc_ref)

        acc_ref[...] += _dot_tn(a_ref[...], b_ref[...])

        @pl.when(k == pl.num_programs(1) - 1)
        def _():
            out_ref[...] = acc_ref[...].reshape(per, r, n).astype(out_ref.dtype)

    return pl.pallas_call(
        body, name="wgrad_" + name, grid=(nblk, T // kt),
        in_specs=[pl.BlockSpec((kt, FC), lambda i, k: (k, i + a0)), pl.BlockSpec((kt, n), lambda i, k: (k, 0)),
                  pl.BlockSpec(memory_space=pl.ANY)],
        out_specs=pl.BlockSpec((per, None, r, n), lambda i, k: (i, l, off // r, 0)),
        out_shape=jax.ShapeDtypeStruct(slab.shape, slab.dtype),
        scratch_shapes=[pltpu.VMEM((FC, n), F32)],
        input_output_aliases={2: 0},
        compiler_params=_cparams(("parallel", "arbitrary")),
    )(a, b, slab)


def _wgrad_d(slab, a, b, l, name):
    off, r, col0, width = LAYOUT[name][:4]
    T = a.shape[0]
    n = b.shape[1]
    kt = min(1024, T)

    def body(a_ref, b_ref, slab_ref, out_ref, acc_ref):
        k = pl.program_id(0)

        @pl.when(k == 0)
        def _():
            acc_ref[...] = jnp.zeros_like(acc_ref)

        acc_ref[...] += _dot_tn(a_ref[...], b_ref[...])

        @pl.when(k == pl.num_programs(0) - 1)
        def _():
            out_ref[...] = acc_ref[...].reshape(NDEV, r, n).astype(out_ref.dtype)

    return pl.pallas_call(
        body, name="wgrad_" + name, grid=(T // kt,),
        in_specs=[pl.BlockSpec((kt, D), lambda k: (k, 0)), pl.BlockSpec((kt, n), lambda k: (k, 0)),
                  pl.BlockSpec(memory_space=pl.ANY)],
        out_specs=pl.BlockSpec((NDEV, None, r, n), lambda k: (0, l, off // r, col0 // n)),
        out_shape=jax.ShapeDtypeStruct(slab.shape, slab.dtype),
        scratch_shapes=[pltpu.VMEM((D, n), F32)],
        input_output_aliases={2: 0},
        compiler_params=_cparams(("arbitrary",)),
    )(a, b, slab)


def _adamw(w, g, m, v):
    m = ADAM_B1 * m + (1.0 - ADAM_B1) * g
    v = ADAM_B2 * v + (1.0 - ADAM_B2) * (g * g)
    m_hat = m / (1.0 - ADAM_B1 ** ADAM_STEP)
    v_hat = v / (1.0 - ADAM_B2 ** ADAM_STEP)
    delta = -ADAM_LR * (m_hat / (jnp.sqrt(v_hat) + ADAM_EPS) + ADAM_WD * w)
    return delta, m, v


def _adam_big(name, me, slab, recv, w, m, v, l0, nl, prev):
    off, rows, col0, width, tr = LAYOUT[name]
    nt = 4 if name == "w_in" else 1
    transpose = tr and name not in SWAPPED
    pblk = (rows, width // nt)
    wblk = (width, rows) if transpose else pblk
    pmap = lambda t: (off // rows, col0 // (width // nt) + t)

    def body(me_ref, s_ref, r_ref, w_ref, m_ref, v_ref, *outs):
        g_out, d_out, m_out, v_out, token = outs[-5:]
        g = s_ref[...].astype(F32)
        for k in range(NDEV - 1):
            g = g + r_ref[k].astype(F32)
        if transpose:
            g = g.T
        d, mm, vv = _adamw(w_ref[...], g, m_ref[...], v_ref[...])
        g_out[...] = g
        d_out[...] = d
        m_out[...] = mm
        v_out[...] = vv
        token[...] = jnp.zeros_like(token)

    wspec = pl.BlockSpec((None,) + wblk, lambda l, t, me: (l + l0, 0, t))
    n_prev = 0 if prev is None else 4
    grid_spec = pltpu.PrefetchScalarGridSpec(
        num_scalar_prefetch=1, grid=(nl, nt),
        in_specs=[pl.BlockSpec((None, None) + pblk, lambda l, t, me: (me[0], l + l0) + pmap(t)),
                  pl.BlockSpec((None, NDEV - 1) + pblk, lambda l, t, me: (l + l0, 0) + pmap(t)), wspec, wspec, wspec]
        + [ANY] * n_prev,
        out_specs=[wspec] * 4 + [pl.BlockSpec((8, 128), lambda l, t, me: (0, 0))])
    return pl.pallas_call(
        body, name=f"adam_{name}_{l0}", grid_spec=grid_spec,
        out_shape=[jax.ShapeDtypeStruct(w.shape, F32)] * 4 + [jax.ShapeDtypeStruct((8, 128), F32)],
        input_output_aliases={6 + i: i for i in range(n_prev)},
        compiler_params=_cparams(("arbitrary", "arbitrary")),
    )(me, slab, recv, w, m, v, *(prev or ()))


def _adam_small(gall, w, m, v):
    rows = w.shape[0]
    tr = 32

    def body(g_ref, w_ref, m_ref, v_ref, g_out, d_out, m_out, v_out):
        g = g_ref[0]
        for k in range(1, NDEV):
            g = g + g_ref[k]
        d, mm, vv = _adamw(w_ref[...], g, m_ref[...], v_ref[...])
        g_out[...] = g
        d_out[...] = d
        m_out[...] = mm
        v_out[...] = vv

    spec = pl.BlockSpec((tr, D), lambda i: (i, 0))
    return pl.pallas_call(
        body, name="adam_small", grid=(rows // tr,),
        in_specs=[pl.BlockSpec((NDEV, tr, D), lambda i: (0, i, 0)), spec, spec, spec],
        out_specs=[spec] * 4, out_shape=[jax.ShapeDtypeStruct(w.shape, F32)] * 4,
        compiler_params=_cparams(("parallel",)),
    )(gall, w, m, v)


def _adam_plain(g, w, m, v):
    def body(g_ref, w_ref, m_ref, v_ref, d_out, m_out, v_out):
        d, mm, vv = _adamw(w_ref[...], g_ref[...], m_ref[...], v_ref[...])
        d_out[...] = d
        m_out[...] = mm
        v_out[...] = vv

    vm = pl.BlockSpec(memory_space=pltpu.VMEM)
    return pl.pallas_call(
        body, name="adam_conv_k", in_specs=[vm] * 4, out_specs=[vm] * 3,
        out_shape=[jax.ShapeDtypeStruct(w.shape, F32)] * 3,
    )(g, w, m, v)


def _pad_cols(a, n):
    return jnp.pad(a, [(0, 0)] * (a.ndim - 1) + [(0, n - a.shape[-1])])


def _pack_small(d, conv_k_full):
    L = d["pool_w"].shape[0]
    gains = jnp.stack([d[n].reshape(L, D) for n in GAINS], axis=1)
    halves = [_pad_cols(d[n].reshape(L, PW), D) for n in HALVES] + [jnp.zeros((L, D), F32)]
    halves = jnp.stack(halves, axis=1)
    ck = jnp.zeros((L, 16, D), F32) if conv_k_full is None else conv_k_full.reshape(L, 16, D)
    out = jnp.concatenate([gains, halves, d["pool_w"].reshape(L, 64, D), d["sgu_w_s"].reshape(L, 64, D), ck], axis=1)
    return out.reshape(L * SMALL_ROWS, D)


def _unpack_small(a, like):
    L = a.shape[0] // SMALL_ROWS
    a = a.reshape(L, SMALL_ROWS, D)
    out = {}
    for i, n in enumerate(GAINS):
        out[n] = a[:, i, :].reshape(like[n].shape)
    for i, n in enumerate(HALVES):
        out[n] = a[:, 8 + i, :PW].reshape(like[n].shape)
    out["pool_w"] = a[:, 16:80, :].reshape(like["pool_w"].shape)
    out["sgu_w_s"] = a[:, 80:144, :].reshape(like["sgu_w_s"].shape)
    out["conv_k_full"] = a[:, 144:160, :].reshape(L, CONV_HALO, PW)
    return out


WEIGHTS = ("ffn1_pre_g", "ffn1_w_gate", "ffn1_w_up", "ffn1_w_down", "ffn1_post_g", "mix_pre_g", "w_in", "pool_w", "pool_scale",
           "w_pool_out", "sgu_ln_g", "sgu_ln_b", "sgu_w_s", "sgu_b_s", "w_sgu_out", "conv_dw_k", "conv_dw_b", "conv_ln_g",
           "conv_ln_b", "w_conv_out", "w_out", "mix_post_g", "ffn2_pre_g", "ffn2_w_gate", "ffn2_w_up", "ffn2_w_down",
           "ffn2_post_g", "ple_w_proj", "ple_pre_g", "ple_w_gate", "ple_post_g")
SMALL = GAINS + HALVES + ("pool_w", "sgu_w_s")
FFN1_ROWS = ((704, 1056),)
FFN1_ROWS_OUT = ((0, 704), (1760, ROWS - 1760))


class _Comm:
    def __init__(self, w, me):
        self.w, self.me = w, me

    def gather_start(self, l, after):
        packed, lands = _prep(self.w, l, self.me, [lax.empty(GATHERED[n], BF) for n in GNAMES])
        send_sems, recv_sems, packed, lands, token = _gather_start(l, packed, lands, packed if after is None else after)
        return (send_sems, recv_sems, packed, lands), token[0, 0]

    def gather_finish(self, l, state, after):
        send_sems, recv_sems, packed, lands = state
        packed, lands = _gather_wait(l, send_sems, recv_sems, packed, lands, packed if after is None else after)
        return _gather_forward(l, packed, lands)

    def scatter_start(self, tag, l, slab, recv, parts, after):
        after = jnp.zeros((8, 128), F32) if after is None else after
        send_sems, recv_sems, slab, recv, token = _scatter_start(tag, l, slab, recv, parts, after)
        return (tag, l, send_sems, recv_sems, parts), slab, recv, token[0, 0]

    def gather_small(self, name, rows):
        return _gather_rows("allgather_" + name, rows)

    def scatter_finish(self, state, slab, recv):
        tag, l, send_sems, recv_sems, parts = state
        return _scatter_wait(tag, l, send_sems, recv_sems, slab, recv, parts)


def _fwd_bwd(x, p, target, w, conv_k, comm):
    L = w["w_in"].shape[0]
    T = x.shape[1]
    row = lambda a: a.reshape(L, 1, a.shape[-1])
    sm = {n: row(w[n]) for n in GAINS + ("pool_scale", "sgu_ln_g", "sgu_ln_b", "conv_dw_b", "conv_ln_g", "conv_ln_b")}
    sm.update(pool_w=w["pool_w"], sgu_w_s=w["sgu_w_s"], sgu_b_sT=w["sgu_b_s"].transpose(0, 2, 1), conv_k=conv_k)

    h = x[0]
    saved, gws = [], [None] * L
    state, _ = comm.gather_start(0, conv_k)
    gws[0] = comm.gather_finish(0, state, None)
    for l in range(L):
        gw = gws[l]
        pre1 = sm["ffn1_pre_g"]
        if l + 1 < L:
            state, token = comm.gather_start(l + 1, gw["pg"])
            pre1 = pre1 + token
        s = {"h0": h}
        h, s["n1"], s["ab1"], s["f1"] = _ffn_fwd(h, l, pre1, sm["ffn1_post_g"], gw["gu1"], gw["d1"])
        s["h1"] = h
        s["nm"], s["z"] = _mix_in_fwd(h, l, sm["mix_pre_g"], gw["win"])
        h = _mix_core_fwd(s["z"], h, l, sm, gw)
        s["h2"] = h
        h, s["n2"], s["ab2"], s["f2"] = _ffn_fwd(h, l, sm["ffn2_pre_g"], sm["ffn2_post_g"], gw["gu2"], gw["d2"])
        s["h3"] = h
        h, s["np"], s["pb"], s["gp"], s["e"] = _ple_fwd(h, p, l, sm["ple_pre_g"], sm["ple_post_g"], gw["pr"], gw["pg"])
        saved.append(s)
        if l + 1 < L:
            gws[l + 1] = comm.gather_finish(l + 1, state, h)

    loss_part, dh = _loss_head(h, target)

    slab = lax.empty((NDEV, L, ROWS, D), BF)
    recv = lax.empty((L, NDEV - 1, ROWS, D), BF)
    sg = {n: [None] * L for n in SMALL + ("conv_k", "sgu_b_sT")}
    pending, token = [], None
    gsmall = [None] * L

    def small_rows(l, with_ffn1):
        d = {n: sg[n][l][None] for n in SMALL if n != "sgu_b_s" and (with_ffn1 or not n.startswith("ffn1"))}
        d["sgu_b_s"] = sg["sgu_b_sT"][l].T[None]
        if not with_ffn1:
            d["ffn1_pre_g"] = d["ffn1_post_g"] = jnp.zeros((1, D), F32)
        return _pack_small(d, sg["conv_k"][l][None])
    for l in reversed(range(L)):
        s, gw = saved[l], gws[l]
        post = sm["ple_post_g"] if token is None else sm["ple_post_g"] + token
        dh, de, dgp, sg["ple_pre_g"][l], sg["ple_post_g"][l] = _ple_bwd(
            dh, s["h3"], s["gp"], s["e"], l, sm["ple_pre_g"], post, gw["pg"])
        slab = _wgrad_d(slab, s["np"], dgp, l, "ple_w_gate")
        slab = _wgrad_d(slab, de, s["pb"], l, "ple_w_proj")

        dh, dab, ss, df, sg["ffn2_pre_g"][l], sg["ffn2_post_g"][l] = _ffn_bwd(
            dh, s["h2"], s["ab2"], s["f2"], l, sm["ffn2_pre_g"], sm["ffn2_post_g"], gw["gu2"], gw["d2"])
        slab = _wgrad_f(slab, dab, s["n2"], l, "ffn2_w_gate")
        slab = _wgrad_f(slab, dab, s["n2"], l, "ffn2_w_up", a_col0=F)
        slab = _wgrad_f(slab, ss, df, l, "ffn2_w_down")

        (dz, mg, dob, dy0, dy1, dy2, pm, sgv, cv), g_mix = _mix_core_bwd(dh, s["z"], s["h1"], l, sm, gw)
        for n in g_mix:
            sg[n][l] = g_mix[n]
        dh, sg["mix_pre_g"][l] = _mix_in_bwd(dh, dz, s["h1"], l, sm["mix_pre_g"], gw["win"])
        slab = _wgrad_f(slab, dz, s["nm"], l, "w_in")
        slab = _wgrad_d(slab, mg, dob, l, "w_out")
        slab = _wgrad_d(slab, dy0, pm, l, "w_pool_out")
        slab = _wgrad_d(slab, dy1, sgv, l, "w_sgu_out")
        slab = _wgrad_d(slab, dy2, cv, l, "w_conv_out")

        pre1, parts, tag = sm["ffn1_pre_g"], ((0, ROWS),), str(l)
        if l == 0:
            for st in pending:
                slab, recv = comm.scatter_finish(st, slab, recv)
            gsmall[0] = comm.gather_small("small_grads_0", small_rows(0, False))
            st, slab, recv, token = comm.scatter_start("0a", 0, slab, recv, FFN1_ROWS_OUT, gsmall[0])
            pending, pre1, parts, tag = [st], pre1 + token, FFN1_ROWS, "0b"
        dh, dab, ss, df, sg["ffn1_pre_g"][l], sg["ffn1_post_g"][l] = _ffn_bwd(
            dh, s["h0"], s["ab1"], s["f1"], l, pre1, sm["ffn1_post_g"], gw["gu1"], gw["d1"])
        slab = _wgrad_f(slab, dab, s["n1"], l, "ffn1_w_gate")
        slab = _wgrad_f(slab, dab, s["n1"], l, "ffn1_w_up", a_col0=F)
        slab = _wgrad_f(slab, ss, df, l, "ffn1_w_down")
        after = None
        if l > 0:
            for st in pending:
                slab, recv = comm.scatter_finish(st, slab, recv)
            pending = []
            after = gsmall[l] = comm.gather_small(f"small_grads_{l}", small_rows(l, True))
        st, slab, recv, token = comm.scatter_start(tag, l, slab, recv, parts, after)
        pending.append(st)
    return loss_part, dh.reshape(1, T, D), slab, recv, sg, gsmall, pending, token


def _step(x, p, target, w, m, v):
    L = w["w_in"].shape[0]
    ix, iy, ic = lax.axis_index("x"), lax.axis_index("y"), lax.axis_index("c")
    me = 4 * ix + 2 * iy + ic

    ck_local = jnp.pad(w["conv_dw_k"].reshape(L, CONV_K, 64), ((0, 0), (0, 1), (0, 0))).reshape(L * 2, D)
    ck_all = _gather_rows("allgather_conv_k", _pad_rows8(ck_local))[:, :L * 2]
    conv_k = ck_all.reshape(NDEV, L, CONV_HALO, 64).transpose(1, 2, 0, 3).reshape(L, CONV_HALO, PW)

    me_arr = me.astype(jnp.int32).reshape(1)
    comm = _Comm(w, me_arr)
    loss_part, grad_x, slab, recv, sg, gsmall, pending, token = _fwd_bwd(x, p, target, w, conv_k, comm)
    loss = lax.psum(loss_part[0, 0], ("x", "y", "c"))

    early = {}
    if L > 1:
        early = {n: _adam_big(n, me_arr, slab, recv, w[n], m[n], v[n], 1, L - 1, None) for n in BIG}
        token = token + sum(e[4][0, 0] for e in early.values())
    last = jnp.pad(sg["ffn1_pre_g"][0], ((0, 7), (0, 0))) + jnp.pad(sg["ffn1_post_g"][0], ((1, 6), (0, 0))) + token
    last = comm.gather_small("small_grads_last", last)
    for st in pending:
        slab, recv = comm.scatter_finish(st, slab, recv)
    gall = jnp.concatenate(gsmall, axis=1).at[:, 0:2, :].set(last[:, 0:2, :])
    res = {n: _adam_big(n, me_arr, slab, recv, w[n], m[n], v[n], 0, 1, early[n][:4] if early else None)[:4] for n in BIG}
    for n in SWAPPED:
        res[n] = tuple(jnp.swapaxes(a, 1, 2) for a in res[n])

    outs = _adam_small(gall, _pack_small(w, None), _pack_small(m, None), _pack_small(v, None))
    unpacked = [_unpack_small(o, w) for o in outs]
    for n in SMALL:
        res[n] = tuple(u[n] for u in unpacked)

    gk = lax.dynamic_slice_in_dim(unpacked[0]["conv_k_full"][:, :CONV_K, :], me * 64, 64, axis=2)
    shp = w["conv_dw_k"].shape
    flat = lambda a: a.reshape(L * CONV_K, 64)
    dk, mk, vk = _adam_plain(flat(gk), flat(w["conv_dw_k"]), flat(m["conv_dw_k"]), flat(v["conv_dw_k"]))
    res["conv_dw_k"] = (gk.reshape(shp), dk.reshape(shp), mk.reshape(shp), vk.reshape(shp))

    return (loss, grad_x, *[res[n][0] for n in WEIGHTS], *[res[n][1] for n in WEIGHTS],
            *[res[n][2] for n in WEIGHTS], *[res[n][3] for n in WEIGHTS])


def _pad_rows8(a):
    return jnp.pad(a, ((0, (-a.shape[0]) % 8), (0, 0)))


def kernel(x, p, ffn1_pre_g, ffn1_w_gate, ffn1_w_up, ffn1_w_down, ffn1_post_g, mix_pre_g, w_in, pool_w, pool_scale, w_pool_out, sgu_ln_g, sgu_ln_b, sgu_w_s, sgu_b_s, w_sgu_out, conv_dw_k, conv_dw_b, conv_ln_g, conv_ln_b, w_conv_out, w_out, mix_post_g, ffn2_pre_g, ffn2_w_gate, ffn2_w_up, ffn2_w_down, ffn2_post_g, ple_w_proj, ple_pre_g, ple_w_gate, ple_post_g, loss_target, m_ffn1_pre_g, m_ffn1_w_gate, m_ffn1_w_up, m_ffn1_w_down, m_ffn1_post_g, m_mix_pre_g, m_w_in, m_pool_w, m_pool_scale, m_w_pool_out, m_sgu_ln_g, m_sgu_ln_b, m_sgu_w_s, m_sgu_b_s, m_w_sgu_out, m_conv_dw_k, m_conv_dw_b, m_conv_ln_g, m_conv_ln_b, m_w_conv_out, m_w_out, m_mix_post_g, m_ffn2_pre_g, m_ffn2_w_gate, m_ffn2_w_up, m_ffn2_w_down, m_ffn2_post_g, m_ple_w_proj, m_ple_pre_g, m_ple_w_gate, m_ple_post_g, v_ffn1_pre_g, v_ffn1_w_gate, v_ffn1_w_up, v_ffn1_w_down, v_ffn1_post_g, v_mix_pre_g, v_w_in, v_pool_w, v_pool_scale, v_w_pool_out, v_sgu_ln_g, v_sgu_ln_b, v_sgu_w_s, v_sgu_b_s, v_w_sgu_out, v_conv_dw_k, v_conv_dw_b, v_conv_ln_g, v_conv_ln_b, v_w_conv_out, v_w_out, v_mix_post_g, v_ffn2_pre_g, v_ffn2_w_gate, v_ffn2_w_up, v_ffn2_w_down, v_ffn2_post_g, v_ple_w_proj, v_ple_pre_g, v_ple_w_gate, v_ple_post_g):
    args = dict(locals())
    give = lambda n, a: jnp.swapaxes(a, 1, 2) if n in SWAPPED else a
    w = {n: give(n, args[n]) for n in WEIGHTS}
    m = {n: give(n, args["m_" + n]) for n in WEIGHTS}
    v = {n: give(n, args["v_" + n]) for n in WEIGHTS}
    return _step(x, p, loss_target, w, m, v)
```

```python
import functools

import jax
import jax.numpy as jnp
from jax import lax
from jax.experimental import pallas as pl
from jax.experimental.pallas import tpu as pltpu

D = 1024
F = 2816
C = 5632
PW = 512
PLE = 256
NDEV = 8
CHUNK = 128
POOL_WINDOWS = (2, 4, 8, 16)
CONV_K = 31
POOL_HALO = 16
CONV_HALO = 32
EPS = 1e-6
OFF_U, OFF_V, OFF_A, OFF_B, OFF_G = 512, 1024, 1536, 2048, 2560

ADAM_LR, ADAM_B1, ADAM_B2, ADAM_EPS, ADAM_WD, ADAM_STEP = 0.001, 0.9, 0.999, 1e-08, 0.01, 10

BF = jnp.bfloat16
F32 = jnp.float32
VMEM_LIMIT = 56 * 1024 * 1024
MESH = pl.DeviceIdType.MESH
INV_SQRT2 = 0.7071067811865476
INV_SQRT_2PI = 0.3989422804014327

ROWS = 3328
LAYOUT = {
    "w_in": (0, 704, 0, 1024, True),
    "ffn1_w_gate": (704, 352, 0, 1024, True),
    "ffn1_w_up": (1056, 352, 0, 1024, True),
    "ffn1_w_down": (1408, 352, 0, 1024, False),
    "ffn2_w_gate": (1760, 352, 0, 1024, True),
    "ffn2_w_up": (2112, 352, 0, 1024, True),
    "ffn2_w_down": (2464, 352, 0, 1024, False),
    "w_pool_out": (2816, 128, 0, 512, True),
    "w_sgu_out": (2816, 128, 512, 512, True),
    "w_conv_out": (2944, 128, 0, 512, True),
    "ple_w_proj": (2944, 128, 512, 256, True),
    "w_out": (3072, 128, 0, 1024, False),
    "ple_w_gate": (3200, 128, 0, 1024, False),
}
BIG = tuple(LAYOUT)
SWAPPED = ("w_in", "ffn1_w_gate", "ffn1_w_up", "ffn2_w_gate", "ffn2_w_up")
GATHERED = {
    "win": (C, D), "gu1": (2 * F, D), "d1": (F, D), "gu2": (2 * F, D), "d2": (F, D),
    "po": (D, PW), "so": (D, PW), "co": (D, PW), "pr": (D, PW), "wo": (D, D), "pg": (D, D),
}
PIECES = (
    ("win", 0, 0, 704, 0, 1024), ("gu1", 0, 704, 352, 0, 1024), ("gu1", F, 1056, 352, 0, 1024),
    ("d1", 0, 1408, 352, 0, 1024), ("gu2", 0, 1760, 352, 0, 1024), ("gu2", F, 2112, 352, 0, 1024),
    ("d2", 0, 2464, 352, 0, 1024), ("po", 0, 2816, 128, 0, 512), ("so", 0, 2816, 128, 512, 512),
    ("co", 0, 2944, 128, 0, 512), ("pr", 0, 2944, 128, 512, 512), ("wo", 0, 3072, 128, 0, 1024),
    ("pg", 0, 3200, 128, 0, 1024),
)
GAINS = ("ffn1_pre_g", "ffn1_post_g", "mix_pre_g", "mix_post_g", "ffn2_pre_g", "ffn2_post_g", "ple_pre_g", "ple_post_g")
HALVES = ("pool_scale", "sgu_ln_g", "sgu_ln_b", "sgu_b_s", "conv_dw_b", "conv_ln_g", "conv_ln_b")
SMALL_ROWS = 160


def _cparams(sem=None, **kw):
    if sem is not None:
        kw["dimension_semantics"] = sem
    return pltpu.CompilerParams(vmem_limit_bytes=VMEM_LIMIT, **kw)


def _whole(l, shape):
    nd = len(shape)
    return pl.BlockSpec((None,) + tuple(shape), lambda *_: (l,) + (0,) * nd, pipeline_mode=pl.Buffered(1))


def _full(shape):
    nd = len(shape)
    return pl.BlockSpec(tuple(shape), lambda *_: (0,) * nd, pipeline_mode=pl.Buffered(1))


def _row(l, n):
    return pl.BlockSpec((None, 1, n), lambda *_: (l, 0, 0))


def _dot(a, b):
    return jnp.dot(a, b, preferred_element_type=F32)


def _dot_nt(a, b):
    return lax.dot_general(a, b, (((1,), (1,)), ((), ())), preferred_element_type=F32)


def _dot_tn(a, b):
    return lax.dot_general(a, b, (((0,), (0,)), ((), ())), preferred_element_type=F32)


def _mean(x):
    return jnp.mean(x, axis=-1, keepdims=True)


def _colsum(x):
    return jnp.sum(x, axis=0, keepdims=True)


def _rms(x):
    r = lax.rsqrt(_mean(x * x) + EPS)
    return x * r, r


def _rms_bwd(xh, r, g, dy):
    dxh = dy * g
    return r * (dxh - xh * _mean(dxh * xh)), _colsum(dy * xh)


def _ln(x):
    xc = x - _mean(x)
    r = lax.rsqrt(_mean(xc * xc) + EPS)
    return xc * r, r


def _ln_bwd(xh, r, g, dy):
    dxh = dy * g
    return r * (dxh - _mean(dxh) - xh * _mean(dxh * xh)), _colsum(dy * xh), _colsum(dy)


def _sigmoid(x):
    return jax.nn.sigmoid(x)


def _prep(w, l, me, lands):
    nb, npc = len(BIG), len(PIECES)

    def body(me_ref, *refs):
        ins, out = dict(zip(BIG, refs[:nb])), refs[nb + NG]
        land_refs, sems = refs[nb + NG + 1:nb + 2 * NG + 1], refs[-1]
        for name, (off, rows, col0, width, tr) in LAYOUT.items():
            v = ins[name][...]
            if tr and name not in SWAPPED:
                v = v.T
            out[pl.ds(off, rows), pl.ds(col0, width)] = v.astype(BF)
        out[pl.ds(2944, 128), pl.ds(768, 256)] = jnp.zeros((128, 256), BF)
        mine = [pltpu.make_async_copy(src, dst, sems.at[i]) for i, (src, dst) in enumerate(_piece_refs(out, land_refs, me_ref[0]))]
        for cp in mine:
            cp.start()
        for cp in mine:
            cp.wait()

    grid_spec = pltpu.PrefetchScalarGridSpec(
        num_scalar_prefetch=1, grid=(1,),
        in_specs=[pl.BlockSpec((None,) + w[n].shape[1:], lambda i, me: (l, 0, 0)) for n in BIG] + [ANY] * NG,
        out_specs=[pl.BlockSpec((ROWS, D), lambda i, me: (0, 0))] + [ANY] * NG,
        scratch_shapes=[pltpu.SemaphoreType.DMA((npc,))])
    outs = pl.pallas_call(
        body, name="prep", grid_spec=grid_spec,
        out_shape=[jax.ShapeDtypeStruct((ROWS, D), BF)] + [jax.ShapeDtypeStruct(a.shape, a.dtype) for a in lands],
        input_output_aliases={1 + nb + i: 1 + i for i in range(NG)},
        compiler_params=_cparams(("arbitrary",)),
    )(me, *[w[n] for n in BIG], *lands)
    return outs[0], list(outs[1:])


def _place():
    x, y, c = lax.axis_index("x"), lax.axis_index("y"), lax.axis_index("c")
    chips = [(1 - x, y), (x, 1 - y), (1 - x, 1 - y)]
    return x, y, c, chips


def _allgather(name, src, dsts, pieces):
    npc = len(pieces)

    def body(src_ref, *rest):
        outs, (send_sems, recv_sems, local_sems) = rest[:len(dsts)], rest[len(dsts):]
        x, y, c, chips = _place()
        me, sibling = (x, y, c), (x, y, 1 - c)

        def shard_of(dev):
            return 4 * dev[0] + 2 * dev[1] + dev[2]

        def copies(k, block, to, from_src):
            res = []
            for di, dst_fn, src_sl in pieces:
                dst = outs[di].at[dst_fn(shard_of(block))]
                s = src_ref.at[src_sl] if from_src else dst
                res.append(pltpu.make_async_remote_copy(src_ref=s, dst_ref=dst, send_sem=send_sems.at[k],
                                                        recv_sem=recv_sems.at[k], device_id=to, device_id_type=MESH))
            return res

        def whole(k):
            return pltpu.make_async_remote_copy(src_ref=src_ref, dst_ref=src_ref, send_sem=send_sems.at[k],
                                                recv_sem=recv_sems.at[k], device_id=me, device_id_type=MESH)

        mine = [pltpu.make_async_copy(src_ref.at[src_sl], outs[di].at[dst_fn(shard_of(me))], local_sems.at[i])
                for i, (di, dst_fn, src_sl) in enumerate(pieces)]
        for cp in mine:
            cp.start()
        for cp in copies(0, me, sibling, True):
            cp.start()
        for j, chip in enumerate(chips):
            for cp in copies(1 + j, me, (*chip, c), True):
                cp.start()
        for j, chip in enumerate(chips):
            whole(1 + j).wait_recv()
            for cp in copies(4 + j, (*chip, c), sibling, False):
                cp.start()
        for k in (0, 4, 5, 6):
            whole(k).wait_recv()
        for k in range(7):
            whole(k).wait_send()
        for cp in mine:
            cp.wait()

    any_spec = pl.BlockSpec(memory_space=pl.ANY)
    return pl.pallas_call(
        body, name=name, in_specs=[any_spec], out_specs=[any_spec] * len(dsts), out_shape=dsts,
        scratch_shapes=[pltpu.SemaphoreType.DMA((7,)), pltpu.SemaphoreType.DMA((7,)), pltpu.SemaphoreType.DMA((npc,))],
    )(src)


HBM = pl.BlockSpec(memory_space=pltpu.HBM)
SEM = pl.BlockSpec(memory_space=pltpu.SEMAPHORE)
ANY = pl.BlockSpec(memory_space=pl.ANY)
EFFECT = pltpu.SideEffectType.DATAFLOW_SIDE_EFFECTING
GNAMES = tuple(GATHERED)
NG = len(GNAMES)


def _in_hbm(a):
    return pltpu.with_memory_space_constraint(a, pltpu.HBM)


ALL_PIECES = tuple(range(len(PIECES)))
FFN1_PIECES = (1, 2, 3)
REST_PIECES = tuple(i for i in ALL_PIECES if i not in FFN1_PIECES)


def _piece_refs(packed_ref, land_refs, shard, pieces=ALL_PIECES):
    out = []
    for gname, row0, poff, prow, pcol, width in [PIECES[i] for i in pieces]:
        land = land_refs[GNAMES.index(gname)]
        out.append((packed_ref.at[pl.ds(poff, prow), pl.ds(pcol, width)], land.at[pl.ds(row0 + shard * prow, prow), :]))
    return out


def _piece_rows(pieces):
    return sum(PIECES[i][3] * PIECES[i][5] for i in pieces) // D


def _gather_start(l, packed, lands, after, pieces):
    def body(packed_ref, *rest):
        land_refs, send_sems, recv_sems = rest[:NG], rest[NG + 1], rest[NG + 2]
        token = rest[-1]
        x, y, c, chips = _place()
        targets = [(x, y, 1 - c)] + [(*chip, c) for chip in chips]
        for k, to in enumerate(targets):
            for src, dst in _piece_refs(packed_ref, land_refs, 4 * x + 2 * y + c, pieces):
                pltpu.make_async_remote_copy(src_ref=src, dst_ref=dst, send_sem=send_sems.at[k], recv_sem=recv_sems.at[k],
                                             device_id=to, device_id_type=MESH).start()
        token[...] = jnp.zeros_like(token)

    hbm = lambda a: pltpu.HBM(a.shape, a.dtype)
    outs = pl.pallas_call(
        body, name=f"gather_start_{l}",
        out_shape=(pltpu.SemaphoreType.DMA((4,)), pltpu.SemaphoreType.DMA((4,)), hbm(packed), *[hbm(a) for a in lands],
                   jax.ShapeDtypeStruct((8, 128), F32)),
        in_specs=(HBM,) * (1 + NG) + (ANY,),
        out_specs=(SEM, SEM) + (HBM,) * (1 + NG) + (pl.BlockSpec(memory_space=pltpu.VMEM),),
        input_output_aliases={i: 2 + i for i in range(1 + NG)},
        compiler_params=pltpu.CompilerParams(has_side_effects=EFFECT),
    )(_in_hbm(packed), *[_in_hbm(a) for a in lands], after)
    return outs[0], outs[1], outs[2], list(outs[3:3 + NG]), outs[-1]


def _gather_wait(l, send_sems, recv_sems, packed, lands, after, pieces):
    nrows = _piece_rows(pieces)

    def body(packed_ref, *rest):
        send_sems, recv_sems = rest[NG], rest[NG + 1]
        x, y, c, _ = _place()
        for k in range(4):
            size = packed_ref.at[pl.ds(0, nrows)]
            cp = pltpu.make_async_remote_copy(src_ref=size, dst_ref=size, send_sem=send_sems.at[k],
                                              recv_sem=recv_sems.at[k], device_id=(x, y, c), device_id_type=MESH)
            cp.wait_send()
            cp.wait_recv()

    hbm = lambda a: pltpu.HBM(a.shape, a.dtype)
    outs = pl.pallas_call(
        body, name=f"gather_wait_{l}",
        out_shape=(hbm(packed), *[hbm(a) for a in lands]),
        in_specs=(HBM,) * (1 + NG) + (SEM, SEM, ANY), out_specs=(HBM,) * (1 + NG),
        input_output_aliases={i: i for i in range(1 + NG)},
        compiler_params=pltpu.CompilerParams(has_side_effects=EFFECT),
    )(packed, *lands, send_sems, recv_sems, after)
    return outs[0], list(outs[1:])


def _gather_forward(l, packed, lands, pieces):
    nrows = _piece_rows(pieces)

    def body(packed_ref, *rest):
        land_refs, (send_sems, recv_sems) = rest[NG:2 * NG], rest[2 * NG:]
        x, y, c, chips = _place()
        for j, (cx, cy) in enumerate(chips):
            for _, rows in _piece_refs(packed_ref, land_refs, 4 * cx + 2 * cy + c, pieces):
                pltpu.make_async_remote_copy(src_ref=rows, dst_ref=rows, send_sem=send_sems.at[j], recv_sem=recv_sems.at[j],
                                             device_id=(x, y, 1 - c), device_id_type=MESH).start()
        for j in range(3):
            size = packed_ref.at[pl.ds(0, nrows)]
            cp = pltpu.make_async_remote_copy(src_ref=size, dst_ref=size, send_sem=send_sems.at[j],
                                              recv_sem=recv_sems.at[j], device_id=(x, y, c), device_id_type=MESH)
            cp.wait_recv()
            cp.wait_send()

    outs = pl.pallas_call(
        body, name=f"gather_forward_{l}", in_specs=[ANY] * (1 + NG), out_specs=[ANY] * NG,
        out_shape=[jax.ShapeDtypeStruct(a.shape, a.dtype) for a in lands],
        input_output_aliases={1 + i: i for i in range(NG)},
        scratch_shapes=[pltpu.SemaphoreType.DMA((3,)), pltpu.SemaphoreType.DMA((3,))],
    )(packed, *lands)
    return dict(zip(GNAMES, outs))


RELATIONS = ((0, 0, 1), (1, 0, 0), (0, 1, 0), (1, 1, 0), (1, 0, 1), (0, 1, 1), (1, 1, 1))


def _peers():
    x, y, c = lax.axis_index("x"), lax.axis_index("y"), lax.axis_index("c")
    return [((1 - x) if fx else x, (1 - y) if fy else y, (1 - c) if fc else c) for fx, fy, fc in RELATIONS]


def _scatter_start(tag, l, slab, recv, parts, after):
    def body(slab_ref, recv_ref, after_ref, send_sems, recv_sems, slab_out, recv_out, token):
        for k, to in enumerate(_peers()):
            for r0, nr in parts:
                pltpu.make_async_remote_copy(src_ref=slab_ref.at[4 * to[0] + 2 * to[1] + to[2], l, pl.ds(r0, nr)],
                                             dst_ref=recv_ref.at[l, k, pl.ds(r0, nr)],
                                             send_sem=send_sems.at[k], recv_sem=recv_sems.at[k],
                                             device_id=to, device_id_type=MESH).start()
        token[...] = jnp.zeros_like(token)

    return pl.pallas_call(
        body, name=f"scatter_start_{tag}",
        out_shape=(pltpu.SemaphoreType.DMA((7,)), pltpu.SemaphoreType.DMA((7,)), pltpu.HBM(slab.shape, slab.dtype),
                   pltpu.HBM(recv.shape, recv.dtype), jax.ShapeDtypeStruct((8, 128), F32)),
        in_specs=(HBM, HBM, ANY), out_specs=(SEM, SEM, HBM, HBM, pl.BlockSpec(memory_space=pltpu.VMEM)),
        input_output_aliases={0: 2, 1: 3},
        compiler_params=pltpu.CompilerParams(has_side_effects=EFFECT),
    )(_in_hbm(slab), _in_hbm(recv), after)


def _scatter_wait(tag, l, send_sems, recv_sems, slab, recv, parts):
    total = sum(nr for _, nr in parts)

    def body(slab_ref, recv_ref, send_sems, recv_sems, slab_out, recv_out):
        for k, to in enumerate(_peers()):
            cp = pltpu.make_async_remote_copy(src_ref=slab_ref.at[0, l, pl.ds(0, total)], dst_ref=recv_ref.at[l, k, pl.ds(0, total)],
                                              send_sem=send_sems.at[k], recv_sem=recv_sems.at[k], device_id=to, device_id_type=MESH)
            cp.wait_send()
            cp.wait_recv()

    return pl.pallas_call(
        body, name=f"scatter_wait_{tag}",
        out_shape=(pltpu.HBM(slab.shape, slab.dtype), pltpu.HBM(recv.shape, recv.dtype)),
        in_specs=(HBM, HBM, SEM, SEM), out_specs=(HBM, HBM), input_output_aliases={0: 0, 1: 1},
        compiler_params=pltpu.CompilerParams(has_side_effects=EFFECT),
    )(slab, recv, send_sems, recv_sems)


def _gather_rows(name, src):
    dst = jax.ShapeDtypeStruct((NDEV,) + src.shape, src.dtype)
    full = (slice(None), slice(None), slice(None))
    pieces = [(0, lambda shard: (pl.ds(shard, 1), slice(None), slice(None)), full)]
    return _allgather(name, src.reshape((1,) + src.shape), [dst], pieces)[0]


FC = 1408


def _ffn_fwd(h, l, pre_g, post_g, wgu, wd):
    T = h.shape[0]
    R = min(512, T)

    def body(h_ref, pg_ref, qg_ref, wgu_ref, wd_ref, out_ref, n_ref, ab_ref, f_ref):
        hh = h_ref[...]
        xh, _ = _rms(hh)
        n = (xh * pg_ref[...]).astype(BF)
        n_ref[...] = n
        f = jnp.zeros((R, D), F32)
        for ci in range(F // FC):
            a = _dot_nt(n, wgu_ref[pl.ds(ci * FC, FC), :])
            b = _dot_nt(n, wgu_ref[pl.ds(F + ci * FC, FC), :])
            ab_ref[:, pl.ds(ci * FC, FC)] = a.astype(BF)
            ab_ref[:, pl.ds(F + ci * FC, FC)] = b.astype(BF)
            s = (a * _sigmoid(a) * b).astype(BF)
            f = f + _dot(s, wd_ref[pl.ds(ci * FC, FC), :])
        f_ref[...] = f
        fh, _ = _rms(f)
        out_ref[...] = hh + 0.5 * (fh * qg_ref[...])

    tile = lambda n: pl.BlockSpec((R, n), lambda i: (i, 0))
    return pl.pallas_call(
        body, name="ffn_fwd", grid=(T // R,),
        in_specs=[tile(D), _row(l, D), _row(l, D), _full((2 * F, D)), _full((F, D))],
        out_specs=[tile(D), tile(D), tile(2 * F), tile(D)],
        out_shape=[jax.ShapeDtypeStruct((T, D), F32), jax.ShapeDtypeStruct((T, D), BF),
                   jax.ShapeDtypeStruct((T, 2 * F), BF), jax.ShapeDtypeStruct((T, D), F32)],
        compiler_params=_cparams(("parallel",)),
    )(h, pre_g, post_g, wgu, wd)


def _ffn_bwd(dout, h, ab, f, l, pre_g, post_g, wgu, wd):
    T = h.shape[0]
    R = min(256, T)

    def body(do_ref, h_ref, ab_ref, f_ref, pg_ref, qg_ref, wgu_ref, wd_ref,
             dh_ref, dab_ref, s_ref, df_ref, dpg_ref, dqg_ref):
        i = pl.program_id(0)

        @pl.when(i == 0)
        def _():
            dpg_ref[...] = jnp.zeros_like(dpg_ref)
            dqg_ref[...] = jnp.zeros_like(dqg_ref)

        do = do_ref[...]
        fh, fr = _rms(f_ref[...])
        df, dq = _rms_bwd(fh, fr, qg_ref[...], 0.5 * do)
        dqg_ref[...] += dq
        df = df.astype(BF)
        df_ref[...] = df
        dn = jnp.zeros((R, D), F32)
        for ci in range(F // FC):
            ga, gb = pl.ds(ci * FC, FC), pl.ds(F + ci * FC, FC)
            ds = _dot_nt(df, wd_ref[ga, :])
            a = ab_ref[:, ga].astype(F32)
            b = ab_ref[:, gb].astype(F32)
            sg = _sigmoid(a)
            sil = a * sg
            s_ref[:, ga] = (sil * b).astype(BF)
            da = (ds * b * (sg * (1.0 + a * (1.0 - sg)))).astype(BF)
            db = (ds * sil).astype(BF)
            dab_ref[:, ga] = da
            dab_ref[:, gb] = db
            dn = dn + _dot(da, wgu_ref[ga, :]) + _dot(db, wgu_ref[gb, :])
        xh, xr = _rms(h_ref[...])
        dx, dp = _rms_bwd(xh, xr, pg_ref[...], dn)
        dpg_ref[...] += dp
        dh_ref[...] = do + dx

    tile = lambda n: pl.BlockSpec((R, n), lambda i: (i, 0))
    acc = pl.BlockSpec((1, D), lambda i: (0, 0))
    return pl.pallas_call(
        body, name="ffn_bwd", grid=(T // R,),
        in_specs=[tile(D), tile(D), tile(2 * F), tile(D), _row(l, D), _row(l, D), _full((2 * F, D)), _full((F, D))],
        out_specs=[tile(D), tile(2 * F), tile(F), tile(D), acc, acc],
        out_shape=[jax.ShapeDtypeStruct((T, D), F32), jax.ShapeDtypeStruct((T, 2 * F), BF),
                   jax.ShapeDtypeStruct((T, F), BF), jax.ShapeDtypeStruct((T, D), BF),
                   jax.ShapeDtypeStruct((1, D), F32), jax.ShapeDtypeStruct((1, D), F32)],
        compiler_params=_cparams(("arbitrary",)),
    )(dout, h, ab, f, pre_g, post_g, wgu, wd)


def _mix_in_fwd(h, l, pre_g, win):
    T = h.shape[0]
    R = min(256, T)

    def body(h_ref, pg_ref, w_ref, n_ref, z_ref):
        xh, _ = _rms(h_ref[...])
        n = (xh * pg_ref[...]).astype(BF)
        n_ref[...] = n
        for ci in range(C // FC):
            z_ref[:, pl.ds(ci * FC, FC)] = _dot_nt(n, w_ref[pl.ds(ci * FC, FC), :])

    tile = lambda n: pl.BlockSpec((R, n), lambda i: (i, 0))
    return pl.pallas_call(
        body, name="mix_in_fwd", grid=(T // R,),
        in_specs=[tile(D), _row(l, D), _full((C, D))],
        out_specs=[tile(D), tile(C)],
        out_shape=[jax.ShapeDtypeStruct((T, D), BF), jax.ShapeDtypeStruct((T, C), F32)],
        compiler_params=_cparams(("parallel",)),
    )(h, pre_g, win)


def _mix_in_bwd(dout, dz, h, l, pre_g, win):
    T = h.shape[0]
    R = min(512, T)

    def body(do_ref, dz_ref, h_ref, pg_ref, w_ref, dh_ref, dpg_ref):
        @pl.when(pl.program_id(0) == 0)
        def _():
            dpg_ref[...] = jnp.zeros_like(dpg_ref)

        dn = _dot(dz_ref[...], w_ref[...])
        xh, xr = _rms(h_ref[...])
        dx, dp = _rms_bwd(xh, xr, pg_ref[...], dn)
        dpg_ref[...] += dp
        dh_ref[...] = do_ref[...] + dx

    tile = lambda n: pl.BlockSpec((R, n), lambda i: (i, 0))
    return pl.pallas_call(
        body, name="mix_in_bwd", grid=(T // R,),
        in_specs=[tile(D), tile(C), tile(D), _row(l, D), _full((C, D))],
        out_specs=[tile(D), pl.BlockSpec((1, D), lambda i: (0, 0))],
        out_shape=[jax.ShapeDtypeStruct((T, D), F32), jax.ShapeDtypeStruct((1, D), F32)],
        compiler_params=_cparams(("arbitrary",)),
    )(dout, dz, h, pre_g, win)


def _mix_specs(l, R, tile_of):
    def halo(rows, col_block):
        per = R // rows
        return pl.BlockSpec((rows, PW), lambda i: (jnp.maximum(tile_of(i) * per - 1, 0), col_block))
    return [
        pl.BlockSpec((R, C), lambda i: (tile_of(i), 0)),
        halo(POOL_HALO, 0), halo(CONV_HALO, 3), halo(CONV_HALO, 4),
        pl.BlockSpec((R, D), lambda i: (tile_of(i), 0)),
        _whole(l, (4, CHUNK, CHUNK)), _row(l, PW),
        _row(l, PW), _row(l, PW), _whole(l, (4, CHUNK, CHUNK)), _whole(l, (CHUNK, 4)),
        _whole(l, (CONV_HALO, PW)), _row(l, PW), _row(l, PW), _row(l, PW),
        _row(l, D),
        _full((D, PW)), _full((D, PW)), _full((D, PW)), _full((D, D)),
    ]


SUBLANES = 8


def _phase_copies(buf, shifted, rows):
    for b in range(1, SUBLANES):
        shifted[b - 1] = buf[pl.ds(b, rows), :]


def _window(buf, shifted, off, R):
    a, b = divmod(off, SUBLANES)
    return buf[pl.ds(off, R), :] if b == 0 else shifted[b - 1, pl.ds(SUBLANES * a, R), :]


class _MixFwd:
    def __init__(self, R, tile, refs, scratch):
        (z_ref, zph_ref, zah_ref, zbh_ref, _h, pw_ref, ps_ref, lg_ref, lb_ref, ws_ref, bst_ref,
         ck_ref, cb_ref, cg_ref, cbb_ref, _qg, wpo_ref, wso_ref, wco_ref, wout_ref) = refs
        pbuf, xbuf, sbuf, xsh, pm_ref, sg_ref, cv_ref = scratch
        first = tile == 0
        tglob = tile * R + lax.broadcasted_iota(jnp.int32, (R, 1), 0)
        pbuf[pl.ds(0, POOL_HALO), :] = jnp.where(first, 0.0, zph_ref[...])
        pbuf[pl.ds(POOL_HALO, R), :] = z_ref[:, pl.ds(0, PW)]
        self.pooled, self.yg, self.cnt = [], [], []
        for gi, w in enumerate(POOL_WINDOWS):
            cols = pl.ds(gi * CHUNK, CHUNK)
            x = pbuf[pl.ds(POOL_HALO, R), cols]
            acc = x
            for j in range(1, w):
                acc = acc + pbuf[pl.ds(POOL_HALO - j, R), cols]
            cnt = jnp.minimum(tglob + 1, w).astype(F32)
            pooled = (acc / cnt - x).astype(BF)
            yg = _dot(pooled, pw_ref[gi].astype(BF))
            pm_ref[:, cols] = (yg * ps_ref[:, cols]).astype(BF)
            self.pooled.append(pooled)
            self.yg.append(yg)
            self.cnt.append(cnt)
        self.zu, self.zv = z_ref[:, pl.ds(OFF_U, PW)], z_ref[:, pl.ds(OFF_V, PW)]
        self.eu, self.ev = lax.erf(self.zu * INV_SQRT2), lax.erf(self.zv * INV_SQRT2)
        self.u = 0.5 * self.zu * (1.0 + self.eu)
        self.vh, self.vr = _ln(0.5 * self.zv * (1.0 + self.ev))
        self.vln = (self.vh * lg_ref[...] + lb_ref[...]).astype(BF)
        tt = lax.broadcasted_iota(jnp.int32, (CHUNK, CHUNK), 0)
        ss = lax.broadcasted_iota(jnp.int32, (CHUNK, CHUNK), 1)
        self.causal = tt >= ss
        self.wc = [jnp.where(self.causal, ws_ref[hd], 0.0).astype(BF) for hd in range(4)]
        for ck in range(R // CHUNK):
            for hd in range(4):
                rows, cols = pl.ds(ck * CHUNK, CHUNK), pl.ds(hd * CHUNK, CHUNK)
                blk = self.vln[ck * CHUNK:(ck + 1) * CHUNK, hd * CHUNK:(hd + 1) * CHUNK]
                sbuf[rows, cols] = _dot(self.wc[hd], blk) + bst_ref[:, pl.ds(hd, 1)]
        self.s = sbuf[...]
        sg_ref[...] = (self.u * self.s).astype(BF)
        self.za = z_ref[:, pl.ds(OFF_A, PW)]
        self.sgb = _sigmoid(z_ref[:, pl.ds(OFF_B, PW)])
        xbuf[pl.ds(0, CONV_HALO), :] = jnp.where(first, 0.0, zah_ref[...] * _sigmoid(zbh_ref[...]))
        xbuf[pl.ds(CONV_HALO, R), :] = self.za * self.sgb
        _phase_copies(xbuf, xsh, R + CONV_HALO - SUBLANES)
        y = jnp.zeros((R, PW), F32) + cb_ref[...]
        for k in range(CONV_K):
            y = y + _window(xbuf, xsh, CONV_HALO - (CONV_K - 1) + k, R) * ck_ref[pl.ds(k, 1), :]
        self.yh, self.yr = _ln(y)
        self.yl = self.yh * cg_ref[...] + cbb_ref[...]
        self.sy = _sigmoid(self.yl)
        cv_ref[...] = (self.yl * self.sy).astype(BF)
        self.g = [_sigmoid(z_ref[:, pl.ds(OFF_G + j * D, D)]) for j in range(3)]
        self.y = [_dot_nt(pm_ref[...], wpo_ref[...]), _dot_nt(sg_ref[...], wso_ref[...]), _dot_nt(cv_ref[...], wco_ref[...])]
        self.merged = (self.g[0] * self.y[0] + self.g[1] * self.y[1] + self.g[2] * self.y[2]).astype(BF)
        self.o = _dot(self.merged, wout_ref[...])


def _phase_scratch(R):
    return pltpu.VMEM((SUBLANES - 1, R + CONV_HALO - SUBLANES, PW), F32)


def _mix_scratch(R):
    return [pltpu.VMEM((R + POOL_HALO, PW), F32), pltpu.VMEM((R + CONV_HALO, PW), F32), pltpu.VMEM((R, PW), F32),
            _phase_scratch(R)]


def _mix_core_fwd(z, h, l, sm, gw):
    T = h.shape[0]
    R = min(256, T)

    def body(*refs):
        ins, out_ref, scratch = refs[:20], refs[20], refs[21:]
        fw = _MixFwd(R, pl.program_id(0), ins, scratch)
        oh, _ = _rms(fw.o)
        out_ref[...] = ins[4][...] + oh * ins[15][...]

    act = pltpu.VMEM((R, PW), BF)
    return pl.pallas_call(
        body, name="mix_core_fwd", grid=(T // R,),
        in_specs=_mix_specs(l, R, lambda i: i),
        out_specs=pl.BlockSpec((R, D), lambda i: (i, 0)),
        out_shape=jax.ShapeDtypeStruct((T, D), F32),
        scratch_shapes=_mix_scratch(R) + [act, act, act],
        compiler_params=_cparams(("arbitrary",)),
    )(z, z, z, z, h, sm["pool_w"], sm["pool_scale"], sm["sgu_ln_g"], sm["sgu_ln_b"], sm["sgu_w_s"], sm["sgu_b_sT"],
      sm["conv_k"], sm["conv_dw_b"], sm["conv_ln_g"], sm["conv_ln_b"], sm["mix_post_g"],
      gw["po"], gw["so"], gw["co"], gw["wo"])


MIX_SMALL_GRADS = (("pool_w", (4, CHUNK, CHUNK)), ("pool_scale", (1, PW)), ("sgu_ln_g", (1, PW)), ("sgu_ln_b", (1, PW)),
                   ("sgu_w_s", (4, CHUNK, CHUNK)), ("sgu_b_sT", (CHUNK, 4)), ("conv_k", (CONV_HALO, PW)),
                   ("conv_dw_b", (1, PW)), ("conv_ln_g", (1, PW)), ("conv_ln_b", (1, PW)), ("mix_post_g", (1, D)))


def _mix_core_bwd(dout, z, h, l, sm, gw):
    T = h.shape[0]
    R = min(128, T)
    nt = T // R
    tile_of = lambda i: nt - 1 - i

    def body(*refs):
        do_ref, ins = refs[0], refs[1:21]
        (dz_ref, mg_ref, dob_ref, dy0_ref, dy1_ref, dy2_ref, pm_ref, sg_ref, cv_ref,
         dpw_ref, dps_ref, dlg_ref, dlb_ref, dws_ref, dbs_ref, dck_ref, dcb_ref, dcg_ref, dcbb_ref, dqg_ref) = refs[21:41]
        pbuf, xbuf, sbuf, xsh, qbuf, dybuf, dvbuf, dysh = refs[41:]
        (_z, _zp, _za, _zb, h_ref, pw_ref, ps_ref, lg_ref, lb_ref, ws_ref, bst_ref,
         ck_ref, cb_ref, cg_ref, cbb_ref, qg_ref, wpo_ref, wso_ref, wco_ref, wout_ref) = ins
        i = pl.program_id(0)
        small = (dpw_ref, dps_ref, dlg_ref, dlb_ref, dws_ref, dbs_ref, dck_ref, dcb_ref, dcg_ref, dcbb_ref, dqg_ref)

        @pl.when(i == 0)
        def _():
            for r in small:
                r[...] = jnp.zeros_like(r)
            qbuf[pl.ds(R, POOL_HALO), :] = jnp.zeros((POOL_HALO, PW), F32)
            dybuf[pl.ds(R, CONV_HALO), :] = jnp.zeros((CONV_HALO, PW), F32)

        fw = _MixFwd(R, tile_of(i), ins, (pbuf, xbuf, sbuf, xsh, pm_ref, sg_ref, cv_ref))
        mg_ref[...] = fw.merged
        oh, orr = _rms(fw.o)
        do, dq = _rms_bwd(oh, orr, qg_ref[...], do_ref[...])
        dqg_ref[...] += dq
        do = do.astype(BF)
        dob_ref[...] = do
        dm = _dot_nt(do, wout_ref[...])
        dys = []
        for j, dyj_ref in enumerate((dy0_ref, dy1_ref, dy2_ref)):
            g = fw.g[j]
            dz_ref[:, pl.ds(OFF_G + j * D, D)] = (dm * fw.y[j] * g * (1.0 - g)).astype(BF)
            dyj = (dm * g).astype(BF)
            dyj_ref[...] = dyj
            dys.append(dyj)
        dpm = _dot(dys[0], wpo_ref[...])
        dsg = _dot(dys[1], wso_ref[...])
        dcv = _dot(dys[2], wco_ref[...])
        for gi, w in enumerate(POOL_WINDOWS):
            cols = pl.ds(gi * CHUNK, CHUNK)
            dpm_g = dpm[:, gi * CHUNK:(gi + 1) * CHUNK]
            dps_ref[:, cols] += _colsum(dpm_g * fw.yg[gi])
            dyg = (dpm_g * ps_ref[:, cols]).astype(BF)
            dpw_ref[gi] += _dot_tn(fw.pooled[gi], dyg)
            dpooled = _dot_nt(dyg, pw_ref[gi].astype(BF))
            qbuf[pl.ds(0, R), cols] = dpooled / fw.cnt[gi]
            acc = -dpooled
            for j in range(w):
                acc = acc + qbuf[pl.ds(j, R), cols]
            dz_ref[:, cols] = acc.astype(BF)
        qbuf[pl.ds(R, POOL_HALO), :] = qbuf[pl.ds(0, POOL_HALO), :]
        ds = dsg * fw.u
        du = dsg * fw.s
        for ck in range(R // CHUNK):
            for hd in range(4):
                rows, cols = pl.ds(ck * CHUNK, CHUNK), pl.ds(hd * CHUNK, CHUNK)
                ds_f = ds[ck * CHUNK:(ck + 1) * CHUNK, hd * CHUNK:(hd + 1) * CHUNK]
                ds_blk = ds_f.astype(BF)
                v_blk = fw.vln[ck * CHUNK:(ck + 1) * CHUNK, hd * CHUNK:(hd + 1) * CHUNK]
                dbs_ref[:, pl.ds(hd, 1)] += jnp.sum(ds_f, axis=1, keepdims=True)
                dws_ref[hd] += jnp.where(fw.causal, _dot_nt(ds_blk, v_blk), 0.0)
                dvbuf[rows, cols] = _dot_tn(fw.wc[hd], ds_blk)
        dgv, dg, db = _ln_bwd(fw.vh, fw.vr, lg_ref[...], dvbuf[...])
        dlg_ref[...] += dg
        dlb_ref[...] += db
        gelu_grad = lambda x, e: 0.5 * (1.0 + e) + x * jnp.exp(-0.5 * x * x) * INV_SQRT_2PI
        dz_ref[:, pl.ds(OFF_V, PW)] = (dgv * gelu_grad(fw.zv, fw.ev)).astype(BF)
        dz_ref[:, pl.ds(OFF_U, PW)] = (du * gelu_grad(fw.zu, fw.eu)).astype(BF)
        dyl = dcv * (fw.sy * (1.0 + fw.yl * (1.0 - fw.sy)))
        dy, dg, db = _ln_bwd(fw.yh, fw.yr, cg_ref[...], dyl)
        dcg_ref[...] += dg
        dcbb_ref[...] += db
        dcb_ref[...] += _colsum(dy)
        dybuf[pl.ds(0, R), :] = dy
        _phase_copies(dybuf, dysh, R + CONV_HALO - SUBLANES)
        dxg = jnp.zeros((R, PW), F32)
        for k in range(CONV_K):
            dck_ref[pl.ds(k, 1), :] += _colsum(dy * _window(xbuf, xsh, CONV_HALO - (CONV_K - 1) + k, R))
            dxg = dxg + _window(dybuf, dysh, CONV_K - 1 - k, R) * ck_ref[pl.ds(k, 1), :]
        dybuf[pl.ds(R, CONV_HALO), :] = dybuf[pl.ds(0, CONV_HALO), :]
        dz_ref[:, pl.ds(OFF_A, PW)] = (dxg * fw.sgb).astype(BF)
        dz_ref[:, pl.ds(OFF_B, PW)] = (dxg * fw.za * fw.sgb * (1.0 - fw.sgb)).astype(BF)

    tile = lambda n: pl.BlockSpec((R, n), lambda i: (tile_of(i), 0))
    small_specs = [pl.BlockSpec(shape, lambda i, nd=len(shape): (0,) * nd) for _, shape in MIX_SMALL_GRADS]
    outs = pl.pallas_call(
        body, name="mix_core_bwd", grid=(nt,),
        in_specs=[tile(D)] + _mix_specs(l, R, tile_of),
        out_specs=[tile(C), tile(D), tile(D), tile(D), tile(D), tile(D), tile(PW), tile(PW), tile(PW)] + small_specs,
        out_shape=[jax.ShapeDtypeStruct((T, C), BF)] + [jax.ShapeDtypeStruct((T, D), BF)] * 5
        + [jax.ShapeDtypeStruct((T, PW), BF)] * 3 + [jax.ShapeDtypeStruct(shape, F32) for _, shape in MIX_SMALL_GRADS],
        scratch_shapes=_mix_scratch(R) + [pltpu.VMEM((R + POOL_HALO, PW), F32), pltpu.VMEM((R + CONV_HALO, PW), F32),
                                          pltpu.VMEM((R, PW), F32), _phase_scratch(R)],
        compiler_params=_cparams(("arbitrary",)),
    )(dout, z, z, z, z, h, sm["pool_w"], sm["pool_scale"], sm["sgu_ln_g"], sm["sgu_ln_b"], sm["sgu_w_s"], sm["sgu_b_sT"],
      sm["conv_k"], sm["conv_dw_b"], sm["conv_ln_g"], sm["conv_ln_b"], sm["mix_post_g"],
      gw["po"], gw["so"], gw["co"], gw["wo"])
    return outs[:9], dict(zip([n for n, _ in MIX_SMALL_GRADS], outs[9:]))


def _ple_fwd(h, p, l, pre_g, post_g, wpr, wpg):
    T = h.shape[0]
    R = min(512, T)

    def body(h_ref, p_ref, pg_ref, qg_ref, wpr_ref, wpg_ref, out_ref, n_ref, pb_ref, gp_ref, e_ref):
        hh = h_ref[...]
        xh, _ = _rms(hh)
        n = (xh * pg_ref[...]).astype(BF)
        n_ref[...] = n
        pb = p_ref[...].astype(BF)
        pb_ref[:, pl.ds(0, PLE)] = pb
        pb_ref[:, pl.ds(PLE, PW - PLE)] = jnp.zeros((R, PW - PLE), BF)
        e = _dot_nt(pb, wpr_ref[:, pl.ds(0, PLE)])
        gp = _dot(n, wpg_ref[...])
        gp_ref[...] = gp
        e_ref[...] = e
        qh, _ = _rms(_sigmoid(gp) * e)
        out_ref[...] = hh + qh * qg_ref[...]

    tile = lambda n: pl.BlockSpec((R, n), lambda i: (i, 0))
    return pl.pallas_call(
        body, name="ple_fwd", grid=(T // R,),
        in_specs=[tile(D), pl.BlockSpec((None, None, R, PLE), lambda i: (l, 0, i, 0)), _row(l, D), _row(l, D),
                  _full((D, PW)), _full((D, D))],
        out_specs=[tile(D), tile(D), tile(PW), tile(D), tile(D)],
        out_shape=[jax.ShapeDtypeStruct((T, D), F32), jax.ShapeDtypeStruct((T, D), BF), jax.ShapeDtypeStruct((T, PW), BF),
                   jax.ShapeDtypeStruct((T, D), F32), jax.ShapeDtypeStruct((T, D), F32)],
        compiler_params=_cparams(("parallel",)),
    )(h, p, pre_g, post_g, wpr, wpg)


def _ple_bwd(dout, h, gp, e, l, pre_g, post_g, wpg):
    T = h.shape[0]
    R = min(512, T)

    def body(do_ref, h_ref, gp_ref, e_ref, pg_ref, qg_ref, wpg_ref, dh_ref, de_ref, dgp_ref, dpg_ref, dqg_ref):
        @pl.when(pl.program_id(0) == 0)
        def _():
            dpg_ref[...] = jnp.zeros_like(dpg_ref)
            dqg_ref[...] = jnp.zeros_like(dqg_ref)

        do = do_ref[...]
        g = _sigmoid(gp_ref[...])
        e = e_ref[...]
        qh, qr = _rms(g * e)
        dq, dqg = _rms_bwd(qh, qr, qg_ref[...], do)
        dqg_ref[...] += dqg
        de_ref[...] = (dq * g).astype(BF)
        dgp = (dq * e * g * (1.0 - g)).astype(BF)
        dgp_ref[...] = dgp
        dn = _dot_nt(dgp, wpg_ref[...])
        xh, xr = _rms(h_ref[...])
        dx, dp = _rms_bwd(xh, xr, pg_ref[...], dn)
        dpg_ref[...] += dp
        dh_ref[...] = do + dx

    tile = lambda n: pl.BlockSpec((R, n), lambda i: (i, 0))
    acc = pl.BlockSpec((1, D), lambda i: (0, 0))
    return pl.pallas_call(
        body, name="ple_bwd", grid=(T // R,),
        in_specs=[tile(D), tile(D), tile(D), tile(D), _row(l, D), _row(l, D), _full((D, D))],
        out_specs=[tile(D), tile(D), tile(D), acc, acc],
        out_shape=[jax.ShapeDtypeStruct((T, D), F32), jax.ShapeDtypeStruct((T, D), BF), jax.ShapeDtypeStruct((T, D), BF),
                   jax.ShapeDtypeStruct((1, D), F32), jax.ShapeDtypeStruct((1, D), F32)],
        compiler_params=_cparams(("arbitrary",)),
    )(dout, h, gp, e, pre_g, post_g, wpg)


def _loss_head(y, target):
    T = y.shape[0]
    R = min(512, T)

    def body(y_ref, t_ref, loss_ref, dy_ref):
        @pl.when(pl.program_id(0) == 0)
        def _():
            loss_ref[...] = jnp.zeros_like(loss_ref)

        err = y_ref[...] - t_ref[0]
        dy_ref[...] = err * (1.0 / D)
        loss_ref[...] += 0.5 * jnp.sum(_mean(err * err), axis=0, keepdims=True)

    tile = pl.BlockSpec((R, D), lambda i: (i, 0))
    return pl.pallas_call(
        body, name="loss_head", grid=(T // R,),
        in_specs=[tile, pl.BlockSpec((1, R, D), lambda i: (0, i, 0))],
        out_specs=[pl.BlockSpec((1, 1), lambda i: (0, 0)), tile],
        out_shape=[jax.ShapeDtypeStruct((1, 1), F32), jax.ShapeDtypeStruct((T, D), F32)],
        compiler_params=_cparams(("arbitrary",)),
    )(y, target)


def _wgrad_f(slab, a, b, l, name, a_col0=0):
    off, r = LAYOUT[name][0], LAYOUT[name][1]
    per = FC // r
    nblk = NDEV // per
    a0 = a_col0 // FC

    T = a.shape[0]
    n = b.shape[1]
    kt = min(1024, T)

    def body(a_ref, b_ref, slab_ref, out_ref, acc_ref):
        k = pl.program_id(1)

        @pl.when(k == 0)
        def _():
            acc_ref[...] = jnp.zeros_like(acc_ref)

        acc_ref[...] += _dot_tn(a_ref[...], b_ref[...])

        @pl.when(k == pl.num_programs(1) - 1)
        def _():
            out_ref[...] = acc_ref[...].reshape(per, r, n).astype(out_ref.dtype)

    return pl.pallas_call(
        body, name="wgrad_" + name, grid=(nblk, T // kt),
        in_specs=[pl.BlockSpec((kt, FC), lambda i, k: (k, i + a0)), pl.BlockSpec((kt, n), lambda i, k: (k, 0)),
                  pl.BlockSpec(memory_space=pl.ANY)],
        out_specs=pl.BlockSpec((per, None, r, n), lambda i, k: (i, l, off // r, 0)),
        out_shape=jax.ShapeDtypeStruct(slab.shape, slab.dtype),
        scratch_shapes=[pltpu.VMEM((FC, n), F32)],
        input_output_aliases={2: 0},
        compiler_params=_cparams(("parallel", "arbitrary")),
    )(a, b, slab)


def _wgrad_d(slab, a, b, l, name):
    off, r, col0, width = LAYOUT[name][:4]
    T = a.shape[0]
    n = b.shape[1]
    kt = min(1024, T)

    def body(a_ref, b_ref, slab_ref, out_ref, acc_ref):
        k = pl.program_id(0)

        @pl.when(k == 0)
        def _():
            acc_ref[...] = jnp.zeros_like(acc_ref)

        acc_ref[...] += _dot_tn(a_ref[...], b_ref[...])

        @pl.when(k == pl.num_programs(0) - 1)
        def _():
            out_ref[...] = acc_ref[...].reshape(NDEV, r, n).astype(out_ref.dtype)

    return pl.pallas_call(
        body, name="wgrad_" + name, grid=(T // kt,),
        in_specs=[pl.BlockSpec((kt, D), lambda k: (k, 0)), pl.BlockSpec((kt, n), lambda k: (k, 0)),
                  pl.BlockSpec(memory_space=pl.ANY)],
        out_specs=pl.BlockSpec((NDEV, None, r, n), lambda k: (0, l, off // r, col0 // n)),
        out_shape=jax.ShapeDtypeStruct(slab.shape, slab.dtype),
        scratch_shapes=[pltpu.VMEM((D, n), F32)],
        input_output_aliases={2: 0},
        compiler_params=_cparams(("arbitrary",)),
    )(a, b, slab)


def _adamw(w, g, m, v):
    m = ADAM_B1 * m + (1.0 - ADAM_B1) * g
    v = ADAM_B2 * v + (1.0 - ADAM_B2) * (g * g)
    m_hat = m / (1.0 - ADAM_B1 ** ADAM_STEP)
    v_hat = v / (1.0 - ADAM_B2 ** ADAM_STEP)
    delta = -ADAM_LR * (m_hat / (jnp.sqrt(v_hat) + ADAM_EPS) + ADAM_WD * w)
    return delta, m, v


def _adam_big(name, me, slab, recv, w, m, v, l0, nl, prev):
    off, rows, col0, width, tr = LAYOUT[name]
    nt = 4 if name == "w_in" else 1
    transpose = tr and name not in SWAPPED
    pblk = (rows, width // nt)
    wblk = (width, rows) if transpose else pblk
    pmap = lambda t: (off // rows, col0 // (width // nt) + t)

    def body(me_ref, s_ref, r_ref, w_ref, m_ref, v_ref, *outs):
        g_out, d_out, m_out, v_out, token = outs[-5:]
        g = s_ref[...].astype(F32)
        for k in range(NDEV - 1):
            g = g + r_ref[k].astype(F32)
        if transpose:
            g = g.T
        d, mm, vv = _adamw(w_ref[...], g, m_ref[...], v_ref[...])
        g_out[...] = g
        d_out[...] = d
        m_out[...] = mm
        v_out[...] = vv
        token[...] = jnp.zeros_like(token)

    wspec = pl.BlockSpec((None,) + wblk, lambda l, t, me: (l + l0, 0, t))
    n_prev = 0 if prev is None else 4
    grid_spec = pltpu.PrefetchScalarGridSpec(
        num_scalar_prefetch=1, grid=(nl, nt),
        in_specs=[pl.BlockSpec((None, None) + pblk, lambda l, t, me: (me[0], l + l0) + pmap(t)),
                  pl.BlockSpec((None, NDEV - 1) + pblk, lambda l, t, me: (l + l0, 0) + pmap(t)), wspec, wspec, wspec]
        + [ANY] * n_prev,
        out_specs=[wspec] * 4 + [pl.BlockSpec((8, 128), lambda l, t, me: (0, 0))])
    return pl.pallas_call(
        body, name=f"adam_{name}_{l0}", grid_spec=grid_spec,
        out_shape=[jax.ShapeDtypeStruct(w.shape, F32)] * 4 + [jax.ShapeDtypeStruct((8, 128), F32)],
        input_output_aliases={6 + i: i for i in range(n_prev)},
        compiler_params=_cparams(("arbitrary", "arbitrary")),
    )(me, slab, recv, w, m, v, *(prev or ()))


def _adam_small(gall, w, m, v):
    rows = w.shape[0]
    tr = 32

    def body(g_ref, w_ref, m_ref, v_ref, g_out, d_out, m_out, v_out):
        g = g_ref[0]
        for k in range(1, NDEV):
            g = g + g_ref[k]
        d, mm, vv = _adamw(w_ref[...], g, m_ref[...], v_ref[...])
        g_out[...] = g
        d_out[...] = d
        m_out[...] = mm
        v_out[...] = vv

    spec = pl.BlockSpec((tr, D), lambda i: (i, 0))
    return pl.pallas_call(
        body, name="adam_small", grid=(rows // tr,),
        in_specs=[pl.BlockSpec((NDEV, tr, D), lambda i: (0, i, 0)), spec, spec, spec],
        out_specs=[spec] * 4, out_shape=[jax.ShapeDtypeStruct(w.shape, F32)] * 4,
        compiler_params=_cparams(("parallel",)),
    )(gall, w, m, v)


def _adam_plain(g, w, m, v):
    def body(g_ref, w_ref, m_ref, v_ref, d_out, m_out, v_out):
        d, mm, vv = _adamw(w_ref[...], g_ref[...], m_ref[...], v_ref[...])
        d_out[...] = d
        m_out[...] = mm
        v_out[...] = vv

    vm = pl.BlockSpec(memory_space=pltpu.VMEM)
    return pl.pallas_call(
        body, name="adam_conv_k", in_specs=[vm] * 4, out_specs=[vm] * 3,
        out_shape=[jax.ShapeDtypeStruct(w.shape, F32)] * 3,
    )(g, w, m, v)


def _pad_cols(a, n):
    return jnp.pad(a, [(0, 0)] * (a.ndim - 1) + [(0, n - a.shape[-1])])


def _pack_small(d, conv_k_full):
    L = d["pool_w"].shape[0]
    gains = jnp.stack([d[n].reshape(L, D) for n in GAINS], axis=1)
    halves = [_pad_cols(d[n].reshape(L, PW), D) for n in HALVES] + [jnp.zeros((L, D), F32)]
    halves = jnp.stack(halves, axis=1)
    ck = jnp.zeros((L, 16, D), F32) if conv_k_full is None else conv_k_full.reshape(L, 16, D)
    out = jnp.concatenate([gains, halves, d["pool_w"].reshape(L, 64, D), d["sgu_w_s"].reshape(L, 64, D), ck], axis=1)
    return out.reshape(L * SMALL_ROWS, D)


def _unpack_small(a, like):
    L = a.shape[0] // SMALL_ROWS
    a = a.reshape(L, SMALL_ROWS, D)
    out = {}
    for i, n in enumerate(GAINS):
        out[n] = a[:, i, :].reshape(like[n].shape)
    for i, n in enumerate(HALVES):
        out[n] = a[:, 8 + i, :PW].reshape(like[n].shape)
    out["pool_w"] = a[:, 16:80, :].reshape(like["pool_w"].shape)
    out["sgu_w_s"] = a[:, 80:144, :].reshape(like["sgu_w_s"].shape)
    out["conv_k_full"] = a[:, 144:160, :].reshape(L, CONV_HALO, PW)
    return out


WEIGHTS = ("ffn1_pre_g", "ffn1_w_gate", "ffn1_w_up", "ffn1_w_down", "ffn1_post_g", "mix_pre_g", "w_in", "pool_w", "pool_scale",
           "w_pool_out", "sgu_ln_g", "sgu_ln_b", "sgu_w_s", "sgu_b_s", "w_sgu_out", "conv_dw_k", "conv_dw_b", "conv_ln_g",
           "conv_ln_b", "w_conv_out", "w_out", "mix_post_g", "ffn2_pre_g", "ffn2_w_gate", "ffn2_w_up", "ffn2_w_down",
           "ffn2_post_g", "ple_w_proj", "ple_pre_g", "ple_w_gate", "ple_post_g")
SMALL = GAINS + HALVES + ("pool_w", "sgu_w_s")
FFN1_ROWS = ((704, 1056),)
FFN1_ROWS_OUT = ((0, 704), (1760, ROWS - 1760))


class _Comm:
    def __init__(self, w, me):
        self.w, self.me = w, me

    def gather_prepare(self, l):
        return _prep(self.w, l, self.me, [lax.empty(GATHERED[n], BF) for n in GNAMES])

    def gather_start(self, tag, bufs, after, pieces):
        after = jnp.zeros((8, 128), F32) if after is None else after
        send_sems, recv_sems, packed, lands, token = _gather_start(tag, bufs[0], bufs[1], after, pieces)
        return (tag, send_sems, recv_sems, pieces), (packed, lands), token[0, 0]

    def gather_finish(self, state, bufs, after):
        tag, send_sems, recv_sems, pieces = state
        after = jnp.zeros((8, 128), F32) if after is None else after
        packed, lands = _gather_wait(tag, send_sems, recv_sems, bufs[0], bufs[1], after, pieces)
        gw = _gather_forward(tag, packed, lands, pieces)
        return (packed, [gw[n] for n in GNAMES]), gw

    def scatter_start(self, tag, l, slab, recv, parts, after):
        after = jnp.zeros((8, 128), F32) if after is None else after
        send_sems, recv_sems, slab, recv, token = _scatter_start(tag, l, slab, recv, parts, after)
        return (tag, l, send_sems, recv_sems, parts), slab, recv, token[0, 0]

    def gather_small(self, name, rows):
        return _gather_rows("allgather_" + name, rows)

    def scatter_finish(self, state, slab, recv):
        tag, l, send_sems, recv_sems, parts = state
        return _scatter_wait(tag, l, send_sems, recv_sems, slab, recv, parts)


def _fwd_bwd(x, p, target, w, conv_k, comm):
    L = w["w_in"].shape[0]
    T = x.shape[1]
    row = lambda a: a.reshape(L, 1, a.shape[-1])
    sm = {n: row(w[n]) for n in GAINS + ("pool_scale", "sgu_ln_g", "sgu_ln_b", "conv_dw_b", "conv_ln_g", "conv_ln_b")}
    sm.update(pool_w=w["pool_w"], sgu_w_s=w["sgu_w_s"], sgu_b_sT=w["sgu_b_s"].transpose(0, 2, 1), conv_k=conv_k)

    h = x[0]
    saved, gws = [], [None] * L
    bufs0 = comm.gather_prepare(0)
    state_a, bufs0, _ = comm.gather_start("0a", bufs0, conv_k, FFN1_PIECES)
    state_b, bufs0, _ = comm.gather_start("0b", bufs0, None, REST_PIECES)
    bufs0, gws[0] = comm.gather_finish(state_a, bufs0, None)
    for l in range(L):
        gw = gws[l]
        pre1 = sm["ffn1_pre_g"]
        if l + 1 < L:
            state, bufs, token = comm.gather_start(str(l + 1), comm.gather_prepare(l + 1), gw["gu1"], ALL_PIECES)
            pre1 = pre1 + token
        s = {"h0": h}
        h, s["n1"], s["ab1"], s["f1"] = _ffn_fwd(h, l, pre1, sm["ffn1_post_g"], gw["gu1"], gw["d1"])
        if l == 0:
            bufs0, gw = comm.gather_finish(state_b, bufs0, h)
            gws[0] = gw
        s["h1"] = h
        s["nm"], s["z"] = _mix_in_fwd(h, l, sm["mix_pre_g"], gw["win"])
        h = _mix_core_fwd(s["z"], h, l, sm, gw)
        s["h2"] = h
        h, s["n2"], s["ab2"], s["f2"] = _ffn_fwd(h, l, sm["ffn2_pre_g"], sm["ffn2_post_g"], gw["gu2"], gw["d2"])
        s["h3"] = h
        h, s["np"], s["pb"], s["gp"], s["e"] = _ple_fwd(h, p, l, sm["ple_pre_g"], sm["ple_post_g"], gw["pr"], gw["pg"])
        saved.append(s)
        if l + 1 < L:
            _, gws[l + 1] = comm.gather_finish(state, bufs, h)

    loss_part, dh = _loss_head(h, target)

    slab = lax.empty((NDEV, L, ROWS, D), BF)
    recv = lax.empty((L, NDEV - 1, ROWS, D), BF)
    sg = {n: [None] * L for n in SMALL + ("conv_k", "sgu_b_sT")}
    pending, token = [], None
    gsmall = [None] * L

    def small_rows(l, with_ffn1):
        d = {n: sg[n][l][None] for n in SMALL if n != "sgu_b_s" and (with_ffn1 or not n.startswith("ffn1"))}
        d["sgu_b_s"] = sg["sgu_b_sT"][l].T[None]
        if not with_ffn1:
            d["ffn1_pre_g"] = d["ffn1_post_g"] = jnp.zeros((1, D), F32)
        return _pack_small(d, sg["conv_k"][l][None])
    for l in reversed(range(L)):
        s, gw = saved[l], gws[l]
        post = sm["ple_post_g"] if token is None else sm["ple_post_g"] + token
        dh, de, dgp, sg["ple_pre_g"][l], sg["ple_post_g"][l] = _ple_bwd(
            dh, s["h3"], s["gp"], s["e"], l, sm["ple_pre_g"], post, gw["pg"])
        slab = _wgrad_d(slab, s["np"], dgp, l, "ple_w_gate")
        slab = _wgrad_d(slab, de, s["pb"], l, "ple_w_proj")

        dh, dab, ss, df, sg["ffn2_pre_g"][l], sg["ffn2_post_g"][l] = _ffn_bwd(
            dh, s["h2"], s["ab2"], s["f2"], l, sm["ffn2_pre_g"], sm["ffn2_post_g"], gw["gu2"], gw["d2"])
        slab = _wgrad_f(slab, dab, s["n2"], l, "ffn2_w_gate")
        slab = _wgrad_f(slab, dab, s["n2"], l, "ffn2_w_up", a_col0=F)
        slab = _wgrad_f(slab, ss, df, l, "ffn2_w_down")

        (dz, mg, dob, dy0, dy1, dy2, pm, sgv, cv), g_mix = _mix_core_bwd(dh, s["z"], s["h1"], l, sm, gw)
        for n in g_mix:
            sg[n][l] = g_mix[n]
        dh, sg["mix_pre_g"][l] = _mix_in_bwd(dh, dz, s["h1"], l, sm["mix_pre_g"], gw["win"])
        slab = _wgrad_f(slab, dz, s["nm"], l, "w_in")
        slab = _wgrad_d(slab, mg, dob, l, "w_out")
        slab = _wgrad_d(slab, dy0, pm, l, "w_pool_out")
        slab = _wgrad_d(slab, dy1, sgv, l, "w_sgu_out")
        slab = _wgrad_d(slab, dy2, cv, l, "w_conv_out")

        pre1, parts, tag = sm["ffn1_pre_g"], ((0, ROWS),), str(l)
        if l == 0:
            for st in pending:
                slab, recv = comm.scatter_finish(st, slab, recv)
            gsmall[0] = comm.gather_small("small_grads_0", small_rows(0, False))
            st, slab, recv, token = comm.scatter_start("0a", 0, slab, recv, FFN1_ROWS_OUT, gsmall[0])
            pending, pre1, parts, tag = [st], pre1 + token, FFN1_ROWS, "0b"
        dh, dab, ss, df, sg["ffn1_pre_g"][l], sg["ffn1_post_g"][l] = _ffn_bwd(
            dh, s["h0"], s["ab1"], s["f1"], l, pre1, sm["ffn1_post_g"], gw["gu1"], gw["d1"])
        slab = _wgrad_f(slab, dab, s["n1"], l, "ffn1_w_gate")
        slab = _wgrad_f(slab, dab, s["n1"], l, "ffn1_w_up", a_col0=F)
        slab = _wgrad_f(slab, ss, df, l, "ffn1_w_down")
        after = None
        if l > 0:
            for st in pending:
                slab, recv = comm.scatter_finish(st, slab, recv)
            pending = []
            after = gsmall[l] = comm.gather_small(f"small_grads_{l}", small_rows(l, True))
        st, slab, recv, token = comm.scatter_start(tag, l, slab, recv, parts, after)
        pending.append(st)
    return loss_part, dh.reshape(1, T, D), slab, recv, sg, gsmall, pending, token


def _step(x, p, target, w, m, v):
    L = w["w_in"].shape[0]
    ix, iy, ic = lax.axis_index("x"), lax.axis_index("y"), lax.axis_index("c")
    me = 4 * ix + 2 * iy + ic

    ck_local = jnp.pad(w["conv_dw_k"].reshape(L, CONV_K, 64), ((0, 0), (0, 1), (0, 0))).reshape(L * 2, D)
    ck_all = _gather_rows("allgather_conv_k", _pad_rows8(ck_local))[:, :L * 2]
    conv_k = ck_all.reshape(NDEV, L, CONV_HALO, 64).transpose(1, 2, 0, 3).reshape(L, CONV_HALO, PW)

    me_arr = me.astype(jnp.int32).reshape(1)
    comm = _Comm(w, me_arr)
    loss_part, grad_x, slab, recv, sg, gsmall, pending, token = _fwd_bwd(x, p, target, w, conv_k, comm)
    loss = lax.psum(loss_part[0, 0], ("x", "y", "c"))

    early = {}
    if L > 1:
        early = {n: _adam_big(n, me_arr, slab, recv, w[n], m[n], v[n], 1, L - 1, None) for n in BIG}
        token = token + sum(e[4][0, 0] for e in early.values())
    last = jnp.pad(sg["ffn1_pre_g"][0], ((0, 7), (0, 0))) + jnp.pad(sg["ffn1_post_g"][0], ((1, 6), (0, 0))) + token
    last = comm.gather_small("small_grads_last", last)
    for st in pending:
        slab, recv = comm.scatter_finish(st, slab, recv)
    gall = jnp.concatenate(gsmall, axis=1).at[:, 0:2, :].set(last[:, 0:2, :])
    res = {n: _adam_big(n, me_arr, slab, recv, w[n], m[n], v[n], 0, 1, early[n][:4] if early else None)[:4] for n in BIG}
    for n in SWAPPED:
        res[n] = tuple(jnp.swapaxes(a, 1, 2) for a in res[n])

    outs = _adam_small(gall, _pack_small(w, None), _pack_small(m, None), _pack_small(v, None))
    unpacked = [_unpack_small(o, w) for o in outs]
    for n in SMALL:
        res[n] = tuple(u[n] for u in unpacked)

    gk = lax.dynamic_slice_in_dim(unpacked[0]["conv_k_full"][:, :CONV_K, :], me * 64, 64, axis=2)
    shp = w["conv_dw_k"].shape
    flat = lambda a: a.reshape(L * CONV_K, 64)
    dk, mk, vk = _adam_plain(flat(gk), flat(w["conv_dw_k"]), flat(m["conv_dw_k"]), flat(v["conv_dw_k"]))
    res["conv_dw_k"] = (gk.reshape(shp), dk.reshape(shp), mk.reshape(shp), vk.reshape(shp))

    return (loss, grad_x, *[res[n][0] for n in WEIGHTS], *[res[n][1] for n in WEIGHTS],
            *[res[n][2] for n in WEIGHTS], *[res[n][3] for n in WEIGHTS])


def _pad_rows8(a):
    return jnp.pad(a, ((0, (-a.shape[0]) % 8), (0, 0)))


def kernel(x, p, ffn1_pre_g, ffn1_w_gate, ffn1_w_up, ffn1_w_down, ffn1_post_g, mix_pre_g, w_in, pool_w, pool_scale, w_pool_out, sgu_ln_g, sgu_ln_b, sgu_w_s, sgu_b_s, w_sgu_out, conv_dw_k, conv_dw_b, conv_ln_g, conv_ln_b, w_conv_out, w_out, mix_post_g, ffn2_pre_g, ffn2_w_gate, ffn2_w_up, ffn2_w_down, ffn2_post_g, ple_w_proj, ple_pre_g, ple_w_gate, ple_post_g, loss_target, m_ffn1_pre_g, m_ffn1_w_gate, m_ffn1_w_up, m_ffn1_w_down, m_ffn1_post_g, m_mix_pre_g, m_w_in, m_pool_w, m_pool_scale, m_w_pool_out, m_sgu_ln_g, m_sgu_ln_b, m_sgu_w_s, m_sgu_b_s, m_w_sgu_out, m_conv_dw_k, m_conv_dw_b, m_conv_ln_g, m_conv_ln_b, m_w_conv_out, m_w_out, m_mix_post_g, m_ffn2_pre_g, m_ffn2_w_gate, m_ffn2_w_up, m_ffn2_w_down, m_ffn2_post_g, m_ple_w_proj, m_ple_pre_g, m_ple_w_gate, m_ple_post_g, v_ffn1_pre_g, v_ffn1_w_gate, v_ffn1_w_up, v_ffn1_w_down, v_ffn1_post_g, v_mix_pre_g, v_w_in, v_pool_w, v_pool_scale, v_w_pool_out, v_sgu_ln_g, v_sgu_ln_b, v_sgu_w_s, v_sgu_b_s, v_w_sgu_out, v_conv_dw_k, v_conv_dw_b, v_conv_ln_g, v_conv_ln_b, v_w_conv_out, v_w_out, v_mix_post_g, v_ffn2_pre_g, v_ffn2_w_gate, v_ffn2_w_up, v_ffn2_w_down, v_ffn2_post_g, v_ple_w_proj, v_ple_pre_g, v_ple_w_gate, v_ple_post_g):
    args = dict(locals())
    give = lambda n, a: jnp.swapaxes(a, 1, 2) if n in SWAPPED else a
    w = {n: give(n, args[n]) for n in WEIGHTS}
    m = {n: give(n, args["m_" + n]) for n in WEIGHTS}
    v = {n: give(n, args["v_" + n]) for n in WEIGHTS}
    return _step(x, p, loss_target, w, m, v)
```

```python
import functools

import jax
import jax.numpy as jnp
from jax import lax
from jax.experimental import pallas as pl
from jax.experimental.pallas import tpu as pltpu

D = 1024
F = 2816
C = 5632
PW = 512
PLE = 256
NDEV = 8
CHUNK = 128
POOL_WINDOWS = (2, 4, 8, 16)
CONV_K = 31
POOL_HALO = 16
CONV_HALO = 32
EPS = 1e-6
OFF_U, OFF_V, OFF_A, OFF_B, OFF_G = 512, 1024, 1536, 2048, 2560

ADAM_LR, ADAM_B1, ADAM_B2, ADAM_EPS, ADAM_WD, ADAM_STEP = 0.001, 0.9, 0.999, 1e-08, 0.01, 10

BF = jnp.bfloat16
F32 = jnp.float32
VMEM_LIMIT = 56 * 1024 * 1024
MESH = pl.DeviceIdType.MESH
INV_SQRT2 = 0.7071067811865476
INV_SQRT_2PI = 0.3989422804014327

ROWS = 3328
LAYOUT = {
    "w_in": (0, 704, 0, 1024, True),
    "ffn1_w_gate": (704, 352, 0, 1024, True),
    "ffn1_w_up": (1056, 352, 0, 1024, True),
    "ffn1_w_down": (1408, 352, 0, 1024, False),
    "ffn2_w_gate": (1760, 352, 0, 1024, True),
    "ffn2_w_up": (2112, 352, 0, 1024, True),
    "ffn2_w_down": (2464, 352, 0, 1024, False),
    "w_pool_out": (2816, 128, 0, 512, True),
    "w_sgu_out": (2816, 128, 512, 512, True),
    "w_conv_out": (2944, 128, 0, 512, True),
    "ple_w_proj": (2944, 128, 512, 256, True),
    "w_out": (3072, 128, 0, 1024, False),
    "ple_w_gate": (3200, 128, 0, 1024, False),
}
BIG = tuple(LAYOUT)
SWAPPED = ("w_in", "ffn1_w_gate", "ffn1_w_up", "ffn2_w_gate", "ffn2_w_up")
GATHERED = {
    "win": (C, D), "gu1": (2 * F, D), "d1": (F, D), "gu2": (2 * F, D), "d2": (F, D),
    "po": (D, PW), "so": (D, PW), "co": (D, PW), "pr": (D, PW), "wo": (D, D), "pg": (D, D),
}
PIECES = (
    ("win", 0, 0, 704, 0, 1024), ("gu1", 0, 704, 352, 0, 1024), ("gu1", F, 1056, 352, 0, 1024),
    ("d1", 0, 1408, 352, 0, 1024), ("gu2", 0, 1760, 352, 0, 1024), ("gu2", F, 2112, 352, 0, 1024),
    ("d2", 0, 2464, 352, 0, 1024), ("po", 0, 2816, 128, 0, 512), ("so", 0, 2816, 128, 512, 512),
    ("co", 0, 2944, 128, 0, 512), ("pr", 0, 2944, 128, 512, 512), ("wo", 0, 3072, 128, 0, 1024),
    ("pg", 0, 3200, 128, 0, 1024),
)
GAINS = ("ffn1_pre_g", "ffn1_post_g", "mix_pre_g", "mix_post_g", "ffn2_pre_g", "ffn2_post_g", "ple_pre_g", "ple_post_g")
HALVES = ("pool_scale", "sgu_ln_g", "sgu_ln_b", "sgu_b_s", "conv_dw_b", "conv_ln_g", "conv_ln_b")
SMALL_ROWS = 160


def _cparams(sem=None, **kw):
    if sem is not None:
        kw["dimension_semantics"] = sem
    return pltpu.CompilerParams(vmem_limit_bytes=VMEM_LIMIT, **kw)


def _whole(l, shape):
    nd = len(shape)
    return pl.BlockSpec((None,) + tuple(shape), lambda *_: (l,) + (0,) * nd, pipeline_mode=pl.Buffered(1))


def _full(shape):
    nd = len(shape)
    return pl.BlockSpec(tuple(shape), lambda *_: (0,) * nd, pipeline_mode=pl.Buffered(1))


def _row(l, n):
    return pl.BlockSpec((None, 1, n), lambda *_: (l, 0, 0))


def _dot(a, b):
    return jnp.dot(a, b, preferred_element_type=F32)


def _dot_nt(a, b):
    return lax.dot_general(a, b, (((1,), (1,)), ((), ())), preferred_element_type=F32)


def _dot_tn(a, b):
    return lax.dot_general(a, b, (((0,), (0,)), ((), ())), preferred_element_type=F32)


def _mean(x):
    return jnp.mean(x, axis=-1, keepdims=True)


def _colsum(x):
    return jnp.sum(x, axis=0, keepdims=True)


def _rms(x):
    r = lax.rsqrt(_mean(x * x) + EPS)
    return x * r, r


def _rms_bwd(xh, r, g, dy):
    dxh = dy * g
    return r * (dxh - xh * _mean(dxh * xh)), _colsum(dy * xh)


def _ln(x):
    xc = x - _mean(x)
    r = lax.rsqrt(_mean(xc * xc) + EPS)
    return xc * r, r


def _ln_bwd(xh, r, g, dy):
    dxh = dy * g
    return r * (dxh - _mean(dxh) - xh * _mean(dxh * xh)), _colsum(dy * xh), _colsum(dy)


def _sigmoid(x):
    return jax.nn.sigmoid(x)


def _prep(w, l, me, lands):
    nb, npc = len(BIG), len(PIECES)

    def body(me_ref, *refs):
        ins, out = dict(zip(BIG, refs[:nb])), refs[nb + NG]
        land_refs, sems = refs[nb + NG + 1:nb + 2 * NG + 1], refs[-1]
        for name, (off, rows, col0, width, tr) in LAYOUT.items():
            v = ins[name][...]
            if tr and name not in SWAPPED:
                v = v.T
            out[pl.ds(off, rows), pl.ds(col0, width)] = v.astype(BF)
        out[pl.ds(2944, 128), pl.ds(768, 256)] = jnp.zeros((128, 256), BF)
        mine = [pltpu.make_async_copy(src, dst, sems.at[i]) for i, (src, dst) in enumerate(_piece_refs(out, land_refs, me_ref[0]))]
        for cp in mine:
            cp.start()
        for cp in mine:
            cp.wait()

    grid_spec = pltpu.PrefetchScalarGridSpec(
        num_scalar_prefetch=1, grid=(1,),
        in_specs=[pl.BlockSpec((None,) + w[n].shape[1:], lambda i, me: (l, 0, 0)) for n in BIG] + [ANY] * NG,
        out_specs=[pl.BlockSpec((ROWS, D), lambda i, me: (0, 0))] + [ANY] * NG,
        scratch_shapes=[pltpu.SemaphoreType.DMA((npc,))])
    outs = pl.pallas_call(
        body, name="prep", grid_spec=grid_spec,
        out_shape=[jax.ShapeDtypeStruct((ROWS, D), BF)] + [jax.ShapeDtypeStruct(a.shape, a.dtype) for a in lands],
        input_output_aliases={1 + nb + i: 1 + i for i in range(NG)},
        compiler_params=_cparams(("arbitrary",)),
    )(me, *[w[n] for n in BIG], *lands)
    return outs[0], list(outs[1:])


def _place():
    x, y, c = lax.axis_index("x"), lax.axis_index("y"), lax.axis_index("c")
    chips = [(1 - x, y), (x, 1 - y), (1 - x, 1 - y)]
    return x, y, c, chips


def _allgather(name, src, dsts, pieces):
    npc = len(pieces)

    def body(src_ref, *rest):
        outs, (send_sems, recv_sems, local_sems) = rest[:len(dsts)], rest[len(dsts):]
        x, y, c, chips = _place()
        me, sibling = (x, y, c), (x, y, 1 - c)

        def shard_of(dev):
            return 4 * dev[0] + 2 * dev[1] + dev[2]

        def copies(k, block, to, from_src):
            res = []
            for di, dst_fn, src_sl in pieces:
                dst = outs[di].at[dst_fn(shard_of(block))]
                s = src_ref.at[src_sl] if from_src else dst
                res.append(pltpu.make_async_remote_copy(src_ref=s, dst_ref=dst, send_sem=send_sems.at[k],
                                                        recv_sem=recv_sems.at[k], device_id=to, device_id_type=MESH))
            return res

        def whole(k):
            return pltpu.make_async_remote_copy(src_ref=src_ref, dst_ref=src_ref, send_sem=send_sems.at[k],
                                                recv_sem=recv_sems.at[k], device_id=me, device_id_type=MESH)

        mine = [pltpu.make_async_copy(src_ref.at[src_sl], outs[di].at[dst_fn(shard_of(me))], local_sems.at[i])
                for i, (di, dst_fn, src_sl) in enumerate(pieces)]
        for cp in mine:
            cp.start()
        for cp in copies(0, me, sibling, True):
            cp.start()
        for j, chip in enumerate(chips):
            for cp in copies(1 + j, me, (*chip, c), True):
                cp.start()
        for j, chip in enumerate(chips):
            whole(1 + j).wait_recv()
            for cp in copies(4 + j, (*chip, c), sibling, False):
                cp.start()
        for k in (0, 4, 5, 6):
            whole(k).wait_recv()
        for k in range(7):
            whole(k).wait_send()
        for cp in mine:
            cp.wait()

    any_spec = pl.BlockSpec(memory_space=pl.ANY)
    return pl.pallas_call(
        body, name=name, in_specs=[any_spec], out_specs=[any_spec] * len(dsts), out_shape=dsts,
        scratch_shapes=[pltpu.SemaphoreType.DMA((7,)), pltpu.SemaphoreType.DMA((7,)), pltpu.SemaphoreType.DMA((npc,))],
    )(src)


HBM = pl.BlockSpec(memory_space=pltpu.HBM)
SEM = pl.BlockSpec(memory_space=pltpu.SEMAPHORE)
ANY = pl.BlockSpec(memory_space=pl.ANY)
EFFECT = pltpu.SideEffectType.DATAFLOW_SIDE_EFFECTING
GNAMES = tuple(GATHERED)
NG = len(GNAMES)


def _in_hbm(a):
    return pltpu.with_memory_space_constraint(a, pltpu.HBM)


ALL_PIECES = tuple(range(len(PIECES)))
FFN1_PIECES = (1, 2, 3)
REST_PIECES = tuple(i for i in ALL_PIECES if i not in FFN1_PIECES)


def _piece_refs(packed_ref, land_refs, shard, pieces=ALL_PIECES):
    out = []
    for gname, row0, poff, prow, pcol, width in [PIECES[i] for i in pieces]:
        land = land_refs[GNAMES.index(gname)]
        out.append((packed_ref.at[pl.ds(poff, prow), pl.ds(pcol, width)], land.at[pl.ds(row0 + shard * prow, prow), :]))
    return out


def _piece_rows(pieces):
    return sum(PIECES[i][3] * PIECES[i][5] for i in pieces) // D


def _gather_start(l, packed, lands, after, pieces):
    def body(packed_ref, *rest):
        land_refs, send_sems, recv_sems = rest[:NG], rest[NG + 1], rest[NG + 2]
        token = rest[-1]
        x, y, c, chips = _place()
        targets = [(x, y, 1 - c)] + [(*chip, c) for chip in chips]
        for k, to in enumerate(targets):
            for src, dst in _piece_refs(packed_ref, land_refs, 4 * x + 2 * y + c, pieces):
                pltpu.make_async_remote_copy(src_ref=src, dst_ref=dst, send_sem=send_sems.at[k], recv_sem=recv_sems.at[k],
                                             device_id=to, device_id_type=MESH).start()
        token[...] = jnp.zeros_like(token)

    hbm = lambda a: pltpu.HBM(a.shape, a.dtype)
    outs = pl.pallas_call(
        body, name=f"gather_start_{l}",
        out_shape=(pltpu.SemaphoreType.DMA((4,)), pltpu.SemaphoreType.DMA((4,)), hbm(packed), *[hbm(a) for a in lands],
                   jax.ShapeDtypeStruct((8, 128), F32)),
        in_specs=(HBM,) * (1 + NG) + (ANY,),
        out_specs=(SEM, SEM) + (HBM,) * (1 + NG) + (pl.BlockSpec(memory_space=pltpu.VMEM),),
        input_output_aliases={i: 2 + i for i in range(1 + NG)},
        compiler_params=pltpu.CompilerParams(has_side_effects=EFFECT),
    )(_in_hbm(packed), *[_in_hbm(a) for a in lands], after)
    return outs[0], outs[1], outs[2], list(outs[3:3 + NG]), outs[-1]


def _gather_wait(l, send_sems, recv_sems, packed, lands, after, pieces):
    nrows = _piece_rows(pieces)

    def body(packed_ref, *rest):
        send_sems, recv_sems = rest[NG], rest[NG + 1]
        x, y, c, _ = _place()
        for k in range(4):
            size = packed_ref.at[pl.ds(0, nrows)]
            cp = pltpu.make_async_remote_copy(src_ref=size, dst_ref=size, send_sem=send_sems.at[k],
                                              recv_sem=recv_sems.at[k], device_id=(x, y, c), device_id_type=MESH)
            cp.wait_send()
            cp.wait_recv()

    hbm = lambda a: pltpu.HBM(a.shape, a.dtype)
    outs = pl.pallas_call(
        body, name=f"gather_wait_{l}",
        out_shape=(hbm(packed), *[hbm(a) for a in lands]),
        in_specs=(HBM,) * (1 + NG) + (SEM, SEM, ANY), out_specs=(HBM,) * (1 + NG),
        input_output_aliases={i: i for i in range(1 + NG)},
        compiler_params=pltpu.CompilerParams(has_side_effects=EFFECT),
    )(packed, *lands, send_sems, recv_sems, after)
    return outs[0], list(outs[1:])


def _gather_forward(l, packed, lands, pieces):
    nrows = _piece_rows(pieces)

    def body(packed_ref, *rest):
        land_refs, (send_sems, recv_sems) = rest[NG:2 * NG], rest[2 * NG:]
        x, y, c, chips = _place()
        for j, (cx, cy) in enumerate(chips):
            for _, rows in _piece_refs(packed_ref, land_refs, 4 * cx + 2 * cy + c, pieces):
                pltpu.make_async_remote_copy(src_ref=rows, dst_ref=rows, send_sem=send_sems.at[j], recv_sem=recv_sems.at[j],
                                             device_id=(x, y, 1 - c), device_id_type=MESH).start()
        for j in range(3):
            size = packed_ref.at[pl.ds(0, nrows)]
            cp = pltpu.make_async_remote_copy(src_ref=size, dst_ref=size, send_sem=send_sems.at[j],
                                              recv_sem=recv_sems.at[j], device_id=(x, y, c), device_id_type=MESH)
            cp.wait_recv()
            cp.wait_send()

    outs = pl.pallas_call(
        body, name=f"gather_forward_{l}", in_specs=[ANY] * (1 + NG), out_specs=[ANY] * NG,
        out_shape=[jax.ShapeDtypeStruct(a.shape, a.dtype) for a in lands],
        input_output_aliases={1 + i: i for i in range(NG)},
        scratch_shapes=[pltpu.SemaphoreType.DMA((3,)), pltpu.SemaphoreType.DMA((3,))],
    )(packed, *lands)
    return dict(zip(GNAMES, outs))


RELATIONS = ((0, 0, 1), (1, 0, 0), (0, 1, 0), (1, 1, 0), (1, 0, 1), (0, 1, 1), (1, 1, 1))


def _peers():
    x, y, c = lax.axis_index("x"), lax.axis_index("y"), lax.axis_index("c")
    return [((1 - x) if fx else x, (1 - y) if fy else y, (1 - c) if fc else c) for fx, fy, fc in RELATIONS]


def _scatter_start(tag, l, slab, recv, parts, after):
    def body(slab_ref, recv_ref, after_ref, send_sems, recv_sems, slab_out, recv_out, token):
        for k, to in enumerate(_peers()):
            for r0, nr in parts:
                pltpu.make_async_remote_copy(src_ref=slab_ref.at[4 * to[0] + 2 * to[1] + to[2], l, pl.ds(r0, nr)],
                                             dst_ref=recv_ref.at[l, k, pl.ds(r0, nr)],
                                             send_sem=send_sems.at[k], recv_sem=recv_sems.at[k],
                                             device_id=to, device_id_type=MESH).start()
        token[...] = jnp.zeros_like(token)

    return pl.pallas_call(
        body, name=f"scatter_start_{tag}",
        out_shape=(pltpu.SemaphoreType.DMA((7,)), pltpu.SemaphoreType.DMA((7,)), pltpu.HBM(slab.shape, slab.dtype),
                   pltpu.HBM(recv.shape, recv.dtype), jax.ShapeDtypeStruct((8, 128), F32)),
        in_specs=(HBM, HBM, ANY), out_specs=(SEM, SEM, HBM, HBM, pl.BlockSpec(memory_space=pltpu.VMEM)),
        input_output_aliases={0: 2, 1: 3},
        compiler_params=pltpu.CompilerParams(has_side_effects=EFFECT),
    )(_in_hbm(slab), _in_hbm(recv), after)


def _scatter_wait(tag, l, send_sems, recv_sems, slab, recv, parts):
    total = sum(nr for _, nr in parts)

    def body(slab_ref, recv_ref, send_sems, recv_sems, slab_out, recv_out):
        for k, to in enumerate(_peers()):
            cp = pltpu.make_async_remote_copy(src_ref=slab_ref.at[0, l, pl.ds(0, total)], dst_ref=recv_ref.at[l, k, pl.ds(0, total)],
                                              send_sem=send_sems.at[k], recv_sem=recv_sems.at[k], device_id=to, device_id_type=MESH)
            cp.wait_send()
            cp.wait_recv()

    return pl.pallas_call(
        body, name=f"scatter_wait_{tag}",
        out_shape=(pltpu.HBM(slab.shape, slab.dtype), pltpu.HBM(recv.shape, recv.dtype)),
        in_specs=(HBM, HBM, SEM, SEM), out_specs=(HBM, HBM), input_output_aliases={0: 0, 1: 1},
        compiler_params=pltpu.CompilerParams(has_side_effects=EFFECT),
    )(slab, recv, send_sems, recv_sems)


def _gather_rows(name, src):
    dst = jax.ShapeDtypeStruct((NDEV,) + src.shape, src.dtype)
    full = (slice(None), slice(None), slice(None))
    pieces = [(0, lambda shard: (pl.ds(shard, 1), slice(None), slice(None)), full)]
    return _allgather(name, src.reshape((1,) + src.shape), [dst], pieces)[0]


FC = 1408


def _ffn_fwd(h, l, pre_g, post_g, wgu, wd):
    T = h.shape[0]
    R = min(512, T)

    def body(h_ref, pg_ref, qg_ref, wgu_ref, wd_ref, out_ref, n_ref, ab_ref, f_ref):
        hh = h_ref[...]
        xh, _ = _rms(hh)
        n = (xh * pg_ref[...]).astype(BF)
        n_ref[...] = n
        f = jnp.zeros((R, D), F32)
        for ci in range(F // FC):
            a = _dot_nt(n, wgu_ref[pl.ds(ci * FC, FC), :])
            b = _dot_nt(n, wgu_ref[pl.ds(F + ci * FC, FC), :])
            ab_ref[:, pl.ds(ci * FC, FC)] = a.astype(BF)
            ab_ref[:, pl.ds(F + ci * FC, FC)] = b.astype(BF)
            s = (a * _sigmoid(a) * b).astype(BF)
            f = f + _dot(s, wd_ref[pl.ds(ci * FC, FC), :])
        f_ref[...] = f
        fh, _ = _rms(f)
        out_ref[...] = hh + 0.5 * (fh * qg_ref[...])

    tile = lambda n: pl.BlockSpec((R, n), lambda i: (i, 0))
    return pl.pallas_call(
        body, name="ffn_fwd", grid=(T // R,),
        in_specs=[tile(D), _row(l, D), _row(l, D), _full((2 * F, D)), _full((F, D))],
        out_specs=[tile(D), tile(D), tile(2 * F), tile(D)],
        out_shape=[jax.ShapeDtypeStruct((T, D), F32), jax.ShapeDtypeStruct((T, D), BF),
                   jax.ShapeDtypeStruct((T, 2 * F), BF), jax.ShapeDtypeStruct((T, D), F32)],
        compiler_params=_cparams(("parallel",)),
    )(h, pre_g, post_g, wgu, wd)


def _ffn_bwd(dout, h, ab, f, l, pre_g, post_g, wgu, wd):
    T = h.shape[0]
    R = min(256, T)

    def body(do_ref, h_ref, ab_ref, f_ref, pg_ref, qg_ref, wgu_ref, wd_ref,
             dh_ref, dab_ref, s_ref, df_ref, dpg_ref, dqg_ref):
        i = pl.program_id(0)

        @pl.when(i == 0)
        def _():
            dpg_ref[...] = jnp.zeros_like(dpg_ref)
            dqg_ref[...] = jnp.zeros_like(dqg_ref)

        do = do_ref[...]
        fh, fr = _rms(f_ref[...])
        df, dq = _rms_bwd(fh, fr, qg_ref[...], 0.5 * do)
        dqg_ref[...] += dq
        df = df.astype(BF)
        df_ref[...] = df
        dn = jnp.zeros((R, D), F32)
        for ci in range(F // FC):
            ga, gb = pl.ds(ci * FC, FC), pl.ds(F + ci * FC, FC)
            ds = _dot_nt(df, wd_ref[ga, :])
            a = ab_ref[:, ga].astype(F32)
            b = ab_ref[:, gb].astype(F32)
            sg = _sigmoid(a)
            sil = a * sg
            s_ref[:, ga] = (sil * b).astype(BF)
            da = (ds * b * (sg * (1.0 + a * (1.0 - sg)))).astype(BF)
            db = (ds * sil).astype(BF)
            dab_ref[:, ga] = da
            dab_ref[:, gb] = db
            dn = dn + _dot(da, wgu_ref[ga, :]) + _dot(db, wgu_ref[gb, :])
        xh, xr = _rms(h_ref[...])
        dx, dp = _rms_bwd(xh, xr, pg_ref[...], dn)
        dpg_ref[...] += dp
        dh_ref[...] = do + dx

    tile = lambda n: pl.BlockSpec((R, n), lambda i: (i, 0))
    acc = pl.BlockSpec((1, D), lambda i: (0, 0))
    return pl.pallas_call(
        body, name="ffn_bwd", grid=(T // R,),
        in_specs=[tile(D), tile(D), tile(2 * F), tile(D), _row(l, D), _row(l, D), _full((2 * F, D)), _full((F, D))],
        out_specs=[tile(D), tile(2 * F), tile(F), tile(D), acc, acc],
        out_shape=[jax.ShapeDtypeStruct((T, D), F32), jax.ShapeDtypeStruct((T, 2 * F), BF),
                   jax.ShapeDtypeStruct((T, F), BF), jax.ShapeDtypeStruct((T, D), BF),
                   jax.ShapeDtypeStruct((1, D), F32), jax.ShapeDtypeStruct((1, D), F32)],
        compiler_params=_cparams(("arbitrary",)),
    )(dout, h, ab, f, pre_g, post_g, wgu, wd)


def _mix_in_fwd(h, l, pre_g, win):
    T = h.shape[0]
    R = min(256, T)

    def body(h_ref, pg_ref, w_ref, n_ref, z_ref):
        xh, _ = _rms(h_ref[...])
        n = (xh * pg_ref[...]).astype(BF)
        n_ref[...] = n
        for ci in range(C // FC):
            z_ref[:, pl.ds(ci * FC, FC)] = _dot_nt(n, w_ref[pl.ds(ci * FC, FC), :])

    tile = lambda n: pl.BlockSpec((R, n), lambda i: (i, 0))
    return pl.pallas_call(
        body, name="mix_in_fwd", grid=(T // R,),
        in_specs=[tile(D), _row(l, D), _full((C, D))],
        out_specs=[tile(D), tile(C)],
        out_shape=[jax.ShapeDtypeStruct((T, D), BF), jax.ShapeDtypeStruct((T, C), F32)],
        compiler_params=_cparams(("parallel",)),
    )(h, pre_g, win)


def _mix_in_bwd(dout, dz, h, l, pre_g, win):
    T = h.shape[0]
    R = min(512, T)

    def body(do_ref, dz_ref, h_ref, pg_ref, w_ref, dh_ref, dpg_ref):
        @pl.when(pl.program_id(0) == 0)
        def _():
            dpg_ref[...] = jnp.zeros_like(dpg_ref)

        dn = _dot(dz_ref[...], w_ref[...])
        xh, xr = _rms(h_ref[...])
        dx, dp = _rms_bwd(xh, xr, pg_ref[...], dn)
        dpg_ref[...] += dp
        dh_ref[...] = do_ref[...] + dx

    tile = lambda n: pl.BlockSpec((R, n), lambda i: (i, 0))
    return pl.pallas_call(
        body, name="mix_in_bwd", grid=(T // R,),
        in_specs=[tile(D), tile(C), tile(D), _row(l, D), _full((C, D))],
        out_specs=[tile(D), pl.BlockSpec((1, D), lambda i: (0, 0))],
        out_shape=[jax.ShapeDtypeStruct((T, D), F32), jax.ShapeDtypeStruct((1, D), F32)],
        compiler_params=_cparams(("arbitrary",)),
    )(dout, dz, h, pre_g, win)


def _mix_specs(l, R, tile_of):
    def halo(rows, col_block):
        per = R // rows
        return pl.BlockSpec((rows, PW), lambda i: (jnp.maximum(tile_of(i) * per - 1, 0), col_block))
    return [
        pl.BlockSpec((R, C), lambda i: (tile_of(i), 0)),
        halo(POOL_HALO, 0), halo(CONV_HALO, 3), halo(CONV_HALO, 4),
        pl.BlockSpec((R, D), lambda i: (tile_of(i), 0)),
        _whole(l, (4, CHUNK, CHUNK)), _row(l, PW),
        _row(l, PW), _row(l, PW), _whole(l, (4, CHUNK, CHUNK)), _whole(l, (CHUNK, 4)),
        _whole(l, (CONV_HALO, PW)), _row(l, PW), _row(l, PW), _row(l, PW),
        _row(l, D),
        _full((D, PW)), _full((D, PW)), _full((D, PW)), _full((D, D)),
    ]


SUBLANES = 8


def _phase_copies(buf, shifted, rows):
    for b in range(1, SUBLANES):
        shifted[b - 1] = buf[pl.ds(b, rows), :]


def _window(buf, shifted, off, R):
    a, b = divmod(off, SUBLANES)
    return buf[pl.ds(off, R), :] if b == 0 else shifted[b - 1, pl.ds(SUBLANES * a, R), :]


class _MixFwd:
    def __init__(self, R, tile, refs, scratch):
        (z_ref, zph_ref, zah_ref, zbh_ref, _h, pw_ref, ps_ref, lg_ref, lb_ref, ws_ref, bst_ref,
         ck_ref, cb_ref, cg_ref, cbb_ref, _qg, wpo_ref, wso_ref, wco_ref, wout_ref) = refs
        pbuf, xbuf, sbuf, xsh, pm_ref, sg_ref, cv_ref = scratch
        first = tile == 0
        tglob = tile * R + lax.broadcasted_iota(jnp.int32, (R, 1), 0)
        pbuf[pl.ds(0, POOL_HALO), :] = jnp.where(first, 0.0, zph_ref[...])
        pbuf[pl.ds(POOL_HALO, R), :] = z_ref[:, pl.ds(0, PW)]
        self.pooled, self.yg, self.cnt = [], [], []
        for gi, w in enumerate(POOL_WINDOWS):
            cols = pl.ds(gi * CHUNK, CHUNK)
            x = pbuf[pl.ds(POOL_HALO, R), cols]
            acc = x
            for j in range(1, w):
                acc = acc + pbuf[pl.ds(POOL_HALO - j, R), cols]
            cnt = jnp.minimum(tglob + 1, w).astype(F32)
            pooled = (acc / cnt - x).astype(BF)
            yg = _dot(pooled, pw_ref[gi].astype(BF))
            pm_ref[:, cols] = (yg * ps_ref[:, cols]).astype(BF)
            self.pooled.append(pooled)
            self.yg.append(yg)
            self.cnt.append(cnt)
        self.zu, self.zv = z_ref[:, pl.ds(OFF_U, PW)], z_ref[:, pl.ds(OFF_V, PW)]
        self.eu, self.ev = lax.erf(self.zu * INV_SQRT2), lax.erf(self.zv * INV_SQRT2)
        self.u = 0.5 * self.zu * (1.0 + self.eu)
        self.vh, self.vr = _ln(0.5 * self.zv * (1.0 + self.ev))
        self.vln = (self.vh * lg_ref[...] + lb_ref[...]).astype(BF)
        tt = lax.broadcasted_iota(jnp.int32, (CHUNK, CHUNK), 0)
        ss = lax.broadcasted_iota(jnp.int32, (CHUNK, CHUNK), 1)
        self.causal = tt >= ss
        self.wc = [jnp.where(self.causal, ws_ref[hd], 0.0).astype(BF) for hd in range(4)]
        for ck in range(R // CHUNK):
            for hd in range(4):
                rows, cols = pl.ds(ck * CHUNK, CHUNK), pl.ds(hd * CHUNK, CHUNK)
                blk = self.vln[ck * CHUNK:(ck + 1) * CHUNK, hd * CHUNK:(hd + 1) * CHUNK]
                sbuf[rows, cols] = _dot(self.wc[hd], blk) + bst_ref[:, pl.ds(hd, 1)]
        self.s = sbuf[...]
        sg_ref[...] = (self.u * self.s).astype(BF)
        self.za = z_ref[:, pl.ds(OFF_A, PW)]
        self.sgb = _sigmoid(z_ref[:, pl.ds(OFF_B, PW)])
        xbuf[pl.ds(0, CONV_HALO), :] = jnp.where(first, 0.0, zah_ref[...] * _sigmoid(zbh_ref[...]))
        xbuf[pl.ds(CONV_HALO, R), :] = self.za * self.sgb
        _phase_copies(xbuf, xsh, R + CONV_HALO - SUBLANES)
        y = jnp.zeros((R, PW), F32) + cb_ref[...]
        for k in range(CONV_K):
            y = y + _window(xbuf, xsh, CONV_HALO - (CONV_K - 1) + k, R) * ck_ref[pl.ds(k, 1), :]
        self.yh, self.yr = _ln(y)
        self.yl = self.yh * cg_ref[...] + cbb_ref[...]
        self.sy = _sigmoid(self.yl)
        cv_ref[...] = (self.yl * self.sy).astype(BF)
        self.g = [_sigmoid(z_ref[:, pl.ds(OFF_G + j * D, D)]) for j in range(3)]
        self.y = [_dot_nt(pm_ref[...], wpo_ref[...]), _dot_nt(sg_ref[...], wso_ref[...]), _dot_nt(cv_ref[...], wco_ref[...])]
        self.merged = (self.g[0] * self.y[0] + self.g[1] * self.y[1] + self.g[2] * self.y[2]).astype(BF)
        self.o = _dot(self.merged, wout_ref[...])


def _phase_scratch(R):
    return pltpu.VMEM((SUBLANES - 1, R + CONV_HALO - SUBLANES, PW), F32)


def _mix_scratch(R):
    return [pltpu.VMEM((R + POOL_HALO, PW), F32), pltpu.VMEM((R + CONV_HALO, PW), F32), pltpu.VMEM((R, PW), F32),
            _phase_scratch(R)]


def _mix_core_fwd(z, h, l, sm, gw):
    T = h.shape[0]
    R = min(256, T)

    def body(*refs):
        ins, out_ref, scratch = refs[:20], refs[20], refs[21:]
        fw = _MixFwd(R, pl.program_id(0), ins, scratch)
        oh, _ = _rms(fw.o)
        out_ref[...] = ins[4][...] + oh * ins[15][...]

    act = pltpu.VMEM((R, PW), BF)
    return pl.pallas_call(
        body, name="mix_core_fwd", grid=(T // R,),
        in_specs=_mix_specs(l, R, lambda i: i),
        out_specs=pl.BlockSpec((R, D), lambda i: (i, 0)),
        out_shape=jax.ShapeDtypeStruct((T, D), F32),
        scratch_shapes=_mix_scratch(R) + [act, act, act],
        compiler_params=_cparams(("arbitrary",)),
    )(z, z, z, z, h, sm["pool_w"], sm["pool_scale"], sm["sgu_ln_g"], sm["sgu_ln_b"], sm["sgu_w_s"], sm["sgu_b_sT"],
      sm["conv_k"], sm["conv_dw_b"], sm["conv_ln_g"], sm["conv_ln_b"], sm["mix_post_g"],
      gw["po"], gw["so"], gw["co"], gw["wo"])


MIX_SMALL_GRADS = (("pool_w", (4, CHUNK, CHUNK)), ("pool_scale", (1, PW)), ("sgu_ln_g", (1, PW)), ("sgu_ln_b", (1, PW)),
                   ("sgu_w_s", (4, CHUNK, CHUNK)), ("sgu_b_sT", (CHUNK, 4)), ("conv_k", (CONV_HALO, PW)),
                   ("conv_dw_b", (1, PW)), ("conv_ln_g", (1, PW)), ("conv_ln_b", (1, PW)), ("mix_post_g", (1, D)))


def _mix_core_bwd(dout, z, h, l, sm, gw):
    T = h.shape[0]
    R = min(128, T)
    nt = T // R
    tile_of = lambda i: nt - 1 - i

    def body(*refs):
        do_ref, ins = refs[0], refs[1:21]
        (dz_ref, mg_ref, dob_ref, dy0_ref, dy1_ref, dy2_ref, pm_ref, sg_ref, cv_ref,
         dpw_ref, dps_ref, dlg_ref, dlb_ref, dws_ref, dbs_ref, dck_ref, dcb_ref, dcg_ref, dcbb_ref, dqg_ref) = refs[21:41]
        pbuf, xbuf, sbuf, xsh, qbuf, dybuf, dvbuf, dysh = refs[41:]
        (_z, _zp, _za, _zb, h_ref, pw_ref, ps_ref, lg_ref, lb_ref, ws_ref, bst_ref,
         ck_ref, cb_ref, cg_ref, cbb_ref, qg_ref, wpo_ref, wso_ref, wco_ref, wout_ref) = ins
        i = pl.program_id(0)
        small = (dpw_ref, dps_ref, dlg_ref, dlb_ref, dws_ref, dbs_ref, dck_ref, dcb_ref, dcg_ref, dcbb_ref, dqg_ref)

        @pl.when(i == 0)
        def _():
            for r in small:
                r[...] = jnp.zeros_like(r)
            qbuf[pl.ds(R, POOL_HALO), :] = jnp.zeros((POOL_HALO, PW), F32)
            dybuf[pl.ds(R, CONV_HALO), :] = jnp.zeros((CONV_HALO, PW), F32)

        fw = _MixFwd(R, tile_of(i), ins, (pbuf, xbuf, sbuf, xsh, pm_ref, sg_ref, cv_ref))
        mg_ref[...] = fw.merged
        oh, orr = _rms(fw.o)
        do, dq = _rms_bwd(oh, orr, qg_ref[...], do_ref[...])
        dqg_ref[...] += dq
        do = do.astype(BF)
        dob_ref[...] = do
        dm = _dot_nt(do, wout_ref[...])
        dys = []
        for j, dyj_ref in enumerate((dy0_ref, dy1_ref, dy2_ref)):
            g = fw.g[j]
            dz_ref[:, pl.ds(OFF_G + j * D, D)] = (dm * fw.y[j] * g * (1.0 - g)).astype(BF)
            dyj = (dm * g).astype(BF)
            dyj_ref[...] = dyj
            dys.append(dyj)
        dpm = _dot(dys[0], wpo_ref[...])
        dsg = _dot(dys[1], wso_ref[...])
        dcv = _dot(dys[2], wco_ref[...])
        for gi, w in enumerate(POOL_WINDOWS):
            cols = pl.ds(gi * CHUNK, CHUNK)
            dpm_g = dpm[:, gi * CHUNK:(gi + 1) * CHUNK]
            dps_ref[:, cols] += _colsum(dpm_g * fw.yg[gi])
            dyg = (dpm_g * ps_ref[:, cols]).astype(BF)
            dpw_ref[gi] += _dot_tn(fw.pooled[gi], dyg)
            dpooled = _dot_nt(dyg, pw_ref[gi].astype(BF))
            qbuf[pl.ds(0, R), cols] = dpooled / fw.cnt[gi]
            acc = -dpooled
            for j in range(w):
                acc = acc + qbuf[pl.ds(j, R), cols]
            dz_ref[:, cols] = acc.astype(BF)
        qbuf[pl.ds(R, POOL_HALO), :] = qbuf[pl.ds(0, POOL_HALO), :]
        ds = dsg * fw.u
        du = dsg * fw.s
        for ck in range(R // CHUNK):
            for hd in range(4):
                rows, cols = pl.ds(ck * CHUNK, CHUNK), pl.ds(hd * CHUNK, CHUNK)
                ds_f = ds[ck * CHUNK:(ck + 1) * CHUNK, hd * CHUNK:(hd + 1) * CHUNK]
                ds_blk = ds_f.astype(BF)
                v_blk = fw.vln[ck * CHUNK:(ck + 1) * CHUNK, hd * CHUNK:(hd + 1) * CHUNK]
                dbs_ref[:, pl.ds(hd, 1)] += jnp.sum(ds_f, axis=1, keepdims=True)
                dws_ref[hd] += jnp.where(fw.causal, _dot_nt(ds_blk, v_blk), 0.0)
                dvbuf[rows, cols] = _dot_tn(fw.wc[hd], ds_blk)
        dgv, dg, db = _ln_bwd(fw.vh, fw.vr, lg_ref[...], dvbuf[...])
        dlg_ref[...] += dg
        dlb_ref[...] += db
        gelu_grad = lambda x, e: 0.5 * (1.0 + e) + x * jnp.exp(-0.5 * x * x) * INV_SQRT_2PI
        dz_ref[:, pl.ds(OFF_V, PW)] = (dgv * gelu_grad(fw.zv, fw.ev)).astype(BF)
        dz_ref[:, pl.ds(OFF_U, PW)] = (du * gelu_grad(fw.zu, fw.eu)).astype(BF)
        dyl = dcv * (fw.sy * (1.0 + fw.yl * (1.0 - fw.sy)))
        dy, dg, db = _ln_bwd(fw.yh, fw.yr, cg_ref[...], dyl)
        dcg_ref[...] += dg
        dcbb_ref[...] += db
        dcb_ref[...] += _colsum(dy)
        dybuf[pl.ds(0, R), :] = dy
        _phase_copies(dybuf, dysh, R + CONV_HALO - SUBLANES)
        dxg = jnp.zeros((R, PW), F32)
        for k in range(CONV_K):
            dck_ref[pl.ds(k, 1), :] += _colsum(dy * _window(xbuf, xsh, CONV_HALO - (CONV_K - 1) + k, R))
            dxg = dxg + _window(dybuf, dysh, CONV_K - 1 - k, R) * ck_ref[pl.ds(k, 1), :]
        dybuf[pl.ds(R, CONV_HALO), :] = dybuf[pl.ds(0, CONV_HALO), :]
        dz_ref[:, pl.ds(OFF_A, PW)] = (dxg * fw.sgb).astype(BF)
        dz_ref[:, pl.ds(OFF_B, PW)] = (dxg * fw.za * fw.sgb * (1.0 - fw.sgb)).astype(BF)

    tile = lambda n: pl.BlockSpec((R, n), lambda i: (tile_of(i), 0))
    small_specs = [pl.BlockSpec(shape, lambda i, nd=len(shape): (0,) * nd) for _, shape in MIX_SMALL_GRADS]
    outs = pl.pallas_call(
        body, name="mix_core_bwd", grid=(nt,),
        in_specs=[tile(D)] + _mix_specs(l, R, tile_of),
        out_specs=[tile(C), tile(D), tile(D), tile(D), tile(D), tile(D), tile(PW), tile(PW), tile(PW)] + small_specs,
        out_shape=[jax.ShapeDtypeStruct((T, C), BF)] + [jax.ShapeDtypeStruct((T, D), BF)] * 5
        + [jax.ShapeDtypeStruct((T, PW), BF)] * 3 + [jax.ShapeDtypeStruct(shape, F32) for _, shape in MIX_SMALL_GRADS],
        scratch_shapes=_mix_scratch(R) + [pltpu.VMEM((R + POOL_HALO, PW), F32), pltpu.VMEM((R + CONV_HALO, PW), F32),
                                          pltpu.VMEM((R, PW), F32), _phase_scratch(R)],
        compiler_params=_cparams(("arbitrary",)),
    )(dout, z, z, z, z, h, sm["pool_w"], sm["pool_scale"], sm["sgu_ln_g"], sm["sgu_ln_b"], sm["sgu_w_s"], sm["sgu_b_sT"],
      sm["conv_k"], sm["conv_dw_b"], sm["conv_ln_g"], sm["conv_ln_b"], sm["mix_post_g"],
      gw["po"], gw["so"], gw["co"], gw["wo"])
    return outs[:9], dict(zip([n for n, _ in MIX_SMALL_GRADS], outs[9:]))


def _ple_fwd(h, p, l, pre_g, post_g, wpr, wpg):
    T = h.shape[0]
    R = min(512, T)

    def body(h_ref, p_ref, pg_ref, qg_ref, wpr_ref, wpg_ref, out_ref, n_ref, pb_ref, gp_ref, e_ref):
        hh = h_ref[...]
        xh, _ = _rms(hh)
        n = (xh * pg_ref[...]).astype(BF)
        n_ref[...] = n
        pb = p_ref[...].astype(BF)
        pb_ref[:, pl.ds(0, PLE)] = pb
        pb_ref[:, pl.ds(PLE, PW - PLE)] = jnp.zeros((R, PW - PLE), BF)
        e = _dot_nt(pb, wpr_ref[:, pl.ds(0, PLE)])
        gp = _dot(n, wpg_ref[...])
        gp_ref[...] = gp
        e_ref[...] = e
        qh, _ = _rms(_sigmoid(gp) * e)
        out_ref[...] = hh + qh * qg_ref[...]

    tile = lambda n: pl.BlockSpec((R, n), lambda i: (i, 0))
    return pl.pallas_call(
        body, name="ple_fwd", grid=(T // R,),
        in_specs=[tile(D), pl.BlockSpec((None, None, R, PLE), lambda i: (l, 0, i, 0)), _row(l, D), _row(l, D),
                  _full((D, PW)), _full((D, D))],
        out_specs=[tile(D), tile(D), tile(PW), tile(D), tile(D)],
        out_shape=[jax.ShapeDtypeStruct((T, D), F32), jax.ShapeDtypeStruct((T, D), BF), jax.ShapeDtypeStruct((T, PW), BF),
                   jax.ShapeDtypeStruct((T, D), F32), jax.ShapeDtypeStruct((T, D), F32)],
        compiler_params=_cparams(("parallel",)),
    )(h, p, pre_g, post_g, wpr, wpg)


def _ple_bwd(dout, h, gp, e, l, pre_g, post_g, wpg):
    T = h.shape[0]
    R = min(512, T)

    def body(do_ref, h_ref, gp_ref, e_ref, pg_ref, qg_ref, wpg_ref, dh_ref, de_ref, dgp_ref, dpg_ref, dqg_ref):
        @pl.when(pl.program_id(0) == 0)
        def _():
            dpg_ref[...] = jnp.zeros_like(dpg_ref)
            dqg_ref[...] = jnp.zeros_like(dqg_ref)

        do = do_ref[...]
        g = _sigmoid(gp_ref[...])
        e = e_ref[...]
        qh, qr = _rms(g * e)
        dq, dqg = _rms_bwd(qh, qr, qg_ref[...], do)
        dqg_ref[...] += dqg
        de_ref[...] = (dq * g).astype(BF)
        dgp = (dq * e * g * (1.0 - g)).astype(BF)
        dgp_ref[...] = dgp
        dn = _dot_nt(dgp, wpg_ref[...])
        xh, xr = _rms(h_ref[...])
        dx, dp = _rms_bwd(xh, xr, pg_ref[...], dn)
        dpg_ref[...] += dp
        dh_ref[...] = do + dx

    tile = lambda n: pl.BlockSpec((R, n), lambda i: (i, 0))
    acc = pl.BlockSpec((1, D), lambda i: (0, 0))
    return pl.pallas_call(
        body, name="ple_bwd", grid=(T // R,),
        in_specs=[tile(D), tile(D), tile(D), tile(D), _row(l, D), _row(l, D), _full((D, D))],
        out_specs=[tile(D), tile(D), tile(D), acc, acc],
        out_shape=[jax.ShapeDtypeStruct((T, D), F32), jax.ShapeDtypeStruct((T, D), BF), jax.ShapeDtypeStruct((T, D), BF),
                   jax.ShapeDtypeStruct((1, D), F32), jax.ShapeDtypeStruct((1, D), F32)],
        compiler_params=_cparams(("arbitrary",)),
    )(dout, h, gp, e, pre_g, post_g, wpg)


def _loss_head(y, target):
    T = y.shape[0]
    R = min(512, T)

    def body(y_ref, t_ref, loss_ref, dy_ref):
        @pl.when(pl.program_id(0) == 0)
        def _():
            loss_ref[...] = jnp.zeros_like(loss_ref)

        err = y_ref[...] - t_ref[0]
        dy_ref[...] = err * (1.0 / D)
        loss_ref[...] += 0.5 * jnp.sum(_mean(err * err), axis=0, keepdims=True)

    tile = pl.BlockSpec((R, D), lambda i: (i, 0))
    return pl.pallas_call(
        body, name="loss_head", grid=(T // R,),
        in_specs=[tile, pl.BlockSpec((1, R, D), lambda i: (0, i, 0))],
        out_specs=[pl.BlockSpec((1, 1), lambda i: (0, 0)), tile],
        out_shape=[jax.ShapeDtypeStruct((1, 1), F32), jax.ShapeDtypeStruct((T, D), F32)],
        compiler_params=_cparams(("arbitrary",)),
    )(y, target)


def _wgrad_f(slab, a, b, l, name, a_col0=0):
    off, r = LAYOUT[name][0], LAYOUT[name][1]
    per = FC // r
    nblk = NDEV // per
    a0 = a_col0 // FC

    T = a.shape[0]
    n = b.shape[1]
    kt = min(2048, T)

    def body(a_ref, b_ref, slab_ref, out_ref, acc_ref):
        k = pl.program_id(1)

        @pl.when(k == 0)
        def _():
            acc_ref[...] = jnp.zeros_like(acc_ref)

        acc_ref[...] += _dot_tn(a_ref[...], b_ref[...])

        @pl.when(k == pl.num_programs(1) - 1)
        def _():
            out_ref[...] = acc_ref[...].reshape(per, r, n).astype(out_ref.dtype)

    return pl.pallas_call(
        body, name="wgrad_" + name, grid=(nblk, T // kt),
        in_specs=[pl.BlockSpec((kt, FC), lambda i, k: (k, i + a0)), pl.BlockSpec((kt, n), lambda i, k: (k, 0)),
                  pl.BlockSpec(memory_space=pl.ANY)],
        out_specs=pl.BlockSpec((per, None, r, n), lambda i, k: (i, l, off // r, 0)),
        out_shape=jax.ShapeDtypeStruct(slab.shape, slab.dtype),
        scratch_shapes=[pltpu.VMEM((FC, n), F32)],
        input_output_aliases={2: 0},
        compiler_params=_cparams(("parallel", "arbitrary")),
    )(a, b, slab)


def _wgrad_d(slab, a, b, l, name):
    off, r, col0, width = LAYOUT[name][:4]
    T = a.shape[0]
    n = b.shape[1]
    kt = min(2048, T)

    def body(a_ref, b_ref, slab_ref, out_ref, acc_ref):
        k = pl.program_id(0)

        @pl.when(k == 0)
        def _():
            acc_ref[...] = jnp.zeros_like(acc_ref)

        acc_ref[...] += _dot_tn(a_ref[...], b_ref[...])

        @pl.when(k == pl.num_programs(0) - 1)
        def _():
            out_ref[...] = acc_ref[...].reshape(NDEV, r, n).astype(out_ref.dtype)

    return pl.pallas_call(
        body, name="wgrad_" + name, grid=(T // kt,),
        in_specs=[pl.BlockSpec((kt, D), lambda k: (k, 0)), pl.BlockSpec((kt, n), lambda k: (k, 0)),
                  pl.BlockSpec(memory_space=pl.ANY)],
        out_specs=pl.BlockSpec((NDEV, None, r, n), lambda k: (0, l, off // r, col0 // n)),
        out_shape=jax.ShapeDtypeStruct(slab.shape, slab.dtype),
        scratch_shapes=[pltpu.VMEM((D, n), F32)],
        input_output_aliases={2: 0},
        compiler_params=_cparams(("arbitrary",)),
    )(a, b, slab)


def _adamw(w, g, m, v):
    m = ADAM_B1 * m + (1.0 - ADAM_B1) * g
    v = ADAM_B2 * v + (1.0 - ADAM_B2) * (g * g)
    m_hat = m / (1.0 - ADAM_B1 ** ADAM_STEP)
    v_hat = v / (1.0 - ADAM_B2 ** ADAM_STEP)
    delta = -ADAM_LR * (m_hat / (jnp.sqrt(v_hat) + ADAM_EPS) + ADAM_WD * w)
    return delta, m, v


def _adam_big(name, me, slab, recv, w, m, v, l0, nl, prev):
    off, rows, col0, width, tr = LAYOUT[name]
    nt = 4 if name == "w_in" else 1
    transpose = tr and name not in SWAPPED
    pblk = (rows, width // nt)
    wblk = (width, rows) if transpose else pblk
    pmap = lambda t: (off // rows, col0 // (width // nt) + t)

    def body(me_ref, s_ref, r_ref, w_ref, m_ref, v_ref, *outs):
        g_out, d_out, m_out, v_out, token = outs[-5:]
        g = s_ref[...].astype(F32)
        for k in range(NDEV - 1):
            g = g + r_ref[k].astype(F32)
        if transpose:
            g = g.T
        d, mm, vv = _adamw(w_ref[...], g, m_ref[...], v_ref[...])
        g_out[...] = g
        d_out[...] = d
        m_out[...] = mm
        v_out[...] = vv
        token[...] = jnp.zeros_like(token)

    wspec = pl.BlockSpec((None,) + wblk, lambda l, t, me: (l + l0, 0, t))
    n_prev = 0 if prev is None else 4
    grid_spec = pltpu.PrefetchScalarGridSpec(
        num_scalar_prefetch=1, grid=(nl, nt),
        in_specs=[pl.BlockSpec((None, None) + pblk, lambda l, t, me: (me[0], l + l0) + pmap(t)),
                  pl.BlockSpec((None, NDEV - 1) + pblk, lambda l, t, me: (l + l0, 0) + pmap(t)), wspec, wspec, wspec]
        + [ANY] * n_prev,
        out_specs=[wspec] * 4 + [pl.BlockSpec((8, 128), lambda l, t, me: (0, 0))])
    return pl.pallas_call(
        body, name=f"adam_{name}_{l0}", grid_spec=grid_spec,
        out_shape=[jax.ShapeDtypeStruct(w.shape, F32)] * 4 + [jax.ShapeDtypeStruct((8, 128), F32)],
        input_output_aliases={6 + i: i for i in range(n_prev)},
        compiler_params=_cparams(("arbitrary", "arbitrary")),
    )(me, slab, recv, w, m, v, *(prev or ()))


def _adam_small(gall, w, m, v):
    rows = w.shape[0]
    tr = 32

    def body(g_ref, w_ref, m_ref, v_ref, g_out, d_out, m_out, v_out):
        g = g_ref[0]
        for k in range(1, NDEV):
            g = g + g_ref[k]
        d, mm, vv = _adamw(w_ref[...], g, m_ref[...], v_ref[...])
        g_out[...] = g
        d_out[...] = d
        m_out[...] = mm
        v_out[...] = vv

    spec = pl.BlockSpec((tr, D), lambda i: (i, 0))
    return pl.pallas_call(
        body, name="adam_small", grid=(rows // tr,),
        in_specs=[pl.BlockSpec((NDEV, tr, D), lambda i: (0, i, 0)), spec, spec, spec],
        out_specs=[spec] * 4, out_shape=[jax.ShapeDtypeStruct(w.shape, F32)] * 4,
        compiler_params=_cparams(("parallel",)),
    )(gall, w, m, v)


def _adam_plain(g, w, m, v):
    def body(g_ref, w_ref, m_ref, v_ref, d_out, m_out, v_out):
        d, mm, vv = _adamw(w_ref[...], g_ref[...], m_ref[...], v_ref[...])
        d_out[...] = d
        m_out[...] = mm
        v_out[...] = vv

    vm = pl.BlockSpec(memory_space=pltpu.VMEM)
    return pl.pallas_call(
        body, name="adam_conv_k", in_specs=[vm] * 4, out_specs=[vm] * 3,
        out_shape=[jax.ShapeDtypeStruct(w.shape, F32)] * 3,
    )(g, w, m, v)


def _pad_cols(a, n):
    return jnp.pad(a, [(0, 0)] * (a.ndim - 1) + [(0, n - a.shape[-1])])


def _pack_small(d, conv_k_full):
    L = d["pool_w"].shape[0]
    gains = jnp.stack([d[n].reshape(L, D) for n in GAINS], axis=1)
    halves = [_pad_cols(d[n].reshape(L, PW), D) for n in HALVES] + [jnp.zeros((L, D), F32)]
    halves = jnp.stack(halves, axis=1)
    ck = jnp.zeros((L, 16, D), F32) if conv_k_full is None else conv_k_full.reshape(L, 16, D)
    out = jnp.concatenate([gains, halves, d["pool_w"].reshape(L, 64, D), d["sgu_w_s"].reshape(L, 64, D), ck], axis=1)
    return out.reshape(L * SMALL_ROWS, D)


def _unpack_small(a, like):
    L = a.shape[0] // SMALL_ROWS
    a = a.reshape(L, SMALL_ROWS, D)
    out = {}
    for i, n in enumerate(GAINS):
        out[n] = a[:, i, :].reshape(like[n].shape)
    for i, n in enumerate(HALVES):
        out[n] = a[:, 8 + i, :PW].reshape(like[n].shape)
    out["pool_w"] = a[:, 16:80, :].reshape(like["pool_w"].shape)
    out["sgu_w_s"] = a[:, 80:144, :].reshape(like["sgu_w_s"].shape)
    out["conv_k_full"] = a[:, 144:160, :].reshape(L, CONV_HALO, PW)
    return out


WEIGHTS = ("ffn1_pre_g", "ffn1_w_gate", "ffn1_w_up", "ffn1_w_down", "ffn1_post_g", "mix_pre_g", "w_in", "pool_w", "pool_scale",
           "w_pool_out", "sgu_ln_g", "sgu_ln_b", "sgu_w_s", "sgu_b_s", "w_sgu_out", "conv_dw_k", "conv_dw_b", "conv_ln_g",
           "conv_ln_b", "w_conv_out", "w_out", "mix_post_g", "ffn2_pre_g", "ffn2_w_gate", "ffn2_w_up", "ffn2_w_down",
           "ffn2_post_g", "ple_w_proj", "ple_pre_g", "ple_w_gate", "ple_post_g")
SMALL = GAINS + HALVES + ("pool_w", "sgu_w_s")
FFN1_ROWS = ((704, 1056),)
FFN1_ROWS_OUT = ((0, 704), (1760, ROWS - 1760))


class _Comm:
    def __init__(self, w, me):
        self.w, self.me = w, me

    def gather_prepare(self, l):
        return _prep(self.w, l, self.me, [lax.empty(GATHERED[n], BF) for n in GNAMES])

    def gather_start(self, tag, bufs, after, pieces):
        after = jnp.zeros((8, 128), F32) if after is None else after
        send_sems, recv_sems, packed, lands, token = _gather_start(tag, bufs[0], bufs[1], after, pieces)
        return (tag, send_sems, recv_sems, pieces), (packed, lands), token[0, 0]

    def gather_finish(self, state, bufs, after):
        tag, send_sems, recv_sems, pieces = state
        after = jnp.zeros((8, 128), F32) if after is None else after
        packed, lands = _gather_wait(tag, send_sems, recv_sems, bufs[0], bufs[1], after, pieces)
        gw = _gather_forward(tag, packed, lands, pieces)
        return (packed, [gw[n] for n in GNAMES]), gw

    def scatter_start(self, tag, l, slab, recv, parts, after):
        after = jnp.zeros((8, 128), F32) if after is None else after
        send_sems, recv_sems, slab, recv, token = _scatter_start(tag, l, slab, recv, parts, after)
        return (tag, l, send_sems, recv_sems, parts), slab, recv, token[0, 0]

    def gather_small(self, name, rows):
        return _gather_rows("allgather_" + name, rows)

    def scatter_finish(self, state, slab, recv):
        tag, l, send_sems, recv_sems, parts = state
        return _scatter_wait(tag, l, send_sems, recv_sems, slab, recv, parts)


def _fwd_bwd(x, p, target, w, conv_k, comm):
    L = w["w_in"].shape[0]
    T = x.shape[1]
    row = lambda a: a.reshape(L, 1, a.shape[-1])
    sm = {n: row(w[n]) for n in GAINS + ("pool_scale", "sgu_ln_g", "sgu_ln_b", "conv_dw_b", "conv_ln_g", "conv_ln_b")}
    sm.update(pool_w=w["pool_w"], sgu_w_s=w["sgu_w_s"], sgu_b_sT=w["sgu_b_s"].transpose(0, 2, 1), conv_k=conv_k)

    h = x[0]
    saved, gws = [], [None] * L
    bufs0 = comm.gather_prepare(0)
    state_a, bufs0, _ = comm.gather_start("0a", bufs0, conv_k, FFN1_PIECES)
    state_b, bufs0, _ = comm.gather_start("0b", bufs0, None, REST_PIECES)
    bufs0, gws[0] = comm.gather_finish(state_a, bufs0, None)
    for l in range(L):
        gw = gws[l]
        pre1 = sm["ffn1_pre_g"]
        if l + 1 < L:
            state, bufs, token = comm.gather_start(str(l + 1), comm.gather_prepare(l + 1), gw["gu1"], ALL_PIECES)
            pre1 = pre1 + token
        s = {"h0": h}
        h, s["n1"], s["ab1"], s["f1"] = _ffn_fwd(h, l, pre1, sm["ffn1_post_g"], gw["gu1"], gw["d1"])
        if l == 0:
            bufs0, gw = comm.gather_finish(state_b, bufs0, h)
            gws[0] = gw
        s["h1"] = h
        s["nm"], s["z"] = _mix_in_fwd(h, l, sm["mix_pre_g"], gw["win"])
        h = _mix_core_fwd(s["z"], h, l, sm, gw)
        s["h2"] = h
        h, s["n2"], s["ab2"], s["f2"] = _ffn_fwd(h, l, sm["ffn2_pre_g"], sm["ffn2_post_g"], gw["gu2"], gw["d2"])
        s["h3"] = h
        h, s["np"], s["pb"], s["gp"], s["e"] = _ple_fwd(h, p, l, sm["ple_pre_g"], sm["ple_post_g"], gw["pr"], gw["pg"])
        saved.append(s)
        if l + 1 < L:
            _, gws[l + 1] = comm.gather_finish(state, bufs, h)

    loss_part, dh = _loss_head(h, target)

    slab = lax.empty((NDEV, L, ROWS, D), BF)
    recv = lax.empty((L, NDEV - 1, ROWS, D), BF)
    sg = {n: [None] * L for n in SMALL + ("conv_k", "sgu_b_sT")}
    pending, token = [], None
    gsmall = [None] * L

    def small_rows(l, with_ffn1):
        d = {n: sg[n][l][None] for n in SMALL if n != "sgu_b_s" and (with_ffn1 or not n.startswith("ffn1"))}
        d["sgu_b_s"] = sg["sgu_b_sT"][l].T[None]
        if not with_ffn1:
            d["ffn1_pre_g"] = d["ffn1_post_g"] = jnp.zeros((1, D), F32)
        return _pack_small(d, sg["conv_k"][l][None])
    for l in reversed(range(L)):
        s, gw = saved[l], gws[l]
        post = sm["ple_post_g"] if token is None else sm["ple_post_g"] + token
        dh, de, dgp, sg["ple_pre_g"][l], sg["ple_post_g"][l] = _ple_bwd(
            dh, s["h3"], s["gp"], s["e"], l, sm["ple_pre_g"], post, gw["pg"])
        slab = _wgrad_d(slab, s["np"], dgp, l, "ple_w_gate")
        slab = _wgrad_d(slab, de, s["pb"], l, "ple_w_proj")

        dh, dab, ss, df, sg["ffn2_pre_g"][l], sg["ffn2_post_g"][l] = _ffn_bwd(
            dh, s["h2"], s["ab2"], s["f2"], l, sm["ffn2_pre_g"], sm["ffn2_post_g"], gw["gu2"], gw["d2"])
        slab = _wgrad_f(slab, dab, s["n2"], l, "ffn2_w_gate")
        slab = _wgrad_f(slab, dab, s["n2"], l, "ffn2_w_up", a_col0=F)
        slab = _wgrad_f(slab, ss, df, l, "ffn2_w_down")

        (dz, mg, dob, dy0, dy1, dy2, pm, sgv, cv), g_mix = _mix_core_bwd(dh, s["z"], s["h1"], l, sm, gw)
        for n in g_mix:
            sg[n][l] = g_mix[n]
        dh, sg["mix_pre_g"][l] = _mix_in_bwd(dh, dz, s["h1"], l, sm["mix_pre_g"], gw["win"])
        slab = _wgrad_f(slab, dz, s["nm"], l, "w_in")
        slab = _wgrad_d(slab, mg, dob, l, "w_out")
        slab = _wgrad_d(slab, dy0, pm, l, "w_pool_out")
        slab = _wgrad_d(slab, dy1, sgv, l, "w_sgu_out")
        slab = _wgrad_d(slab, dy2, cv, l, "w_conv_out")

        pre1, parts, tag = sm["ffn1_pre_g"], ((0, ROWS),), str(l)
        if l == 0:
            for st in pending:
                slab, recv = comm.scatter_finish(st, slab, recv)
            gsmall[0] = comm.gather_small("small_grads_0", small_rows(0, False))
            st, slab, recv, token = comm.scatter_start("0a", 0, slab, recv, FFN1_ROWS_OUT, gsmall[0])
            pending, pre1, parts, tag = [st], pre1 + token, FFN1_ROWS, "0b"
        dh, dab, ss, df, sg["ffn1_pre_g"][l], sg["ffn1_post_g"][l] = _ffn_bwd(
            dh, s["h0"], s["ab1"], s["f1"], l, pre1, sm["ffn1_post_g"], gw["gu1"], gw["d1"])
        slab = _wgrad_f(slab, dab, s["n1"], l, "ffn1_w_gate")
        slab = _wgrad_f(slab, dab, s["n1"], l, "ffn1_w_up", a_col0=F)
        slab = _wgrad_f(slab, ss, df, l, "ffn1_w_down")
        after = None
        if l > 0:
            for st in pending:
                slab, recv = comm.scatter_finish(st, slab, recv)
            pending = []
            after = gsmall[l] = comm.gather_small(f"small_grads_{l}", small_rows(l, True))
        st, slab, recv, token = comm.scatter_start(tag, l, slab, recv, parts, after)
        pending.append(st)
    return loss_part, dh.reshape(1, T, D), slab, recv, sg, gsmall, pending, token


def _step(x, p, target, w, m, v):
    L = w["w_in"].shape[0]
    ix, iy, ic = lax.axis_index("x"), lax.axis_index("y"), lax.axis_index("c")
    me = 4 * ix + 2 * iy + ic

    ck_local = jnp.pad(w["conv_dw_k"].reshape(L, CONV_K, 64), ((0, 0), (0, 1), (0, 0))).reshape(L * 2, D)
    ck_all = _gather_rows("allgather_conv_k", _pad_rows8(ck_local))[:, :L * 2]
    conv_k = ck_all.reshape(NDEV, L, CONV_HALO, 64).transpose(1, 2, 0, 3).reshape(L, CONV_HALO, PW)

    me_arr = me.astype(jnp.int32).reshape(1)
    comm = _Comm(w, me_arr)
    loss_part, grad_x, slab, recv, sg, gsmall, pending, token = _fwd_bwd(x, p, target, w, conv_k, comm)
    loss = lax.psum(loss_part[0, 0], ("x", "y", "c"))

    early = {}
    if L > 1:
        early = {n: _adam_big(n, me_arr, slab, recv, w[n], m[n], v[n], 1, L - 1, None) for n in BIG}
        token = token + sum(e[4][0, 0] for e in early.values())
    last = jnp.pad(sg["ffn1_pre_g"][0], ((0, 7), (0, 0))) + jnp.pad(sg["ffn1_post_g"][0], ((1, 6), (0, 0))) + token
    last = comm.gather_small("small_grads_last", last)
    for st in pending:
        slab, recv = comm.scatter_finish(st, slab, recv)
    gall = jnp.concatenate(gsmall, axis=1).at[:, 0:2, :].set(last[:, 0:2, :])
    res = {n: _adam_big(n, me_arr, slab, recv, w[n], m[n], v[n], 0, 1, early[n][:4] if early else None)[:4] for n in BIG}
    for n in SWAPPED:
        res[n] = tuple(jnp.swapaxes(a, 1, 2) for a in res[n])

    outs = _adam_small(gall, _pack_small(w, None), _pack_small(m, None), _pack_small(v, None))
    unpacked = [_unpack_small(o, w) for o in outs]
    for n in SMALL:
        res[n] = tuple(u[n] for u in unpacked)

    gk = lax.dynamic_slice_in_dim(unpacked[0]["conv_k_full"][:, :CONV_K, :], me * 64, 64, axis=2)
    shp = w["conv_dw_k"].shape
    flat = lambda a: a.reshape(L * CONV_K, 64)
    dk, mk, vk = _adam_plain(flat(gk), flat(w["conv_dw_k"]), flat(m["conv_dw_k"]), flat(v["conv_dw_k"]))
    res["conv_dw_k"] = (gk.reshape(shp), dk.reshape(shp), mk.reshape(shp), vk.reshape(shp))

    return (loss, grad_x, *[res[n][0] for n in WEIGHTS], *[res[n][1] for n in WEIGHTS],
            *[res[n][2] for n in WEIGHTS], *[res[n][3] for n in WEIGHTS])


def _pad_rows8(a):
    return jnp.pad(a, ((0, (-a.shape[0]) % 8), (0, 0)))


def kernel(x, p, ffn1_pre_g, ffn1_w_gate, ffn1_w_up, ffn1_w_down, ffn1_post_g, mix_pre_g, w_in, pool_w, pool_scale, w_pool_out, sgu_ln_g, sgu_ln_b, sgu_w_s, sgu_b_s, w_sgu_out, conv_dw_k, conv_dw_b, conv_ln_g, conv_ln_b, w_conv_out, w_out, mix_post_g, ffn2_pre_g, ffn2_w_gate, ffn2_w_up, ffn2_w_down, ffn2_post_g, ple_w_proj, ple_pre_g, ple_w_gate, ple_post_g, loss_target, m_ffn1_pre_g, m_ffn1_w_gate, m_ffn1_w_up, m_ffn1_w_down, m_ffn1_post_g, m_mix_pre_g, m_w_in, m_pool_w, m_pool_scale, m_w_pool_out, m_sgu_ln_g, m_sgu_ln_b, m_sgu_w_s, m_sgu_b_s, m_w_sgu_out, m_conv_dw_k, m_conv_dw_b, m_conv_ln_g, m_conv_ln_b, m_w_conv_out, m_w_out, m_mix_post_g, m_ffn2_pre_g, m_ffn2_w_gate, m_ffn2_w_up, m_ffn2_w_down, m_ffn2_post_g, m_ple_w_proj, m_ple_pre_g, m_ple_w_gate, m_ple_post_g, v_ffn1_pre_g, v_ffn1_w_gate, v_ffn1_w_up, v_ffn1_w_down, v_ffn1_post_g, v_mix_pre_g, v_w_in, v_pool_w, v_pool_scale, v_w_pool_out, v_sgu_ln_g, v_sgu_ln_b, v_sgu_w_s, v_sgu_b_s, v_w_sgu_out, v_conv_dw_k, v_conv_dw_b, v_conv_ln_g, v_conv_ln_b, v_w_conv_out, v_w_out, v_mix_post_g, v_ffn2_pre_g, v_ffn2_w_gate, v_ffn2_w_up, v_ffn2_w_down, v_ffn2_post_g, v_ple_w_proj, v_ple_pre_g, v_ple_w_gate, v_ple_post_g):
    args = dict(locals())
    give = lambda n, a: jnp.swapaxes(a, 1, 2) if n in SWAPPED else a
    w = {n: give(n, args[n]) for n in WEIGHTS}
    m = {n: give(n, args["m_" + n]) for n in WEIGHTS}
    v = {n: give(n, args["v_" + n]) for n in WEIGHTS}
    return _step(x, p, loss_target, w, m, v)
```
